```python
import math
import jax, jax.numpy as jnp
from jax import lax
import numpy as np

D_MODEL = 1024
BATCH = 4
SEQ = 8192
DEPTH = 4

HEAD_DIM = 64
MIX_WIDTH = D_MODEL
ATT_HEADS = (3 * MIX_WIDTH) // (8 * HEAD_DIM)
ATT_WIDTH = ATT_HEADS * HEAD_DIM
DN_HEADS = (3 * MIX_WIDTH) // (8 * HEAD_DIM)
DN_WIDTH = DN_HEADS * HEAD_DIM
POOL_WIDTH = MIX_WIDTH - ATT_WIDTH - DN_WIDTH
POOL_WINDOWS = (2, 4, 8, 16)
POOL_GROUPS = len(POOL_WINDOWS)
POOL_GDIM = POOL_WIDTH // POOL_GROUPS
DILATED_GROUPS = ((128, 1), (512, 4), (2048, 16))
ATT_BLOCK = 128
CONV_K = 4
DN_CHUNK = 64
MEM_LEN = 256
X_HEADS = 4
X_HEAD_DIM = D_MODEL // X_HEADS
D_FF = 2816
N_EXPERTS = 8
TOP_K = 2
D_EXPERT = 3584
MOE_BLOCK = 256
N_DENSE = (DEPTH + 1) // 2
N_MOE = DEPTH // 2
DEEPNORM_ALPHA = (2 * DEPTH) ** 0.25
DEEPNORM_BETA = (8 * DEPTH) ** -0.25
LN_EPS = 1e-5
NORM_EPS = 1e-6

OFF_ATT = 0
OFF_DN = OFF_ATT + 3 * ATT_WIDTH
OFF_BETA = OFF_DN + 3 * DN_WIDTH
OFF_DECAY = OFF_BETA + DN_HEADS
OFF_GATE = OFF_DECAY + DN_HEADS
OFF_POOL = OFF_GATE + DN_WIDTH
IN_COLS = OFF_POOL + POOL_WIDTH

kernel_name = "hybrid_dilated_deltanet_pool_moe_trunk"


def _layer_norm(x, g, b):
    xf = x.astype(jnp.float32)
    mu = xf.mean(-1, keepdims=True)
    var = jnp.square(xf - mu).mean(-1, keepdims=True)
    return ((xf - mu) * lax.rsqrt(var + LN_EPS) * g + b).astype(x.dtype)


def _rms_norm(x, w):
    xf = x.astype(jnp.float32)
    return xf * lax.rsqrt(jnp.mean(jnp.square(xf), -1, keepdims=True) + NORM_EPS) * w


def _l2norm(x):
    return x * lax.rsqrt(jnp.sum(jnp.square(x), -1, keepdims=True) + NORM_EPS)


def _swiglu(x, w1, w3, w2):
    return (jax.nn.silu(x @ w1) * (x @ w3)) @ w2


def _alibi_slopes(n):
    return jnp.exp2(-8.0 * jnp.arange(1, n + 1, dtype=jnp.float32) / n)


def _dilated_branch(q, k, v, pos, slopes, window, dilation):
    B, S, H, E = q.shape
    L = S // dilation
    span = window // dilation
    nb = -(-L // ATT_BLOCK)
    Lp = nb * ATT_BLOCK

    def to_sub(t):
        t = t.reshape(B, L, dilation, H, E).transpose(0, 2, 3, 1, 4)
        return jnp.pad(t, ((0, 0), (0, 0), (0, 0), (0, Lp - L), (0, 0)))

    def key_windows(t):
        t = jnp.pad(t, ((0, 0), (0, 0), (0, 0), (ATT_BLOCK, 0), (0, 0)))
        t = t.reshape(B, dilation, H, nb + 1, ATT_BLOCK, E)
        return jnp.concatenate([t[:, :, :, :-1], t[:, :, :, 1:]], axis=4)

    qb = to_sub(q).reshape(B, dilation, H, nb, ATT_BLOCK, E)
    kw = key_windows(to_sub(k))
    vw = key_windows(to_sub(v))

    ps = jnp.pad(pos.reshape(B, L, dilation).transpose(0, 2, 1), ((0, 0), (0, 0), (0, Lp - L)))
    pq = ps.reshape(B, dilation, nb, ATT_BLOCK)
    pkp = jnp.pad(ps, ((0, 0), (0, 0), (ATT_BLOCK, 0))).reshape(B, dilation, nb + 1, ATT_BLOCK)
    pk = jnp.concatenate([pkp[:, :, :-1], pkp[:, :, 1:]], axis=3)

    a = jnp.arange(ATT_BLOCK)[:, None]
    c = jnp.arange(2 * ATT_BLOCK)[None, :]
    rel = ATT_BLOCK + a - c
    key_idx = (jnp.arange(nb)[:, None, None] - 1) * ATT_BLOCK + c[None]
    valid = (rel >= 0) & (rel <= span) & (key_idx >= 0)

    dist = (pq[..., :, None] - pk[..., None, :]).astype(jnp.float32)
    s = jnp.einsum('bdhnqe,bdhnke->bdhnqk', qb, kw) * (E ** -0.5)
    s = s - slopes[:, None, None, None] * dist[:, :, None]
    s = jnp.where(valid, s, -jnp.inf)
    m = s.max(-1, keepdims=True)
    p = jnp.exp(s - m)
    den = p.sum(-1, keepdims=True)
    o = jnp.einsum('bdhnqk,bdhnke->bdhnqe', p, vw)

    def from_sub(t):
        e = t.shape[-1]
        t = t.reshape(B, dilation, H, Lp, e)[:, :, :, :L]
        return t.transpose(0, 3, 1, 2, 4).reshape(B, S, H, e)

    return from_sub(m), from_sub(den), from_sub(o)


def _dilated_attention(q, k, v, pos):
    slopes = _alibi_slopes(q.shape[2])
    parts = [_dilated_branch(q, k, v, pos, slopes, w, d) for (w, d) in DILATED_GROUPS]
    m_all = jnp.max(jnp.stack([p[0] for p in parts], 0), axis=0)
    num = sum(jnp.exp(m - m_all) * o for (m, _, o) in parts)
    den = sum(jnp.exp(m - m_all) * s for (m, s, _) in parts)
    return num / den


def _causal_conv(u, w):
    K = w.shape[0]
    S = u.shape[1]
    up = jnp.pad(u, ((0, 0), (K - 1, 0), (0, 0)))
    return sum(up[:, j:j + S] * w[j] for j in range(K))


def _gated_delta_rule(q, k, v, g, beta):
    B, S, H, DK = q.shape
    DV = v.shape[-1]
    N = S // DN_CHUNK

    def chunks(t):
        return t.reshape(B, N, DN_CHUNK, H, t.shape[-1]).transpose(0, 3, 1, 2, 4)

    q = chunks(q) * (DK ** -0.5)
    k = chunks(k)
    v = chunks(v)
    g = g.reshape(B, N, DN_CHUNK, H).transpose(0, 3, 1, 2)
    beta = beta.reshape(B, N, DN_CHUNK, H).transpose(0, 3, 1, 2)
    G = jnp.cumsum(g, axis=-1)
    idx = jnp.arange(DN_CHUNK)
    causal = idx[:, None] >= idx[None, :]
    strict = idx[:, None] > idx[None, :]
    decay = jnp.exp(jnp.where(causal, G[..., :, None] - G[..., None, :], -jnp.inf))
    kk = jnp.einsum('bhnik,bhnjk->bhnij', k, k)
    a_mat = jnp.where(strict, beta[..., :, None] * kk * decay, 0.0) + jnp.eye(DN_CHUNK, dtype=jnp.float32)
    u = lax.linalg.triangular_solve(a_mat, v * beta[..., None], left_side=True, lower=True, unit_diagonal=True)
    w = lax.linalg.triangular_solve(a_mat, k * (beta * jnp.exp(G))[..., None], left_side=True, lower=True,
                                    unit_diagonal=True)
    qk = jnp.einsum('bhnik,bhnjk->bhnij', q, k) * decay
    q_dec = q * jnp.exp(G)[..., None]
    k_dec = k * jnp.exp(G[..., -1:] - G)[..., None]
    g_tot = jnp.exp(G[..., -1])

    def step(state, inp):
        u_n, w_n, qd_n, kd_n, qk_n, gt_n = inp
        v_new = u_n - jnp.einsum('bhck,bhkv->bhcv', w_n, state)
        o_n = jnp.einsum('bhck,bhkv->bhcv', qd_n, state) + jnp.einsum('bhij,bhjv->bhiv', qk_n, v_new)
        state = gt_n[..., None, None] * state + jnp.einsum('bhck,bhcv->bhkv', kd_n, v_new)
        return state, o_n

    xs = tuple(jnp.moveaxis(t, 2, 0) for t in (u, w, q_dec, k_dec, qk, g_tot))
    state0 = jnp.zeros((B, H, DK, DV), jnp.float32)
    _, o = lax.scan(step, state0, xs)
    return o.transpose(1, 0, 3, 2, 4).reshape(B, S, H, DV)


def _multiscale_pool(u, w_group, scale):
    B, S, _ = u.shape
    uf = u.astype(jnp.float32).reshape(B, S, POOL_GROUPS, POOL_GDIM)
    cs = jnp.pad(jnp.cumsum(uf, axis=1), ((0, 0), (1, 0), (0, 0), (0, 0)))
    t = jnp.arange(S)
    outs = []
    for gi, win in enumerate(POOL_WINDOWS):
        cs_g = cs[:, :, gi]
        lag = jnp.maximum(t + 1 - win, 0)
        cnt = jnp.minimum(t + 1, win).astype(jnp.float32)[None, :, None]
        outs.append((cs_g[:, 1:] - cs_g[:, lag]) / cnt - uf[:, :, gi])
    pooled = jnp.stack(outs, axis=2)
    mixed = jnp.einsum('bsgc,gcd->bsgd', pooled, w_group.astype(jnp.float32))
    return mixed.reshape(B, S, POOL_WIDTH) * scale


def _hybrid_mixer(x, positions, w_in, conv_w, a_log, dt_bias, dn_norm_w, pool_w, pool_scale, w_out):
    B, S, _ = x.shape
    f32 = jnp.float32
    h = x @ w_in
    qa, ka, va = [h[..., OFF_ATT + i * ATT_WIDTH:OFF_ATT + (i + 1) * ATT_WIDTH]
                  .reshape(B, S, ATT_HEADS, HEAD_DIM).astype(f32) for i in range(3)]
    att = _dilated_attention(qa, ka, va, positions).reshape(B, S, ATT_WIDTH)
    qkv = jax.nn.silu(_causal_conv(h[..., OFF_DN:OFF_DN + 3 * DN_WIDTH], conv_w).astype(f32))
    qd, kd, vd = [qkv[..., i * DN_WIDTH:(i + 1) * DN_WIDTH].reshape(B, S, DN_HEADS, HEAD_DIM) for i in range(3)]
    qd = _l2norm(qd)
    kd = _l2norm(kd)
    beta = jax.nn.sigmoid(h[..., OFF_BETA:OFF_BETA + DN_HEADS].astype(f32))
    g = -jnp.exp(a_log.astype(f32)) * jax.nn.softplus(h[..., OFF_DECAY:OFF_DECAY + DN_HEADS].astype(f32)
                                                      + dt_bias.astype(f32))
    od = _gated_delta_rule(qd, kd, vd, g, beta)
    gate = h[..., OFF_GATE:OFF_GATE + DN_WIDTH].astype(f32).reshape(B, S, DN_HEADS, HEAD_DIM)
    dn = (_rms_norm(od, dn_norm_w) * jax.nn.silu(gate)).reshape(B, S, DN_WIDTH)
    pool = _multiscale_pool(h[..., OFF_POOL:OFF_POOL + POOL_WIDTH], pool_w, pool_scale)
    mixed = jnp.concatenate([att.astype(x.dtype), dn.astype(x.dtype), pool.astype(x.dtype)], axis=-1)
    return mixed @ w_out


def _cross_attention(x, mem, wq, wk, wv, wo):
    B, S, D = x.shape
    M = mem.shape[1]
    q = (x @ wq).reshape(B, S, X_HEADS, X_HEAD_DIM)
    k = (mem @ wk).reshape(B, M, X_HEADS, X_HEAD_DIM)
    v = (mem @ wv).reshape(B, M, X_HEADS, X_HEAD_DIM)
    s = jnp.einsum('bshe,bmhe->bhsm', q, k).astype(jnp.float32) * (X_HEAD_DIM ** -0.5)
    p = jax.nn.softmax(s, axis=-1).astype(x.dtype)
    o = jnp.einsum('bhsm,bmhe->bshe', p, v).reshape(B, S, D)
    return o @ wo


def _moe_swiglu(x, router_w, w1, w3, w2):
    B, S, D = x.shape
    T = B * S
    A = T * TOP_K
    xt = x.reshape(T, D)
    logits = (xt @ router_w).astype(jnp.float32)
    top_logit, top_e = lax.top_k(logits, TOP_K)
    gate = jax.nn.softmax(top_logit, axis=-1)
    flat_e = top_e.reshape(-1)
    flat_tok = jnp.repeat(jnp.arange(T, dtype=jnp.int32), TOP_K)
    flat_gate = gate.reshape(-1)
    order = jnp.argsort(flat_e)
    sorted_e = flat_e[order]
    counts = jnp.bincount(flat_e, length=N_EXPERTS)
    padded = (counts + MOE_BLOCK - 1) // MOE_BLOCK * MOE_BLOCK
    start = jnp.cumsum(counts) - counts
    pad_end = jnp.cumsum(padded)
    pad_start = pad_end - padded
    dest = pad_start[sorted_e] + jnp.arange(A, dtype=jnp.int32) - start[sorted_e]
    n_blocks = -(-A // MOE_BLOCK) + N_EXPERTS
    n_slots = n_blocks * MOE_BLOCK
    slot_tok = jnp.zeros((n_slots,), jnp.int32).at[dest].set(flat_tok[order])
    slot_gate = jnp.zeros((n_slots,), jnp.float32).at[dest].set(flat_gate[order])
    block_e = jnp.minimum(jnp.searchsorted(pad_end, jnp.arange(n_blocks, dtype=jnp.int32) * MOE_BLOCK,
                                           side='right'), N_EXPERTS - 1)
    xs = xt[slot_tok].reshape(n_blocks, MOE_BLOCK, D)

    def expert_block(args):
        xb, e = args
        return _swiglu(xb, w1[e], w3[e], w2[e])

    ys = lax.map(expert_block, (xs, block_e))
    y = jnp.zeros((T, D), jnp.float32).at[slot_tok].add(
        ys.reshape(n_slots, D).astype(jnp.float32) * slot_gate[:, None])
    return y.reshape(B, S, D).astype(x.dtype)


def setup_inputs(seed: int = 0) -> dict:
    key = jax.random.key(seed)
    ks = jax.random.split(key, 32)
    f32 = jnp.float32

    def nrm(k, shape, fan_in, mult=1.0):
        return jax.random.normal(k, shape, f32) * (mult * fan_in ** -0.5)

    def gain(k, shape):
        return 1.0 + 0.1 * jax.random.normal(k, shape, f32)

    def bias(k, shape):
        return 0.02 * jax.random.normal(k, shape, f32)

    bd = DEEPNORM_BETA
    x = jax.random.normal(ks[0], (BATCH, SEQ, D_MODEL), f32)
    mem = jax.random.normal(ks[1], (BATCH, MEM_LEN, D_MODEL), f32)
    positions = (jax.random.randint(ks[2], (BATCH, 1), 0, 4096, dtype=jnp.int32)
                 + jnp.arange(SEQ, dtype=jnp.int32)[None, :])
    col_scale = jnp.ones((IN_COLS,), f32)
    col_scale = col_scale.at[OFF_ATT + 2 * ATT_WIDTH:OFF_ATT + 3 * ATT_WIDTH].set(bd)
    col_scale = col_scale.at[OFF_DN + 2 * DN_WIDTH:OFF_DN + 3 * DN_WIDTH].set(bd)
    w_in = nrm(ks[3], (DEPTH, D_MODEL, IN_COLS), D_MODEL) * col_scale
    conv_w = nrm(ks[4], (DEPTH, CONV_K, 3 * DN_WIDTH), CONV_K)
    a_log = jnp.log(jax.random.uniform(ks[5], (DEPTH, DN_HEADS), f32, 1.0, 16.0))
    dt = jnp.exp(jax.random.uniform(ks[6], (DEPTH, DN_HEADS), f32, math.log(1e-3), math.log(1e-1)))
    dt_bias = dt + jnp.log(-jnp.expm1(-dt))
    dn_norm_w = gain(ks[7], (DEPTH, HEAD_DIM))
    pool_w = nrm(ks[8], (DEPTH, POOL_GROUPS, POOL_GDIM, POOL_GDIM), POOL_GDIM)
    pool_scale = gain(ks[9], (DEPTH, POOL_WIDTH))
    w_out = nrm(ks[10], (DEPTH, MIX_WIDTH, D_MODEL), MIX_WIDTH, bd)
    ln_mix_g = gain(ks[11], (DEPTH, D_MODEL))
    ln_mix_b = bias(ks[12], (DEPTH, D_MODEL))
    xq_w = nrm(ks[13], (DEPTH, D_MODEL, D_MODEL), D_MODEL)
    xk_w = nrm(ks[14], (DEPTH, D_MODEL, D_MODEL), D_MODEL)
    xv_w = nrm(ks[15], (DEPTH, D_MODEL, D_MODEL), D_MODEL, bd)
    xo_w = nrm(ks[16], (DEPTH, D_MODEL, D_MODEL), D_MODEL, bd)
    ln_x_g = gain(ks[17], (DEPTH, D_MODEL))
    ln_x_b = bias(ks[18], (DEPTH, D_MODEL))
    ffn_w1 = nrm(ks[19], (N_DENSE, D_MODEL, D_FF), D_MODEL, bd)
    ffn_w3 = nrm(ks[20], (N_DENSE, D_MODEL, D_FF), D_MODEL, bd)
    ffn_w2 = nrm(ks[21], (N_DENSE, D_FF, D_MODEL), D_FF, bd)
    router_w = nrm(ks[22], (N_MOE, D_MODEL, N_EXPERTS), D_MODEL)
    moe_w1 = nrm(ks[23], (N_MOE, N_EXPERTS, D_MODEL, D_EXPERT), D_MODEL, bd)
    moe_w3 = nrm(ks[24], (N_MOE, N_EXPERTS, D_MODEL, D_EXPERT), D_MODEL, bd)
    moe_w2 = nrm(ks[25], (N_MOE, N_EXPERTS, D_EXPERT, D_MODEL), D_EXPERT, bd)
    ln_ffn_g = gain(ks[26], (DEPTH, D_MODEL))
    ln_ffn_b = bias(ks[27], (DEPTH, D_MODEL))
    return {"x": x, "mem": mem, "positions": positions, "w_in": w_in, "conv_w": conv_w,
            "a_log": a_log, "dt_bias": dt_bias, "dn_norm_w": dn_norm_w, "pool_w": pool_w,
            "pool_scale": pool_scale, "w_out": w_out, "ln_mix_g": ln_mix_g, "ln_mix_b": ln_mix_b,
            "xq_w": xq_w, "xk_w": xk_w, "xv_w": xv_w, "xo_w": xo_w, "ln_x_g": ln_x_g, "ln_x_b": ln_x_b,
            "ffn_w1": ffn_w1, "ffn_w3": ffn_w3, "ffn_w2": ffn_w2, "router_w": router_w,
            "moe_w1": moe_w1, "moe_w3": moe_w3, "moe_w2": moe_w2,
            "ln_ffn_g": ln_ffn_g, "ln_ffn_b": ln_ffn_b}


def reference(x, mem, positions, w_in, conv_w, a_log, dt_bias, dn_norm_w, pool_w, pool_scale, w_out,
              ln_mix_g, ln_mix_b, xq_w, xk_w, xv_w, xo_w, ln_x_g, ln_x_b, ffn_w1, ffn_w3, ffn_w2,
              router_w, moe_w1, moe_w3, moe_w2, ln_ffn_g, ln_ffn_b):
    for l in range(DEPTH):
        y = _hybrid_mixer(x, positions, w_in[l], conv_w[l], a_log[l], dt_bias[l], dn_norm_w[l],
                          pool_w[l], pool_scale[l], w_out[l])
        x = _layer_norm(DEEPNORM_ALPHA * x + y, ln_mix_g[l], ln_mix_b[l])
        y = _cross_attention(x, mem, xq_w[l], xk_w[l], xv_w[l], xo_w[l])
        x = _layer_norm(DEEPNORM_ALPHA * x + y, ln_x_g[l], ln_x_b[l])
        if l % 2 == 0:
            j = l // 2
            y = _swiglu(x, ffn_w1[j], ffn_w3[j], ffn_w2[j])
        else:
            j = l // 2
            y = _moe_swiglu(x, router_w[j], moe_w1[j], moe_w3[j], moe_w2[j])
        x = _layer_norm(DEEPNORM_ALPHA * x + y, ln_ffn_g[l], ln_ffn_b[l])
    return x
```

```python
import functools
import math

import jax
import jax.numpy as jnp
from jax import lax
from jax.experimental import pallas as pl
from jax.experimental.pallas import tpu as pltpu

F32, BF16, I32 = jnp.float32, jnp.bfloat16, jnp.int32

D_MODEL = 1024
HEAD_DIM = 64
ATT_HEADS = 6
ATT_WIDTH = ATT_HEADS * HEAD_DIM
DN_HEADS = 6
DN_WIDTH = DN_HEADS * HEAD_DIM
POOL_WIDTH = 256
POOL_WINDOWS = (2, 4, 8, 16)
POOL_GDIM = POOL_WIDTH // len(POOL_WINDOWS)
DILATIONS = (1, 4, 16)
CONV_K = 4
X_HEADS = 4
X_HEAD_DIM = D_MODEL // X_HEADS
N_EXPERTS = 8
DEPTH = 4
ALPHA = (2 * DEPTH) ** 0.25
LN_EPS = 1e-5
NORM_EPS = 1e-6

OFF_DN = 3 * ATT_WIDTH
OFF_BETA = OFF_DN + 3 * DN_WIDTH
OFF_DECAY = OFF_BETA + DN_HEADS
OFF_GATE = OFF_DECAY + DN_HEADS
OFF_POOL = OFF_GATE + DN_WIDTH

LANES = 128
P_ATT = 0
P_DN = 3 * ATT_WIDTH
P_GATE = P_DN + 3 * DN_WIDTH
P_SMALL = P_GATE + DN_WIDTH
P_POOL = P_SMALL + LANES
P_COLS = P_POOL + POOL_WIDTH

QB = 128
NEG = -1e30
VMEM_LIMIT = 48 * 1024 * 1024


def _params(*sem):
    return pltpu.CompilerParams(dimension_semantics=sem, vmem_limit_bytes=VMEM_LIMIT)


def _sigmoid(x):
    return 1.0 / (1.0 + jnp.exp(-x))


def _layer_norm_rows(z, g, b):
    mu = jnp.mean(z, axis=-1, keepdims=True)
    zc = z - mu
    var = jnp.mean(zc * zc, axis=-1, keepdims=True)
    return zc * lax.rsqrt(var + LN_EPS) * g + b


def _mm_body(x_ref, w_ref, o_ref):
    o_ref[...] = jnp.dot(x_ref[...].astype(BF16), w_ref[...],
                         preferred_element_type=F32).astype(o_ref.dtype)


def _matmul(x, w, out_dtype, tm, tn):
    m, k = x.shape
    n = w.shape[1]
    return pl.pallas_call(
        _mm_body, grid=(m // tm, n // tn),
        in_specs=[pl.BlockSpec((tm, k), lambda i, j: (i, 0)),
                  pl.BlockSpec((k, tn), lambda i, j: (0, j))],
        out_specs=pl.BlockSpec((tm, tn), lambda i, j: (i, j)),
        out_shape=jax.ShapeDtypeStruct((m, n), out_dtype),
        compiler_params=_params("parallel", "arbitrary"), name="matmul")(x, w)


def _att_body(q_ref, kc_ref, kp_ref, vc_ref, vp_ref, pq_ref, pkc_ref, pkp_ref, o_ref, lse_ref,
              kbuf, vbuf, pkbuf, *, rows, slopes):
    n = pl.program_id(2)
    nsub = rows // QB
    kbuf[0:QB, :] = kp_ref[...].astype(BF16)
    kbuf[QB:, :] = kc_ref[...].astype(BF16)
    vbuf[0:QB, :] = vp_ref[...].astype(BF16)
    vbuf[QB:, :] = vc_ref[...].astype(BF16)
    pkbuf[0] = pkp_ref[...]
    for j in range(nsub):
        pkbuf[j + 1] = pkc_ref[:, j * QB:(j + 1) * QB]
    a = lax.broadcasted_iota(I32, (QB, 2 * QB), 0)
    c = lax.broadcasted_iota(I32, (QB, 2 * QB), 1)
    tri = jnp.where(c < QB, c - a, a - (c - QB)) >= 0
    lane = lax.broadcasted_iota(I32, (QB, LANES), 1)
    low = lane < HEAD_DIM

    def sub(sb, carry):
        r0 = pl.multiple_of(sb * QB, QB)
        q = q_ref[pl.ds(r0, QB), :] * (HEAD_DIM ** -0.5)
        kw = kbuf[pl.ds(r0, 2 * QB), :]
        vw = vbuf[pl.ds(r0, 2 * QB), :]
        pq = pq_ref[pl.ds(r0, QB), :]
        pk = jnp.concatenate([pkbuf[sb], pkbuf[sb + 1]], axis=1)
        dist = (pq - pk).astype(F32)
        mask = tri & (c >= jnp.where((sb > 0) | (n > 0), 0, QB))
        lse = jnp.zeros((QB, LANES), F32)
        for p in range(ATT_HEADS // 2):
            cs = slice(p * LANES, (p + 1) * LANES)
            qp, kp, vp = q[:, cs], kw[:, cs], vw[:, cs]
            outs = []
            for hh in range(2):
                h = 2 * p + hh
                qm = jnp.where(low if hh == 0 else ~low, qp, 0.0).astype(BF16)
                s = lax.dot_general(qm, kp, (((1,), (1,)), ((), ())), preferred_element_type=F32)
                s = jnp.where(mask, s - slopes[h] * dist, NEG)
                m = jnp.max(s, axis=-1, keepdims=True)
                e = jnp.exp(s - m)
                l = jnp.sum(e, axis=-1, keepdims=True)
                o = jnp.dot(e.astype(BF16), vp, preferred_element_type=F32)
                outs.append(o / l)
                lse = jnp.where(lane == h, m + jnp.log(l), lse)
            o_ref[pl.ds(r0, QB), cs] = jnp.where(low, outs[0], outs[1])
        lse_ref[pl.ds(r0, QB), :] = lse
        return carry

    lax.fori_loop(0, nsub, sub, 0)


def _att_branch(h2, posc, posr, dil, batch, seq):
    sub_len = seq // dil
    rows = min(512, sub_len)
    nsteps = sub_len // rows
    rb = rows // QB
    nq = P_COLS // ATT_WIDTH
    h3 = h2.reshape(batch, sub_len, dil * P_COLS)
    slopes = tuple(2.0 ** (-8.0 * (i + 1) / ATT_HEADS) for i in range(ATT_HEADS))

    def cur(off):
        return pl.BlockSpec((None, rows, ATT_WIDTH), lambda b, r, n: (b, n, r * nq + off))

    def prev(off):
        return pl.BlockSpec((None, QB, ATT_WIDTH),
                            lambda b, r, n: (b, jnp.maximum(n * rb - 1, 0), r * nq + off))

    o3, l3 = pl.pallas_call(
        functools.partial(_att_body, rows=rows, slopes=slopes),
        grid=(batch, dil, nsteps),
        in_specs=[cur(0), cur(1), prev(1), cur(2), prev(2),
                  pl.BlockSpec((None, None, rows, 1), lambda b, r, n: (b, r, n, 0)),
                  pl.BlockSpec((None, None, 1, rows), lambda b, r, n: (b, r, 0, n)),
                  pl.BlockSpec((None, None, 1, QB),
                               lambda b, r, n: (b, r, 0, jnp.maximum(n * rb - 1, 0)))],
        out_specs=[pl.BlockSpec((None, rows, ATT_WIDTH), lambda b, r, n: (b, n, r)),
                   pl.BlockSpec((None, rows, LANES), lambda b, r, n: (b, n, r))],
        out_shape=[jax.ShapeDtypeStruct((batch, sub_len, dil * ATT_WIDTH), F32),
                   jax.ShapeDtypeStruct((batch, sub_len, dil * LANES), F32)],
        scratch_shapes=[pltpu.VMEM((rows + QB, ATT_WIDTH), BF16),
                        pltpu.VMEM((rows + QB, ATT_WIDTH), BF16),
                        pltpu.VMEM((rb + 1, 1, QB), I32)],
        compiler_params=_params("parallel", "parallel", "arbitrary"),
        name=f"dilated_att_d{dil}")(h3, h3, h3, h3, h3, posc, posr, posr)
    return o3.reshape(batch * seq, ATT_WIDTH), l3.reshape(batch * seq, LANES)


def _merge_body(o1, o2, o3, l1, l2, l3, out):
    ls = [l1[...], l2[...], l3[...]]
    m = jnp.maximum(jnp.maximum(ls[0], ls[1]), ls[2])
    ws = [jnp.exp(l - m) for l in ls]
    inv = 1.0 / (ws[0] + ws[1] + ws[2])
    ws = [w * inv for w in ws]
    lane = lax.broadcasted_iota(I32, (out.shape[0], LANES), 1)
    low = lane < HEAD_DIM
    for p in range(ATT_HEADS // 2):
        cs = slice(p * LANES, (p + 1) * LANES)
        acc = None
        for w, o in zip(ws, (o1, o2, o3)):
            wp = jnp.where(low, w[:, 2 * p:2 * p + 1], w[:, 2 * p + 1:2 * p + 2])
            t = wp * o[:, cs]
            acc = t if acc is None else acc + t
        out[:, cs] = acc.astype(out.dtype)


def _att_merge(os_, ls_, tm=1024):
    t = os_[0].shape[0]
    ospec = pl.BlockSpec((tm, ATT_WIDTH), lambda i: (i, 0))
    lspec = pl.BlockSpec((tm, LANES), lambda i: (i, 0))
    return pl.pallas_call(
        _merge_body, grid=(t // tm,), in_specs=[ospec] * 3 + [lspec] * 3, out_specs=ospec,
        out_shape=jax.ShapeDtypeStruct((t, ATT_WIDTH), BF16),
        compiler_params=_params("parallel"), name="att_merge")(*os_, *ls_)


def _dn_body(hq, hk, hv, pq, pk, pv, small, gate, convw, pvec, normw, out, xbuf, qkv, s_ref):
    n = pl.program_id(1)
    c_ = QB

    @pl.when(n == 0)
    def _():
        s_ref[...] = jnp.zeros_like(s_ref)

    halo = 8
    has_prev = jnp.where(n > 0, 1.0, 0.0)
    for idx, (cur, prev) in enumerate(((hq, pq), (hk, pk), (hv, pv))):
        xbuf[idx, 0:halo, :] = prev[...] * has_prev
        xbuf[idx, halo:, :] = cur[...]
        acc = None
        for j in range(CONV_K):
            start = halo - (CONV_K - 1) + j
            t = xbuf[idx, start:start + c_, :] * convw[j:j + 1, idx * DN_WIDTH:(idx + 1) * DN_WIDTH]
            acc = t if acc is None else acc + t
        qkv[idx] = acc * _sigmoid(acc)

    sm = small[...]
    beta_all = _sigmoid(sm)
    z = sm + pvec[1:2, :]
    softplus = jnp.maximum(z, 0.0) + jnp.log(1.0 + jnp.exp(-jnp.abs(z)))
    g_all = -jnp.exp(pvec[0:1, :]) * softplus

    lane = lax.broadcasted_iota(I32, (c_, LANES), 1)
    low = lane < HEAD_DIM
    row = lax.broadcasted_iota(I32, (c_, c_), 0)
    col = lax.broadcasted_iota(I32, (c_, c_), 1)
    lower = row >= col
    strict = row > col
    ltri = jnp.where(lower, 1.0, 0.0)
    eye = jnp.where(row == col, 1.0, 0.0)
    blockdiag = (row < HEAD_DIM) == (col < HEAD_DIM)

    def headsum(x):
        sa = jnp.sum(jnp.where(low, x, 0.0), axis=-1, keepdims=True)
        sb = jnp.sum(jnp.where(low, 0.0, x), axis=-1, keepdims=True)
        return jnp.where(low, sa, sb)

    def bdot(a, b):
        return jnp.dot(a.astype(BF16), b.astype(BF16), preferred_element_type=F32)

    for p in range(DN_HEADS // 2):
        cs = slice(p * LANES, (p + 1) * LANES)
        q, k, v = qkv[0, :, cs], qkv[1, :, cs], qkv[2, :, cs]
        q = q * lax.rsqrt(headsum(q * q) + NORM_EPS) * (HEAD_DIM ** -0.5)
        k = k * lax.rsqrt(headsum(k * k) + NORM_EPS)
        beta_cols = (beta_all[:, 2 * p:2 * p + 1], beta_all[:, 2 * p + 1:2 * p + 2])
        beta = jnp.where(low, beta_cols[0], beta_cols[1])
        g = jnp.where(low, g_all[:, DN_HEADS + 2 * p:DN_HEADS + 2 * p + 1],
                      g_all[:, DN_HEADS + 2 * p + 1:DN_HEADS + 2 * p + 2])
        gc = jnp.dot(ltri, g, preferred_element_type=F32, precision=lax.Precision.HIGHEST)
        eg = jnp.exp(gc)
        glast = gc[c_ - 1:c_, :]
        kb = k.astype(BF16)
        rhs = jnp.concatenate([v * beta, k * beta * eg], axis=1).astype(BF16)
        us, ws, qks = [], [], []
        for hh in range(2):
            sel = low if hh == 0 else ~low
            gcol = gc[:, hh * HEAD_DIM:hh * HEAD_DIM + 1]
            gb = jnp.broadcast_to(gcol, (c_, c_))
            diff = gb - gb.T
            dec = jnp.where(lower, jnp.exp(jnp.minimum(diff, 0.0)), 0.0)
            km = jnp.where(sel, k, 0.0).astype(BF16)
            qm = jnp.where(sel, q, 0.0).astype(BF16)
            kk = lax.dot_general(km, kb, (((1,), (1,)), ((), ())), preferred_element_type=F32)
            qk = lax.dot_general(qm, kb, (((1,), (1,)), ((), ())), preferred_element_type=F32)
            nm = jnp.where(strict, beta_cols[hh] * kk * dec, 0.0)
            x = eye - nm
            pw = nm
            for _ in range(int(math.log2(c_)) - 1):
                pw = bdot(pw, pw)
                x = x + bdot(x, pw)
            uw = jnp.dot(x.astype(BF16), rhs, preferred_element_type=F32)
            us.append(uw[:, :LANES])
            ws.append(uw[:, LANES:])
            qks.append(qk * dec)
        u = jnp.where(low, us[0], us[1])
        w = jnp.where(low, ws[0], ws[1])
        state = s_ref[p]
        sb16 = state.astype(BF16)
        vnew = u - jnp.dot(w.astype(BF16), sb16, preferred_element_type=F32)
        vn16 = vnew.astype(BF16)
        o = jnp.dot((q * eg).astype(BF16), sb16, preferred_element_type=F32)
        o = o + jnp.where(low, jnp.dot(qks[0].astype(BF16), vn16, preferred_element_type=F32),
                          jnp.dot(qks[1].astype(BF16), vn16, preferred_element_type=F32))
        kd = k * jnp.exp(glast - gc)
        upd = jnp.dot(kd.T.astype(BF16), vn16, preferred_element_type=F32)
        s_ref[p] = state * jnp.exp(glast) + jnp.where(blockdiag, upd, 0.0)
        ms = headsum(o * o) * (1.0 / HEAD_DIM)
        gt = gate[:, cs]
        out[:, cs] = (o * lax.rsqrt(ms + NORM_EPS) * normw[:, cs] * (gt * _sigmoid(gt))).astype(out.dtype)


def _delta_net(h2, convw, pvec, normw, batch, seq):
    h3 = h2.reshape(batch, seq, P_COLS)
    nchunks = seq // QB
    qcol = P_DN // DN_WIDTH

    def cur(off):
        return pl.BlockSpec((None, QB, DN_WIDTH), lambda b, n: (b, n, qcol + off))

    def prev(off):
        return pl.BlockSpec((None, 8, DN_WIDTH),
                            lambda b, n: (b, jnp.maximum(n * (QB // 8) - 1, 0), qcol + off))

    out = pl.pallas_call(
        _dn_body, grid=(batch, nchunks),
        in_specs=[cur(0), cur(1), cur(2), prev(0), prev(1), prev(2),
                  pl.BlockSpec((None, QB, LANES), lambda b, n: (b, n, P_SMALL // LANES)),
                  pl.BlockSpec((None, QB, DN_WIDTH), lambda b, n: (b, n, P_GATE // DN_WIDTH)),
                  pl.BlockSpec((CONV_K, 3 * DN_WIDTH), lambda b, n: (0, 0)),
                  pl.BlockSpec((2, LANES), lambda b, n: (0, 0)),
                  pl.BlockSpec((1, DN_WIDTH), lambda b, n: (0, 0))],
        out_specs=pl.BlockSpec((None, QB, DN_WIDTH), lambda b, n: (b, n, 0)),
        out_shape=jax.ShapeDtypeStruct((batch, seq, DN_WIDTH), BF16),
        scratch_shapes=[pltpu.VMEM((3, QB + 8, DN_WIDTH), F32),
                        pltpu.VMEM((3, QB, DN_WIDTH), F32),
                        pltpu.VMEM((DN_HEADS // 2, LANES, LANES), F32)],
        compiler_params=_params("parallel", "arbitrary"),
        name="gated_delta_rule")(h3, h3, h3, h3, h3, h3, h3, h3, convw, pvec, normw)
    return out.reshape(batch * seq, DN_WIDTH)


def _pool_body(cur, prev, wbd, scale, out, xbuf, *, rows):
    n = pl.program_id(1)
    halo = 16
    xbuf[0:halo, :] = prev[...] * jnp.where(n > 0, 1.0, 0.0)
    xbuf[halo:, :] = cur[...]
    u = cur[...]
    lane = lax.broadcasted_iota(I32, (rows, POOL_WIDTH), 1)
    tpos = n * rows + lax.broadcasted_iota(I32, (rows, POOL_WIDTH), 0)
    run = u
    pooled = jnp.zeros_like(u)
    for j in range(1, max(POOL_WINDOWS)):
        run = run + xbuf[halo - j:halo - j + rows, :]
        if j + 1 in POOL_WINDOWS:
            gi = POOL_WINDOWS.index(j + 1)
            cnt = jnp.minimum(tpos + 1, j + 1).astype(F32)
            sel = (lane >= gi * POOL_GDIM) & (lane < (gi + 1) * POOL_GDIM)
            pooled = jnp.where(sel, run / cnt - u, pooled)
    mixed = jnp.dot(pooled.astype(BF16), wbd[...], preferred_element_type=F32)
    out[...] = (mixed * scale[...]).astype(out.dtype)


def _pool(h2, wbd, scale, batch, seq, rows=512):
    h3 = h2.reshape(batch, seq, P_COLS)
    out = pl.pallas_call(
        functools.partial(_pool_body, rows=rows), grid=(batch, seq // rows),
        in_specs=[pl.BlockSpec((None, rows, POOL_WIDTH), lambda b, n: (b, n, P_POOL // POOL_WIDTH)),
                  pl.BlockSpec((None, 16, POOL_WIDTH),
                               lambda b, n: (b, jnp.maximum(n * (rows // 16) - 1, 0), P_POOL // POOL_WIDTH)),
                  pl.BlockSpec((POOL_WIDTH, POOL_WIDTH), lambda b, n: (0, 0)),
                  pl.BlockSpec((1, POOL_WIDTH), lambda b, n: (0, 0))],
        out_specs=pl.BlockSpec((None, rows, POOL_WIDTH), lambda b, n: (b, n, 0)),
        out_shape=jax.ShapeDtypeStruct((batch, seq, POOL_WIDTH), BF16),
        scratch_shapes=[pltpu.VMEM((rows + 16, POOL_WIDTH), F32)],
        compiler_params=_params("parallel", "arbitrary"), name="multiscale_pool")(h3, h3, wbd, scale)
    return out.reshape(batch * seq, POOL_WIDTH)


def _proj_ln_body(a, d, p, x, wa, wd, wp, g, b, o):
    y = jnp.dot(a[...], wa[...], preferred_element_type=F32)
    y = y + jnp.dot(d[...], wd[...], preferred_element_type=F32)
    y = y + jnp.dot(p[...], wp[...], preferred_element_type=F32)
    o[...] = _layer_norm_rows(ALPHA * x[...] + y, g[...], b[...])


def _proj_ln(att, dn, pool, x, wa, wd, wp, g, b, tm=512):
    t = x.shape[0]

    def rows(wd_):
        return pl.BlockSpec((tm, wd_), lambda i: (i, 0))

    def full(a_):
        return pl.BlockSpec(a_.shape, lambda i: (0, 0))

    return pl.pallas_call(
        _proj_ln_body, grid=(t // tm,),
        in_specs=[rows(ATT_WIDTH), rows(DN_WIDTH), rows(POOL_WIDTH), rows(D_MODEL),
                  full(wa), full(wd), full(wp), full(g), full(b)],
        out_specs=rows(D_MODEL), out_shape=jax.ShapeDtypeStruct((t, D_MODEL), F32),
        compiler_params=_params("parallel"), name="out_proj_ln")(att, dn, pool, x, wa, wd, wp, g, b)


def _xattn_body(x, wq, k, v, wo, g, b, o):
    xv = x[...]
    q = jnp.dot(xv.astype(BF16), wq[...], preferred_element_type=F32).astype(BF16)
    heads = []
    for h in range(X_HEADS):
        cs = slice(h * X_HEAD_DIM, (h + 1) * X_HEAD_DIM)
        s = lax.dot_general(q[:, cs], k[:, cs], (((1,), (1,)), ((), ())), preferred_element_type=F32)
        m = jnp.max(s, axis=-1, keepdims=True)
        e = jnp.exp(s - m)
        l = jnp.sum(e, axis=-1, keepdims=True)
        oh = jnp.dot(e.astype(BF16), v[:, cs], preferred_element_type=F32) / l
        heads.append(oh.astype(BF16))
    oc = jnp.concatenate(heads, axis=1)
    y = jnp.dot(oc, wo[...], preferred_element_type=F32)
    o[...] = _layer_norm_rows(ALPHA * xv + y, g[...], b[...])


def _cross_attention(x, wq, k, v, wo, g, b, seq, tm=512):
    t = x.shape[0]
    mem_len = k.shape[1]
    per_batch = seq // tm

    def full(a_):
        return pl.BlockSpec(a_.shape, lambda i: (0, 0))

    kv = pl.BlockSpec((None, mem_len, D_MODEL), lambda i: (i // per_batch, 0, 0))
    rows = pl.BlockSpec((tm, D_MODEL), lambda i: (i, 0))
    return pl.pallas_call(
        _xattn_body, grid=(t // tm,),
        in_specs=[rows, full(wq), kv, kv, full(wo), full(g), full(b)],
        out_specs=rows, out_shape=jax.ShapeDtypeStruct((t, D_MODEL), F32),
        compiler_params=_params("parallel"), name="cross_attention_ln")(x, wq, k, v, wo, g, b)


def _ffn_body(x, w1, w3, w2, g, b, o, xb, acc):
    f = pl.program_id(1)

    @pl.when(f == 0)
    def _():
        xb[...] = x[...].astype(BF16)
        acc[...] = jnp.zeros_like(acc)

    xv = xb[...]
    h1 = jnp.dot(xv, w1[...], preferred_element_type=F32)
    h3 = jnp.dot(xv, w3[...], preferred_element_type=F32)
    hh = (h1 * _sigmoid(h1) * h3).astype(BF16)
    acc[...] += jnp.dot(hh, w2[...], preferred_element_type=F32)

    @pl.when(f == pl.num_programs(1) - 1)
    def _():
        o[...] = _layer_norm_rows(ALPHA * x[...] + acc[...], g[...], b[...])


def _ffn_dense(x, w1, w3, w2, g, b, tm=1024, tf=256):
    t = x.shape[0]
    dff = w1.shape[1]
    rows = pl.BlockSpec((tm, D_MODEL), lambda i, f: (i, 0))
    vec = pl.BlockSpec((1, D_MODEL), lambda i, f: (0, 0))
    return pl.pallas_call(
        _ffn_body, grid=(t // tm, dff // tf),
        in_specs=[rows, pl.BlockSpec((D_MODEL, tf), lambda i, f: (0, f)),
                  pl.BlockSpec((D_MODEL, tf), lambda i, f: (0, f)),
                  pl.BlockSpec((tf, D_MODEL), lambda i, f: (f, 0)), vec, vec],
        out_specs=rows, out_shape=jax.ShapeDtypeStruct((t, D_MODEL), F32),
        scratch_shapes=[pltpu.VMEM((tm, D_MODEL), BF16), pltpu.VMEM((tm, D_MODEL), F32)],
        compiler_params=_params("parallel", "arbitrary"), name="ffn_swiglu_ln")(x, w1, w3, w2, g, b)


def _router_body(x, rw, eidx, gates, ranks, counts, carry, *, tm):
    i = pl.program_id(0)

    @pl.when(i == 0)
    def _():
        carry[...] = jnp.zeros_like(carry)

    logits = jnp.dot(x[...], rw[...], preferred_element_type=F32, precision=lax.Precision.HIGHEST)
    lane = lax.broadcasted_iota(I32, (tm, LANES), 1)
    lanef = lane.astype(F32)
    lg = jnp.where(lane < N_EXPERTS, logits, NEG)
    m1 = jnp.max(lg, axis=-1, keepdims=True)
    i1 = jnp.min(jnp.where(lg == m1, lanef, float(LANES)), axis=-1, keepdims=True)
    lg2 = jnp.where(lanef == i1, NEG, lg)
    m2 = jnp.max(lg2, axis=-1, keepdims=True)
    i2 = jnp.min(jnp.where(lg2 == m2, lanef, float(LANES)), axis=-1, keepdims=True)
    e21 = jnp.exp(m2 - m1)
    g1 = 1.0 / (1.0 + e21)
    g2 = e21 * g1
    hot1 = lanef == i1
    hot2 = lanef == i2
    onehot = jnp.where(hot1 | hot2, 1.0, 0.0)
    r = lax.broadcasted_iota(I32, (tm, tm), 0)
    c = lax.broadcasted_iota(I32, (tm, tm), 1)
    before = jnp.dot(jnp.where(r > c, 1.0, 0.0).astype(BF16), onehot.astype(BF16),
                     preferred_element_type=F32) + carry[...]
    rank1 = jnp.sum(jnp.where(hot1, before, 0.0), axis=-1, keepdims=True)
    rank2 = jnp.sum(jnp.where(hot2, before, 0.0), axis=-1, keepdims=True)
    carry[...] = carry[...] + jnp.sum(onehot, axis=0, keepdims=True)
    eidx[...] = jnp.where(lane == 0, i1, jnp.where(lane == 1, i2, 0.0)).astype(I32)
    gates[...] = jnp.where(lane == 0, g1, jnp.where(lane == 1, g2, 0.0))
    ranks[...] = jnp.where(lane == 0, rank1, jnp.where(lane == 1, rank2, 0.0)).astype(I32)
    counts[...] = jnp.broadcast_to(carry[...], counts.shape).astype(I32)


def _router(x, rw, tm=512):
    t = x.shape[0]
    rows = pl.BlockSpec((tm, LANES), lambda i: (i, 0))
    return pl.pallas_call(
        functools.partial(_router_body, tm=tm), grid=(t // tm,),
        in_specs=[pl.BlockSpec((tm, D_MODEL), lambda i: (i, 0)),
                  pl.BlockSpec((D_MODEL, LANES), lambda i: (0, 0))],
        out_specs=[rows, rows, rows, pl.BlockSpec((8, LANES), lambda i: (0, 0))],
        out_shape=[jax.ShapeDtypeStruct((t, LANES), I32), jax.ShapeDtypeStruct((t, LANES), F32),
                   jax.ShapeDtypeStruct((t, LANES), I32), jax.ShapeDtypeStruct((8, LANES), I32)],
        scratch_shapes=[pltpu.VMEM((1, LANES), F32)],
        compiler_params=_params("arbitrary"), name="router_top2")(x, rw)


def _moe_body(be, nused, xs, w1, w3, w2, ys, acc):
    i = pl.program_id(0)
    f = pl.program_id(1)
    last = f == pl.num_programs(1) - 1
    used = i < nused[0]

    @pl.when(used & (f == 0))
    def _():
        acc[...] = jnp.zeros_like(acc)

    @pl.when(used)
    def _():
        xv = xs[...]
        h1 = jnp.dot(xv, w1[...], preferred_element_type=F32)
        h3 = jnp.dot(xv, w3[...], preferred_element_type=F32)
        hh = (h1 * _sigmoid(h1) * h3).astype(BF16)
        acc[...] += jnp.dot(hh, w2[...], preferred_element_type=F32)

    @pl.when(used & last)
    def _():
        ys[...] = acc[...]

    @pl.when(jnp.logical_not(used) & last)
    def _():
        ys[...] = jnp.zeros_like(ys)


def _moe_experts(block_e, n_used, xs, w1, w3, w2, tm, tf=512):
    n_slots = xs.shape[0]
    dex = w1.shape[2]
    nblk = n_slots // tm

    def live(i, nu):
        return jnp.minimum(i, nu[0] - 1)

    grid_spec = pltpu.PrefetchScalarGridSpec(
        num_scalar_prefetch=2, grid=(nblk, dex // tf),
        in_specs=[pl.BlockSpec((tm, D_MODEL), lambda i, f, be, nu: (live(i, nu), 0)),
                  pl.BlockSpec((None, D_MODEL, tf), lambda i, f, be, nu: (be[i], 0, jnp.where(i < nu[0], f, dex // tf - 1))),
                  pl.BlockSpec((None, D_MODEL, tf), lambda i, f, be, nu: (be[i], 0, jnp.where(i < nu[0], f, dex // tf - 1))),
                  pl.BlockSpec((None, tf, D_MODEL), lambda i, f, be, nu: (be[i], jnp.where(i < nu[0], f, dex // tf - 1), 0))],
        out_specs=pl.BlockSpec((tm, D_MODEL), lambda i, f, be, nu: (i, 0)),
        scratch_shapes=[pltpu.VMEM((tm, D_MODEL), F32)])
    return pl.pallas_call(
        _moe_body, grid_spec=grid_spec, out_shape=jax.ShapeDtypeStruct((n_slots, D_MODEL), F32),
        compiler_params=_params("arbitrary", "arbitrary"), name="moe_experts")(block_e, n_used, xs, w1, w3, w2)


def _combine_ln_body(x, ya, yb, gates, g, b, o):
    gt = gates[...]
    y = ya[...] * gt[:, 0:1] + yb[...] * gt[:, 1:2]
    o[...] = _layer_norm_rows(ALPHA * x[...] + y, g[...], b[...])


def _combine_ln(x, ya, yb, gates, g, b, tm=512):
    t = x.shape[0]
    rows = pl.BlockSpec((tm, D_MODEL), lambda i: (i, 0))
    vec = pl.BlockSpec((1, D_MODEL), lambda i: (0, 0))
    return pl.pallas_call(
        _combine_ln_body, grid=(t // tm,),
        in_specs=[rows, rows, rows, pl.BlockSpec((tm, LANES), lambda i: (i, 0)), vec, vec],
        out_specs=rows, out_shape=jax.ShapeDtypeStruct((t, D_MODEL), F32),
        compiler_params=_params("parallel"), name="moe_combine_ln")(x, ya, yb, gates, g, b)


def _moe_layer(x, rw_pad, w1, w3, w2, g, b, tm=512):
    t = x.shape[0]
    eidx, gates, ranks, counts = _router(x, rw_pad)
    cnt = counts[0, :N_EXPERTS]
    padded = (cnt + tm - 1) // tm * tm
    pad_end = jnp.cumsum(padded)
    pad_start = pad_end - padded
    e2 = eidx[:, :2]
    dest = pad_start[e2] + ranks[:, :2]
    nblk = (2 * t) // tm + N_EXPERTS
    n_slots = nblk * tm
    tok = jnp.broadcast_to(jnp.arange(t, dtype=I32)[:, None], (t, 2))
    slot_tok = jnp.zeros((n_slots,), I32).at[dest.reshape(-1)].set(tok.reshape(-1))
    n_used = (pad_end[-1] // tm).astype(I32).reshape(1)
    blk = jnp.arange(nblk, dtype=I32) * tm
    block_e = jnp.minimum(jnp.sum(pad_end[None, :] <= blk[:, None], axis=1), N_EXPERTS - 1).astype(I32)
    block_e = jnp.where(jnp.arange(nblk) < n_used[0], block_e, block_e[jnp.maximum(n_used[0] - 1, 0)])
    xs = x.astype(BF16)[slot_tok]
    ys = _moe_experts(block_e, n_used, xs, w1, w3, w2, tm)
    return _combine_ln(x, ys[dest[:, 0]], ys[dest[:, 1]], gates, g, b)


def _block_diag(w):
    g, c, _ = w.shape
    eye = jnp.eye(g, dtype=w.dtype)
    return (eye[:, None, :, None] * w[:, :, None, :]).reshape(g * c, g * c)


def kernel(x, mem, positions, w_in, conv_w, a_log, dt_bias, dn_norm_w, pool_w, pool_scale, w_out, ln_mix_g, ln_mix_b, xq_w, xk_w, xv_w, xo_w, ln_x_g, ln_x_b, ffn_w1, ffn_w3, ffn_w2, router_w, moe_w1, moe_w3, moe_w2, ln_ffn_g, ln_ffn_b):
    batch, seq, _ = x.shape
    t = batch * seq
    depth = w_in.shape[0]
    xf = x.reshape(t, D_MODEL)
    memf = mem.reshape(batch * mem.shape[1], D_MODEL)

    posc, posr = {}, {}
    for dil in DILATIONS:
        pt = positions.reshape(batch, seq // dil, dil).transpose(0, 2, 1)
        posc[dil] = pt[..., None]
        posr[dil] = pt[:, :, None, :]

    for l in range(depth):
        wl = w_in[l]
        small = jnp.pad(wl[:, OFF_BETA:OFF_GATE], ((0, 0), (0, LANES - 2 * DN_HEADS)))
        w_in_p = jnp.concatenate([wl[:, :OFF_BETA], wl[:, OFF_GATE:OFF_POOL], small, wl[:, OFF_POOL:]],
                                 axis=1).astype(BF16)
        h = _matmul(xf, w_in_p, F32, 1024, 1024)

        branches = [_att_branch(h, posc[d], posr[d], d, batch, seq) for d in DILATIONS]
        att = _att_merge([o for o, _ in branches], [s for _, s in branches])

        pvec = jnp.zeros((2, LANES), F32)
        pvec = pvec.at[0, DN_HEADS:2 * DN_HEADS].set(a_log[l]).at[1, DN_HEADS:2 * DN_HEADS].set(dt_bias[l])
        normw = jnp.tile(dn_norm_w[l], DN_HEADS)[None, :]
        dn = _delta_net(h, conv_w[l], pvec, normw, batch, seq)

        pool = _pool(h, _block_diag(pool_w[l]).astype(BF16), pool_scale[l][None, :], batch, seq)

        wo = w_out[l].astype(BF16)
        xf = _proj_ln(att, dn, pool, xf, wo[:ATT_WIDTH], wo[ATT_WIDTH:ATT_WIDTH + DN_WIDTH],
                      wo[ATT_WIDTH + DN_WIDTH:], ln_mix_g[l][None, :], ln_mix_b[l][None, :])

        wkv = jnp.concatenate([xk_w[l], xv_w[l]], axis=1).astype(BF16)
        kv = _matmul(memf, wkv, BF16, 512, 1024).reshape(batch, mem.shape[1], 2 * D_MODEL)
        xf = _cross_attention(xf, (xq_w[l] * (X_HEAD_DIM ** -0.5)).astype(BF16), kv[..., :D_MODEL],
                              kv[..., D_MODEL:], xo_w[l].astype(BF16), ln_x_g[l][None, :],
                              ln_x_b[l][None, :], seq)

        j = l // 2
        gl, bl = ln_ffn_g[l][None, :], ln_ffn_b[l][None, :]
        if l % 2 == 0:
            xf = _ffn_dense(xf, ffn_w1[j].astype(BF16), ffn_w3[j].astype(BF16), ffn_w2[j].astype(BF16), gl, bl)
        else:
            rw_pad = jnp.pad(router_w[j], ((0, 0), (0, LANES - N_EXPERTS)))
            xf = _moe_layer(xf, rw_pad, moe_w1[j].astype(BF16), moe_w3[j].astype(BF16),
                            moe_w2[j].astype(BF16), gl, bl)
    return xf.reshape(batch, seq, D_MODEL)
```

```python
import functools
import math

import jax
import jax.numpy as jnp
from jax import lax
from jax.experimental import pallas as pl
from jax.experimental.pallas import tpu as pltpu

F32, BF16, I32 = jnp.float32, jnp.bfloat16, jnp.int32

D_MODEL = 1024
HEAD_DIM = 64
ATT_HEADS = 6
ATT_WIDTH = ATT_HEADS * HEAD_DIM
DN_HEADS = 6
DN_WIDTH = DN_HEADS * HEAD_DIM
POOL_WIDTH = 256
POOL_WINDOWS = (2, 4, 8, 16)
POOL_GDIM = POOL_WIDTH // len(POOL_WINDOWS)
DILATIONS = (1, 4, 16)
CONV_K = 4
X_HEADS = 4
X_HEAD_DIM = D_MODEL // X_HEADS
N_EXPERTS = 8
DEPTH = 4
ALPHA = (2 * DEPTH) ** 0.25
LN_EPS = 1e-5
NORM_EPS = 1e-6

OFF_DN = 3 * ATT_WIDTH
OFF_BETA = OFF_DN + 3 * DN_WIDTH
OFF_DECAY = OFF_BETA + DN_HEADS
OFF_GATE = OFF_DECAY + DN_HEADS
OFF_POOL = OFF_GATE + DN_WIDTH

LANES = 128
R_DN = 0
R_GATE = R_DN + 3 * DN_WIDTH
R_POOL = R_GATE + DN_WIDTH
R_SMALL = R_POOL + POOL_WIDTH
R_COLS = R_SMALL + LANES

QB = 128
NEG = -1e30
VMEM_LIMIT = 48 * 1024 * 1024


def _params(*sem):
    return pltpu.CompilerParams(dimension_semantics=sem, vmem_limit_bytes=VMEM_LIMIT)


def _sigmoid(x):
    return 1.0 / (1.0 + jnp.exp(-x))


def _layer_norm_rows(z, g, b):
    mu = jnp.mean(z, axis=-1, keepdims=True)
    zc = z - mu
    var = jnp.mean(zc * zc, axis=-1, keepdims=True)
    return zc * lax.rsqrt(var + LN_EPS) * g + b


def _rows(start, size, stride):
    return pl.ds(start, size) if stride == 1 else pl.ds(start, size, stride=stride)


def _mm_body(x_ref, w_ref, o_ref):
    o_ref[...] = jnp.dot(x_ref[...].astype(BF16), w_ref[...],
                         preferred_element_type=F32).astype(o_ref.dtype)


def _matmul(x, w, out_dtype, tm, tn):
    m, k = x.shape
    n = w.shape[1]
    return pl.pallas_call(
        _mm_body, grid=(m // tm, n // tn),
        in_specs=[pl.BlockSpec((tm, k), lambda i, j: (i, 0)),
                  pl.BlockSpec((k, tn), lambda i, j: (0, j))],
        out_specs=pl.BlockSpec((tm, tn), lambda i, j: (i, j)),
        out_shape=jax.ShapeDtypeStruct((m, n), out_dtype),
        compiler_params=_params("parallel", "arbitrary"), name="matmul")(x, w)


def _mm_slab_body(x_ref, w_ref, o_ref):
    res = jnp.dot(x_ref[...].astype(BF16), w_ref[...], preferred_element_type=F32)
    for j in range(o_ref.shape[0]):
        o_ref[j] = res[:, j * LANES:(j + 1) * LANES]


def _matmul_slabs(x, w, tm):
    m, k = x.shape
    n = w.shape[1]
    return pl.pallas_call(
        _mm_slab_body, grid=(m // tm,),
        in_specs=[pl.BlockSpec((tm, k), lambda i: (i, 0)), pl.BlockSpec((k, n), lambda i: (0, 0))],
        out_specs=pl.BlockSpec((n // LANES, tm, LANES), lambda i: (0, i, 0)),
        out_shape=jax.ShapeDtypeStruct((n // LANES, m, LANES), F32),
        compiler_params=_params("parallel"), name="matmul_slabs")(x, w)


def _att_body(q_ref, kc_ref, kp_ref, vc_ref, vp_ref, pq_ref, pkc_ref, pkp_ref, o_ref, lse_ref,
              kbuf, vbuf, pkbuf, *, dil, rows, slopes):
    n = pl.program_id(1)
    nsub = rows // QB
    npair = ATT_HEADS // 2
    a = lax.broadcasted_iota(I32, (QB, 2 * QB), 0)
    c = lax.broadcasted_iota(I32, (QB, 2 * QB), 1)
    tri = jnp.where(c < QB, c - a, a - (c - QB)) >= 0
    lane = lax.broadcasted_iota(I32, (QB, LANES), 1)
    low = lane < HEAD_DIM

    def residue(r, carry):
        for p in range(npair):
            kbuf[p, 0:QB, :] = kp_ref[p, _rows(r, QB, dil), :].astype(BF16)
            kbuf[p, QB:, :] = kc_ref[p, _rows(r, rows, dil), :].astype(BF16)
            vbuf[p, 0:QB, :] = vp_ref[p, _rows(r, QB, dil), :].astype(BF16)
            vbuf[p, QB:, :] = vc_ref[p, _rows(r, rows, dil), :].astype(BF16)
        pkbuf[0] = pkp_ref[r]
        for j in range(nsub):
            pkbuf[j + 1] = pkc_ref[r, :, j * QB:(j + 1) * QB]

        def sub(sb, carry2):
            r0 = pl.multiple_of(sb * QB, QB)
            t0 = r + r0 * dil
            pq = pq_ref[r, pl.ds(r0, QB), :]
            pk = jnp.concatenate([pkbuf[sb], pkbuf[sb + 1]], axis=1)
            dist = (pq - pk).astype(F32)
            mask = tri & (c >= jnp.where((sb > 0) | (n > 0), 0, QB))
            lse = jnp.zeros((QB, LANES), F32)
            for p in range(npair):
                qp = q_ref[p, _rows(t0, QB, dil), :] * (HEAD_DIM ** -0.5)
                kp = kbuf[p, pl.ds(r0, 2 * QB), :]
                vp = vbuf[p, pl.ds(r0, 2 * QB), :]
                outs = []
                for hh in range(2):
                    h = 2 * p + hh
                    qm = jnp.where(low if hh == 0 else ~low, qp, 0.0).astype(BF16)
                    s = lax.dot_general(qm, kp, (((1,), (1,)), ((), ())), preferred_element_type=F32)
                    s = jnp.where(mask, s - slopes[h] * dist, NEG)
                    m = jnp.max(s, axis=-1, keepdims=True)
                    e = jnp.exp(s - m)
                    l = jnp.sum(e, axis=-1, keepdims=True)
                    o = jnp.dot(e.astype(BF16), vp, preferred_element_type=F32)
                    outs.append(o / l)
                    lse = jnp.where(lane == h, m + jnp.log(l), lse)
                o_ref[p, _rows(t0, QB, dil), :] = jnp.where(low, outs[0], outs[1])
            lse_ref[_rows(t0, QB, dil), :] = lse
            return carry2

        lax.fori_loop(0, nsub, sub, 0)
        return carry

    lax.fori_loop(0, dil, residue, 0)


def _att_branch(slabs, posc, posr, dil, batch, seq):
    sub_len = seq // dil
    rows = min(512 if dil == 1 else 2048 // dil, sub_len)
    tok = rows * dil
    nsteps = seq // tok
    prev_tok = QB * dil
    ratio = tok // prev_tok
    npair = ATT_HEADS // 2
    slopes = tuple(2.0 ** (-8.0 * (i + 1) / ATT_HEADS) for i in range(ATT_HEADS))

    def cur(which):
        return pl.BlockSpec((npair, tok, LANES), lambda b, n: (which, b * nsteps + n, 0))

    def prev(which):
        return pl.BlockSpec((npair, prev_tok, LANES),
                            lambda b, n: (which, jnp.maximum((b * nsteps + n) * ratio - 1, 0), 0))

    t = batch * seq
    return pl.pallas_call(
        functools.partial(_att_body, dil=dil, rows=rows, slopes=slopes),
        grid=(batch, nsteps),
        in_specs=[cur(0), cur(1), prev(1), cur(2), prev(2),
                  pl.BlockSpec((None, dil, rows, 1), lambda b, n: (b, 0, n, 0)),
                  pl.BlockSpec((None, dil, 1, rows), lambda b, n: (b, 0, 0, n)),
                  pl.BlockSpec((None, dil, 1, QB),
                               lambda b, n: (b, 0, 0, jnp.maximum(n * (rows // QB) - 1, 0)))],
        out_specs=[pl.BlockSpec((npair, tok, LANES), lambda b, n: (0, b * nsteps + n, 0)),
                   pl.BlockSpec((tok, LANES), lambda b, n: (b * nsteps + n, 0))],
        out_shape=[jax.ShapeDtypeStruct((npair, t, LANES), F32),
                   jax.ShapeDtypeStruct((t, LANES), F32)],
        scratch_shapes=[pltpu.VMEM((npair, rows + QB, LANES), BF16),
                        pltpu.VMEM((npair, rows + QB, LANES), BF16),
                        pltpu.VMEM((rows // QB + 1, 1, QB), I32)],
        compiler_params=_params("parallel", "arbitrary"),
        name=f"dilated_att_d{dil}")(slabs, slabs, slabs, slabs, slabs, posc, posr, posr)


def _merge_body(o1, o2, o3, l1, l2, l3, out):
    ls = [l1[...], l2[...], l3[...]]
    m = jnp.maximum(jnp.maximum(ls[0], ls[1]), ls[2])
    ws = [jnp.exp(l - m) for l in ls]
    inv = 1.0 / (ws[0] + ws[1] + ws[2])
    ws = [w * inv for w in ws]
    lane = lax.broadcasted_iota(I32, (out.shape[0], LANES), 1)
    low = lane < HEAD_DIM
    for p in range(ATT_HEADS // 2):
        acc = None
        for w, o in zip(ws, (o1, o2, o3)):
            wp = jnp.where(low, w[:, 2 * p:2 * p + 1], w[:, 2 * p + 1:2 * p + 2])
            t = wp * o[p]
            acc = t if acc is None else acc + t
        out[:, p * LANES:(p + 1) * LANES] = acc.astype(out.dtype)


def _att_merge(os_, ls_, tm=1024):
    npair, t, _ = os_[0].shape
    ospec = pl.BlockSpec((npair, tm, LANES), lambda i: (0, i, 0))
    lspec = pl.BlockSpec((tm, LANES), lambda i: (i, 0))
    return pl.pallas_call(
        _merge_body, grid=(t // tm,), in_specs=[ospec] * 3 + [lspec] * 3,
        out_specs=pl.BlockSpec((tm, ATT_WIDTH), lambda i: (i, 0)),
        out_shape=jax.ShapeDtypeStruct((t, ATT_WIDTH), BF16),
        compiler_params=_params("parallel"), name="att_merge")(*os_, *ls_)


def _split2(a):
    hi = a.astype(BF16)
    lo = (a - hi.astype(F32)).astype(BF16)
    return hi, lo


def _dot(a, b):
    return jnp.dot(a, b, preferred_element_type=F32)


def _dot_hp(a, b):
    ah, al = _split2(a)
    bh, bl = _split2(b)
    return _dot(ah, bh) + _dot(al, bh) + _dot(ah, bl)


def _dot_t(a, b):
    return lax.dot_general(a, b, (((1,), (1,)), ((), ())), preferred_element_type=F32)


def _dot_t_hp(a, b):
    ah, al = _split2(a)
    bh, bl = _split2(b)
    return _dot_t(ah, bh) + _dot_t(al, bh) + _dot_t(ah, bl)


def _dn_body(hq, hk, hv, pq, pk, pv, small, gate, convw, pvec, normw, out, xbuf, qkv, s_ref):
    n = pl.program_id(1)
    c_ = QB
    npair = DN_HEADS // 2

    @pl.when(n == 0)
    def _():
        s_ref[...] = jnp.zeros_like(s_ref)

    halo = 8
    has_prev = jnp.where(n > 0, 1.0, 0.0)
    for idx, (cur, prev) in enumerate(((hq, pq), (hk, pk), (hv, pv))):
        xbuf[idx, 0:halo, :] = prev[...] * has_prev
        xbuf[idx, halo:, :] = cur[...]
        acc = None
        for j in range(CONV_K):
            start = halo - (CONV_K - 1) + j
            t = xbuf[idx, start:start + c_, :] * convw[j:j + 1, idx * DN_WIDTH:(idx + 1) * DN_WIDTH]
            acc = t if acc is None else acc + t
        qkv[idx] = acc * _sigmoid(acc)

    lane = lax.broadcasted_iota(I32, (c_, LANES), 1)
    low = lane < HEAD_DIM
    row = lax.broadcasted_iota(I32, (c_, c_), 0)
    col = lax.broadcasted_iota(I32, (c_, c_), 1)
    lower = row >= col
    strict = row > col
    eye = jnp.where(row == col, 1.0, 0.0)
    blockdiag = (row < HEAD_DIM) == (col < HEAD_DIM)

    sm = small[...]
    beta_all = _sigmoid(sm)
    z = sm + pvec[1:2, :]
    softplus = jnp.maximum(z, 0.0) + jnp.log(1.0 + jnp.exp(-jnp.abs(z)))
    g_all = -jnp.exp(pvec[0:1, :]) * softplus
    ltri = jnp.where(lower, 1.0, 0.0).astype(BF16)
    g1 = g_all.astype(BF16)
    rem = g_all - g1.astype(F32)
    g2 = rem.astype(BF16)
    g3 = (rem - g2.astype(F32)).astype(BF16)
    gc_all = _dot(ltri, g1) + _dot(ltri, g2) + _dot(ltri, g3)
    gc_t = gc_all.T

    def headsum(x):
        sa = jnp.sum(jnp.where(low, x, 0.0), axis=-1, keepdims=True)
        sb = jnp.sum(jnp.where(low, 0.0, x), axis=-1, keepdims=True)
        return jnp.where(low, sa, sb)

    def col_of(arr, j):
        return arr[:, j:j + 1]

    pair = []
    for p in range(npair):
        cs = slice(p * LANES, (p + 1) * LANES)
        q, k, v = qkv[0, :, cs], qkv[1, :, cs], qkv[2, :, cs]
        q = q * lax.rsqrt(headsum(q * q) + NORM_EPS) * (HEAD_DIM ** -0.5)
        k = k * lax.rsqrt(headsum(k * k) + NORM_EPS)
        beta = jnp.where(low, col_of(beta_all, 2 * p), col_of(beta_all, 2 * p + 1))
        gc = jnp.where(low, col_of(gc_all, DN_HEADS + 2 * p), col_of(gc_all, DN_HEADS + 2 * p + 1))
        eg = jnp.exp(gc)
        rhs = jnp.concatenate([v * beta, k * beta * eg], axis=1)
        pair.append(dict(q=q, k=k, kb=k.astype(BF16), gc=gc, eg=eg, rhs=rhs))

    xs, pws, qks = [], [], []
    for h in range(DN_HEADS):
        pr = pair[h // 2]
        sel = low if h % 2 == 0 else ~low
        gcol = col_of(gc_all, DN_HEADS + h)
        grow = gc_t[DN_HEADS + h:DN_HEADS + h + 1, :]
        dec = jnp.where(lower, jnp.exp(jnp.minimum(gcol - grow, 0.0)), 0.0)
        kk = _dot_t(jnp.where(sel, pr["k"], 0.0).astype(BF16), pr["kb"])
        qk = _dot_t_hp(jnp.where(sel, pr["q"], 0.0), pr["k"])
        nm = jnp.where(strict, col_of(beta_all, h) * kk * dec, 0.0)
        xs.append(eye - nm)
        pws.append(nm)
        qks.append((qk * dec).astype(BF16))

    for _ in range(int(math.log2(c_)) - 1):
        for h in range(DN_HEADS):
            pws[h] = _dot_hp(pws[h], pws[h])
        for h in range(DN_HEADS):
            xs[h] = xs[h] + _dot_hp(xs[h], pws[h])

    for p in range(npair):
        pr = pair[p]
        cs = slice(p * LANES, (p + 1) * LANES)
        uw = [_dot_hp(xs[2 * p + hh], pr["rhs"]) for hh in range(2)]
        u = jnp.where(low, uw[0][:, :LANES], uw[1][:, :LANES])
        w = jnp.where(low, uw[0][:, LANES:], uw[1][:, LANES:])
        state = s_ref[p]
        sb16 = state.astype(BF16)
        vnew = u - _dot(w.astype(BF16), sb16)
        vn16 = vnew.astype(BF16)
        o = _dot((pr["q"] * pr["eg"]).astype(BF16), sb16)
        o = o + jnp.where(low, _dot(qks[2 * p], vn16), _dot(qks[2 * p + 1], vn16))
        glast = pr["gc"][c_ - 1:c_, :]
        kd = pr["k"] * jnp.exp(glast - pr["gc"])
        upd = _dot(kd.T.astype(BF16), vn16)
        s_ref[p] = state * jnp.exp(glast) + jnp.where(blockdiag, upd, 0.0)
        ms = headsum(o * o) * (1.0 / HEAD_DIM)
        gt = gate[:, cs]
        out[:, cs] = (o * lax.rsqrt(ms + NORM_EPS) * normw[:, cs] * (gt * _sigmoid(gt))).astype(out.dtype)


def _delta_net(hr, convw, pvec, normw, batch, seq):
    h3 = hr.reshape(batch, seq, R_COLS)
    nchunks = seq // QB
    qcol = R_DN // DN_WIDTH

    def cur(off):
        return pl.BlockSpec((None, QB, DN_WIDTH), lambda b, n: (b, n, qcol + off))

    def prev(off):
        return pl.BlockSpec((None, 8, DN_WIDTH),
                            lambda b, n: (b, jnp.maximum(n * (QB // 8) - 1, 0), qcol + off))

    out = pl.pallas_call(
        _dn_body, grid=(batch, nchunks),
        in_specs=[cur(0), cur(1), cur(2), prev(0), prev(1), prev(2),
                  pl.BlockSpec((None, QB, LANES), lambda b, n: (b, n, R_SMALL // LANES)),
                  pl.BlockSpec((None, QB, DN_WIDTH), lambda b, n: (b, n, R_GATE // DN_WIDTH)),
                  pl.BlockSpec((CONV_K, 3 * DN_WIDTH), lambda b, n: (0, 0)),
                  pl.BlockSpec((2, LANES), lambda b, n: (0, 0)),
                  pl.BlockSpec((1, DN_WIDTH), lambda b, n: (0, 0))],
        out_specs=pl.BlockSpec((None, QB, DN_WIDTH), lambda b, n: (b, n, 0)),
        out_shape=jax.ShapeDtypeStruct((batch, seq, DN_WIDTH), BF16),
        scratch_shapes=[pltpu.VMEM((3, QB + 8, DN_WIDTH), F32),
                        pltpu.VMEM((3, QB, DN_WIDTH), F32),
                        pltpu.VMEM((DN_HEADS // 2, LANES, LANES), F32)],
        compiler_params=_params("parallel", "arbitrary"),
        name="gated_delta_rule")(h3, h3, h3, h3, h3, h3, h3, h3, convw, pvec, normw)
    return out.reshape(batch * seq, DN_WIDTH)


def _pool_body(cur, prev, wbd, scale, out, xbuf, *, rows):
    n = pl.program_id(1)
    halo = 16
    xbuf[0:halo, :] = prev[...] * jnp.where(n > 0, 1.0, 0.0)
    xbuf[halo:, :] = cur[...]
    u = cur[...]
    lane = lax.broadcasted_iota(I32, (rows, POOL_WIDTH), 1)
    tpos = n * rows + lax.broadcasted_iota(I32, (rows, POOL_WIDTH), 0)
    run = u
    pooled = jnp.zeros_like(u)
    for j in range(1, max(POOL_WINDOWS)):
        run = run + xbuf[halo - j:halo - j + rows, :]
        if j + 1 in POOL_WINDOWS:
            gi = POOL_WINDOWS.index(j + 1)
            cnt = jnp.minimum(tpos + 1, j + 1).astype(F32)
            sel = (lane >= gi * POOL_GDIM) & (lane < (gi + 1) * POOL_GDIM)
            pooled = jnp.where(sel, run / cnt - u, pooled)
    mixed = jnp.dot(pooled.astype(BF16), wbd[...], preferred_element_type=F32)
    out[...] = (mixed * scale[...]).astype(out.dtype)


def _pool(hr, wbd, scale, batch, seq, rows=512):
    h3 = hr.reshape(batch, seq, R_COLS)
    out = pl.pallas_call(
        functools.partial(_pool_body, rows=rows), grid=(batch, seq // rows),
        in_specs=[pl.BlockSpec((None, rows, POOL_WIDTH), lambda b, n: (b, n, R_POOL // POOL_WIDTH)),
                  pl.BlockSpec((None, 16, POOL_WIDTH),
                               lambda b, n: (b, jnp.maximum(n * (rows // 16) - 1, 0), R_POOL // POOL_WIDTH)),
                  pl.BlockSpec((POOL_WIDTH, POOL_WIDTH), lambda b, n: (0, 0)),
                  pl.BlockSpec((1, POOL_WIDTH), lambda b, n: (0, 0))],
        out_specs=pl.BlockSpec((None, rows, POOL_WIDTH), lambda b, n: (b, n, 0)),
        out_shape=jax.ShapeDtypeStruct((batch, seq, POOL_WIDTH), BF16),
        scratch_shapes=[pltpu.VMEM((rows + 16, POOL_WIDTH), F32)],
        compiler_params=_params("parallel", "arbitrary"), name="multiscale_pool")(h3, h3, wbd, scale)
    return out.reshape(batch * seq, POOL_WIDTH)


def _proj_ln_body(a, d, p, x, wa, wd, wp, g, b, o):
    y = jnp.dot(a[...], wa[...], preferred_element_type=F32)
    y = y + jnp.dot(d[...], wd[...], preferred_element_type=F32)
    y = y + jnp.dot(p[...], wp[...], preferred_element_type=F32)
    o[...] = _layer_norm_rows(ALPHA * x[...] + y, g[...], b[...])


def _proj_ln(att, dn, pool, x, wa, wd, wp, g, b, tm=512):
    t = x.shape[0]

    def rows(wd_):
        return pl.BlockSpec((tm, wd_), lambda i: (i, 0))

    def full(a_):
        return pl.BlockSpec(a_.shape, lambda i: (0, 0))

    return pl.pallas_call(
        _proj_ln_body, grid=(t // tm,),
        in_specs=[rows(ATT_WIDTH), rows(DN_WIDTH), rows(POOL_WIDTH), rows(D_MODEL),
                  full(wa), full(wd), full(wp), full(g), full(b)],
        out_specs=rows(D_MODEL), out_shape=jax.ShapeDtypeStruct((t, D_MODEL), F32),
        compiler_params=_params("parallel"), name="out_proj_ln")(att, dn, pool, x, wa, wd, wp, g, b)


def _xattn_body(x, wq, k, v, wo, g, b, o):
    xv = x[...]
    q = jnp.dot(xv.astype(BF16), wq[...], preferred_element_type=F32).astype(BF16)
    heads = []
    for h in range(X_HEADS):
        cs = slice(h * X_HEAD_DIM, (h + 1) * X_HEAD_DIM)
        s = lax.dot_general(q[:, cs], k[:, cs], (((1,), (1,)), ((), ())), preferred_element_type=F32)
        m = jnp.max(s, axis=-1, keepdims=True)
        e = jnp.exp(s - m)
        l = jnp.sum(e, axis=-1, keepdims=True)
        oh = jnp.dot(e.astype(BF16), v[:, cs], preferred_element_type=F32) / l
        heads.append(oh.astype(BF16))
    oc = jnp.concatenate(heads, axis=1)
    y = jnp.dot(oc, wo[...], preferred_element_type=F32)
    o[...] = _layer_norm_rows(ALPHA * xv + y, g[...], b[...])


def _cross_attention(x, wq, k, v, wo, g, b, seq, tm=512):
    t = x.shape[0]
    mem_len = k.shape[1]
    per_batch = seq // tm

    def full(a_):
        return pl.BlockSpec(a_.shape, lambda i: (0, 0))

    kv = pl.BlockSpec((None, mem_len, D_MODEL), lambda i: (i // per_batch, 0, 0))
    rows = pl.BlockSpec((tm, D_MODEL), lambda i: (i, 0))
    return pl.pallas_call(
        _xattn_body, grid=(t // tm,),
        in_specs=[rows, full(wq), kv, kv, full(wo), full(g), full(b)],
        out_specs=rows, out_shape=jax.ShapeDtypeStruct((t, D_MODEL), F32),
        compiler_params=_params("parallel"), name="cross_attention_ln")(x, wq, k, v, wo, g, b)


def _ffn_body(x, w1, w3, w2, g, b, o, xb, acc):
    f = pl.program_id(1)

    @pl.when(f == 0)
    def _():
        xb[...] = x[...].astype(BF16)
        acc[...] = jnp.zeros_like(acc)

    xv = xb[...]
    h1 = jnp.dot(xv, w1[...], preferred_element_type=F32)
    h3 = jnp.dot(xv, w3[...], preferred_element_type=F32)
    hh = (h1 * _sigmoid(h1) * h3).astype(BF16)
    acc[...] += jnp.dot(hh, w2[...], preferred_element_type=F32)

    @pl.when(f == pl.num_programs(1) - 1)
    def _():
        o[...] = _layer_norm_rows(ALPHA * x[...] + acc[...], g[...], b[...])


def _ffn_dense(x, w1, w3, w2, g, b, tm=1024, tf=256):
    t = x.shape[0]
    dff = w1.shape[1]
    rows = pl.BlockSpec((tm, D_MODEL), lambda i, f: (i, 0))
    vec = pl.BlockSpec((1, D_MODEL), lambda i, f: (0, 0))
    return pl.pallas_call(
        _ffn_body, grid=(t // tm, dff // tf),
        in_specs=[rows, pl.BlockSpec((D_MODEL, tf), lambda i, f: (0, f)),
                  pl.BlockSpec((D_MODEL, tf), lambda i, f: (0, f)),
                  pl.BlockSpec((tf, D_MODEL), lambda i, f: (f, 0)), vec, vec],
        out_specs=rows, out_shape=jax.ShapeDtypeStruct((t, D_MODEL), F32),
        scratch_shapes=[pltpu.VMEM((tm, D_MODEL), BF16), pltpu.VMEM((tm, D_MODEL), F32)],
        compiler_params=_params("parallel", "arbitrary"), name="ffn_swiglu_ln")(x, w1, w3, w2, g, b)


def _router_body(x, rw, eidx, gates, ranks, counts, carry, *, tm):
    i = pl.program_id(0)

    @pl.when(i == 0)
    def _():
        carry[...] = jnp.zeros_like(carry)

    logits = jnp.dot(x[...], rw[...], preferred_element_type=F32, precision=lax.Precision.HIGHEST)
    lane = lax.broadcasted_iota(I32, (tm, LANES), 1)
    lanef = lane.astype(F32)
    lg = jnp.where(lane < N_EXPERTS, logits, NEG)
    m1 = jnp.max(lg, axis=-1, keepdims=True)
    i1 = jnp.min(jnp.where(lg == m1, lanef, float(LANES)), axis=-1, keepdims=True)
    lg2 = jnp.where(lanef == i1, NEG, lg)
    m2 = jnp.max(lg2, axis=-1, keepdims=True)
    i2 = jnp.min(jnp.where(lg2 == m2, lanef, float(LANES)), axis=-1, keepdims=True)
    e21 = jnp.exp(m2 - m1)
    g1 = 1.0 / (1.0 + e21)
    g2 = e21 * g1
    hot1 = lanef == i1
    hot2 = lanef == i2
    onehot = jnp.where(hot1 | hot2, 1.0, 0.0)
    r = lax.broadcasted_iota(I32, (tm, tm), 0)
    c = lax.broadcasted_iota(I32, (tm, tm), 1)
    before = jnp.dot(jnp.where(r > c, 1.0, 0.0).astype(BF16), onehot.astype(BF16),
                     preferred_element_type=F32) + carry[...]
    rank1 = jnp.sum(jnp.where(hot1, before, 0.0), axis=-1, keepdims=True)
    rank2 = jnp.sum(jnp.where(hot2, before, 0.0), axis=-1, keepdims=True)
    carry[...] = carry[...] + jnp.sum(onehot, axis=0, keepdims=True)
    eidx[...] = jnp.where(lane == 0, i1, jnp.where(lane == 1, i2, 0.0)).astype(I32)
    gates[...] = jnp.where(lane == 0, g1, jnp.where(lane == 1, g2, 0.0))
    ranks[...] = jnp.where(lane == 0, rank1, jnp.where(lane == 1, rank2, 0.0)).astype(I32)
    counts[...] = jnp.broadcast_to(carry[...], counts.shape).astype(I32)


def _router(x, rw, tm=512):
    t = x.shape[0]
    rows = pl.BlockSpec((tm, LANES), lambda i: (i, 0))
    return pl.pallas_call(
        functools.partial(_router_body, tm=tm), grid=(t // tm,),
        in_specs=[pl.BlockSpec((tm, D_MODEL), lambda i: (i, 0)),
                  pl.BlockSpec((D_MODEL, LANES), lambda i: (0, 0))],
        out_specs=[rows, rows, rows, pl.BlockSpec((8, LANES), lambda i: (0, 0))],
        out_shape=[jax.ShapeDtypeStruct((t, LANES), I32), jax.ShapeDtypeStruct((t, LANES), F32),
                   jax.ShapeDtypeStruct((t, LANES), I32), jax.ShapeDtypeStruct((8, LANES), I32)],
        scratch_shapes=[pltpu.VMEM((1, LANES), F32)],
        compiler_params=_params("arbitrary"), name="router_top2")(x, rw)


def _moe_body(be, nused, xs, w1, w3, w2, ys, acc):
    i = pl.program_id(0)
    f = pl.program_id(1)
    last = f == pl.num_programs(1) - 1
    used = i < nused[0]

    @pl.when(used & (f == 0))
    def _():
        acc[...] = jnp.zeros_like(acc)

    @pl.when(used)
    def _():
        xv = xs[...]
        h1 = jnp.dot(xv, w1[...], preferred_element_type=F32)
        h3 = jnp.dot(xv, w3[...], preferred_element_type=F32)
        hh = (h1 * _sigmoid(h1) * h3).astype(BF16)
        acc[...] += jnp.dot(hh, w2[...], preferred_element_type=F32)

    @pl.when(used & last)
    def _():
        ys[...] = acc[...]

    @pl.when(jnp.logical_not(used) & last)
    def _():
        ys[...] = jnp.zeros_like(ys)


def _moe_experts(block_e, n_used, xs, w1, w3, w2, tm, tf=512):
    n_slots = xs.shape[0]
    dex = w1.shape[2]
    nblk = n_slots // tm

    def live(i, nu):
        return jnp.minimum(i, nu[0] - 1)

    grid_spec = pltpu.PrefetchScalarGridSpec(
        num_scalar_prefetch=2, grid=(nblk, dex // tf),
        in_specs=[pl.BlockSpec((tm, D_MODEL), lambda i, f, be, nu: (live(i, nu), 0)),
                  pl.BlockSpec((None, D_MODEL, tf), lambda i, f, be, nu: (be[i], 0, jnp.where(i < nu[0], f, dex // tf - 1))),
                  pl.BlockSpec((None, D_MODEL, tf), lambda i, f, be, nu: (be[i], 0, jnp.where(i < nu[0], f, dex // tf - 1))),
                  pl.BlockSpec((None, tf, D_MODEL), lambda i, f, be, nu: (be[i], jnp.where(i < nu[0], f, dex // tf - 1), 0))],
        out_specs=pl.BlockSpec((tm, D_MODEL), lambda i, f, be, nu: (i, 0)),
        scratch_shapes=[pltpu.VMEM((tm, D_MODEL), F32)])
    return pl.pallas_call(
        _moe_body, grid_spec=grid_spec, out_shape=jax.ShapeDtypeStruct((n_slots, D_MODEL), F32),
        compiler_params=_params("arbitrary", "arbitrary"), name="moe_experts")(block_e, n_used, xs, w1, w3, w2)


def _combine_ln_body(x, ya, yb, gates, g, b, o):
    gt = gates[...]
    y = ya[...] * gt[:, 0:1] + yb[...] * gt[:, 1:2]
    o[...] = _layer_norm_rows(ALPHA * x[...] + y, g[...], b[...])


def _combine_ln(x, ya, yb, gates, g, b, tm=512):
    t = x.shape[0]
    rows = pl.BlockSpec((tm, D_MODEL), lambda i: (i, 0))
    vec = pl.BlockSpec((1, D_MODEL), lambda i: (0, 0))
    return pl.pallas_call(
        _combine_ln_body, grid=(t // tm,),
        in_specs=[rows, rows, rows, pl.BlockSpec((tm, LANES), lambda i: (i, 0)), vec, vec],
        out_specs=rows, out_shape=jax.ShapeDtypeStruct((t, D_MODEL), F32),
        compiler_params=_params("parallel"), name="moe_combine_ln")(x, ya, yb, gates, g, b)


def _moe_layer(x, rw_pad, w1, w3, w2, g, b, tm=512):
    t = x.shape[0]
    eidx, gates, ranks, counts = _router(x, rw_pad)
    cnt = counts[0, :N_EXPERTS]
    padded = (cnt + tm - 1) // tm * tm
    pad_end = jnp.cumsum(padded)
    pad_start = pad_end - padded
    e2 = eidx[:, :2]
    dest = pad_start[e2] + ranks[:, :2]
    nblk = (2 * t) // tm + N_EXPERTS
    n_slots = nblk * tm
    tok = jnp.broadcast_to(jnp.arange(t, dtype=I32)[:, None], (t, 2))
    slot_tok = jnp.zeros((n_slots,), I32).at[dest.reshape(-1)].set(tok.reshape(-1))
    n_used = (pad_end[-1] // tm).astype(I32).reshape(1)
    blk = jnp.arange(nblk, dtype=I32) * tm
    block_e = jnp.minimum(jnp.sum(pad_end[None, :] <= blk[:, None], axis=1), N_EXPERTS - 1).astype(I32)
    block_e = jnp.where(jnp.arange(nblk) < n_used[0], block_e, block_e[jnp.maximum(n_used[0] - 1, 0)])
    xs = x.astype(BF16)[slot_tok]
    ys = _moe_experts(block_e, n_used, xs, w1, w3, w2, tm)
    return _combine_ln(x, ys[dest[:, 0]], ys[dest[:, 1]], gates, g, b)


def _block_diag(w):
    g, c, _ = w.shape
    eye = jnp.eye(g, dtype=w.dtype)
    return (eye[:, None, :, None] * w[:, :, None, :]).reshape(g * c, g * c)


def kernel(x, mem, positions, w_in, conv_w, a_log, dt_bias, dn_norm_w, pool_w, pool_scale, w_out, ln_mix_g, ln_mix_b, xq_w, xk_w, xv_w, xo_w, ln_x_g, ln_x_b, ffn_w1, ffn_w3, ffn_w2, router_w, moe_w1, moe_w3, moe_w2, ln_ffn_g, ln_ffn_b):
    batch, seq, _ = x.shape
    t = batch * seq
    depth = w_in.shape[0]
    xf = x.reshape(t, D_MODEL)
    memf = mem.reshape(batch * mem.shape[1], D_MODEL)

    posc, posr = {}, {}
    for dil in DILATIONS:
        pt = positions.reshape(batch, seq // dil, dil).transpose(0, 2, 1)
        posc[dil] = pt[..., None]
        posr[dil] = pt[:, :, None, :]

    for l in range(depth):
        wl = w_in[l]
        small = jnp.pad(wl[:, OFF_BETA:OFF_GATE], ((0, 0), (0, LANES - 2 * DN_HEADS)))
        w_rest = jnp.concatenate([wl[:, OFF_DN:OFF_BETA], wl[:, OFF_GATE:OFF_POOL], wl[:, OFF_POOL:], small],
                                 axis=1).astype(BF16)
        slabs = _matmul_slabs(xf, wl[:, :OFF_DN].astype(BF16), 512)
        hr = _matmul(xf, w_rest, F32, 1024, R_COLS // 3)

        branches = [_att_branch(slabs, posc[d], posr[d], d, batch, seq) for d in DILATIONS]
        att = _att_merge([o for o, _ in branches], [s for _, s in branches])

        pvec = jnp.zeros((2, LANES), F32)
        pvec = pvec.at[0, DN_HEADS:2 * DN_HEADS].set(a_log[l]).at[1, DN_HEADS:2 * DN_HEADS].set(dt_bias[l])
        normw = jnp.tile(dn_norm_w[l], DN_HEADS)[None, :]
        dn = _delta_net(hr, conv_w[l], pvec, normw, batch, seq)

        pool = _pool(hr, _block_diag(pool_w[l]).astype(BF16), pool_scale[l][None, :], batch, seq)

        wo = w_out[l].astype(BF16)
        xf = _proj_ln(att, dn, pool, xf, wo[:ATT_WIDTH], wo[ATT_WIDTH:ATT_WIDTH + DN_WIDTH],
                      wo[ATT_WIDTH + DN_WIDTH:], ln_mix_g[l][None, :], ln_mix_b[l][None, :])

        wkv = jnp.concatenate([xk_w[l], xv_w[l]], axis=1).astype(BF16)
        kv = _matmul(memf, wkv, BF16, 512, 1024).reshape(batch, mem.shape[1], 2 * D_MODEL)
        xf = _cross_attention(xf, (xq_w[l] * (X_HEAD_DIM ** -0.5)).astype(BF16), kv[..., :D_MODEL],
                              kv[..., D_MODEL:], xo_w[l].astype(BF16), ln_x_g[l][None, :],
                              ln_x_b[l][None, :], seq)

        j = l // 2
        gl, bl = ln_ffn_g[l][None, :], ln_ffn_b[l][None, :]
        if l % 2 == 0:
            xf = _ffn_dense(xf, ffn_w1[j].astype(BF16), ffn_w3[j].astype(BF16), ffn_w2[j].astype(BF16), gl, bl)
        else:
            rw_pad = jnp.pad(router_w[j], ((0, 0), (0, LANES - N_EXPERTS)))
            xf = _moe_layer(xf, rw_pad, moe_w1[j].astype(BF16), moe_w3[j].astype(BF16),
                            moe_w2[j].astype(BF16), gl, bl)
    return xf.reshape(batch, seq, D_MODEL)
```

```python
import functools
import math

import jax
import jax.numpy as jnp
from jax import lax
from jax.experimental import pallas as pl
from jax.experimental.pallas import tpu as pltpu

F32, BF16, I32 = jnp.float32, jnp.bfloat16, jnp.int32

D_MODEL = 1024
HEAD_DIM = 64
ATT_HEADS = 6
ATT_WIDTH = ATT_HEADS * HEAD_DIM
DN_HEADS = 6
DN_WIDTH = DN_HEADS * HEAD_DIM
POOL_WIDTH = 256
POOL_WINDOWS = (2, 4, 8, 16)
POOL_GDIM = POOL_WIDTH // len(POOL_WINDOWS)
DILATIONS = (1, 4, 16)
CONV_K = 4
X_HEADS = 4
X_HEAD_DIM = D_MODEL // X_HEADS
N_EXPERTS = 8
DEPTH = 4
ALPHA = (2 * DEPTH) ** 0.25
LN_EPS = 1e-5
NORM_EPS = 1e-6

OFF_DN = 3 * ATT_WIDTH
OFF_BETA = OFF_DN + 3 * DN_WIDTH
OFF_DECAY = OFF_BETA + DN_HEADS
OFF_GATE = OFF_DECAY + DN_HEADS
OFF_POOL = OFF_GATE + DN_WIDTH

LANES = 128
R_DN = 0
R_GATE = R_DN + 3 * DN_WIDTH
R_POOL = R_GATE + DN_WIDTH
R_SMALL = R_POOL + POOL_WIDTH
R_COLS = R_SMALL + LANES

QB = 128
NEG = -1e30
VMEM_LIMIT = 48 * 1024 * 1024


def _params(*sem):
    return pltpu.CompilerParams(dimension_semantics=sem, vmem_limit_bytes=VMEM_LIMIT)


def _sigmoid(x):
    return 1.0 / (1.0 + jnp.exp(-x))


def _layer_norm_rows(z, g, b):
    mu = jnp.mean(z, axis=-1, keepdims=True)
    zc = z - mu
    var = jnp.mean(zc * zc, axis=-1, keepdims=True)
    return zc * lax.rsqrt(var + LN_EPS) * g + b


def _rows(start, size, stride):
    return pl.ds(start, size) if stride == 1 else pl.ds(start, size, stride=stride)


def _mm_body(x_ref, w_ref, o_ref):
    o_ref[...] = jnp.dot(x_ref[...].astype(BF16), w_ref[...],
                         preferred_element_type=F32).astype(o_ref.dtype)


def _matmul(x, w, out_dtype, tm, tn):
    m, k = x.shape
    n = w.shape[1]
    return pl.pallas_call(
        _mm_body, grid=(m // tm, n // tn),
        in_specs=[pl.BlockSpec((tm, k), lambda i, j: (i, 0)),
                  pl.BlockSpec((k, tn), lambda i, j: (0, j))],
        out_specs=pl.BlockSpec((tm, tn), lambda i, j: (i, j)),
        out_shape=jax.ShapeDtypeStruct((m, n), out_dtype),
        compiler_params=_params("parallel", "arbitrary"), name="matmul")(x, w)


def _mm_slab_body(x_ref, w_ref, o_ref):
    res = jnp.dot(x_ref[...].astype(BF16), w_ref[...], preferred_element_type=F32)
    for j in range(o_ref.shape[0]):
        o_ref[j] = res[:, j * LANES:(j + 1) * LANES]


def _matmul_slabs(x, w, tm):
    m, k = x.shape
    n = w.shape[1]
    return pl.pallas_call(
        _mm_slab_body, grid=(m // tm,),
        in_specs=[pl.BlockSpec((tm, k), lambda i: (i, 0)), pl.BlockSpec((k, n), lambda i: (0, 0))],
        out_specs=pl.BlockSpec((n // LANES, tm, LANES), lambda i: (0, i, 0)),
        out_shape=jax.ShapeDtypeStruct((n // LANES, m, LANES), F32),
        compiler_params=_params("parallel"), name="matmul_slabs")(x, w)


def _att_body(q_ref, kc_ref, kp_ref, vc_ref, vp_ref, pq_ref, pkc_ref, pkp_ref, o_ref, lse_ref,
              kbuf, vbuf, pkbuf, *, dil, rows, slopes):
    n = pl.program_id(1)
    nsub = rows // QB
    npair = ATT_HEADS // 2
    a = lax.broadcasted_iota(I32, (QB, 2 * QB), 0)
    c = lax.broadcasted_iota(I32, (QB, 2 * QB), 1)
    tri = jnp.where(c < QB, c - a, a - (c - QB)) >= 0
    lane = lax.broadcasted_iota(I32, (QB, LANES), 1)
    low = lane < HEAD_DIM

    def residue(r, carry):
        for p in range(npair):
            kbuf[p, 0:QB, :] = kp_ref[p, _rows(r, QB, dil), :].astype(BF16)
            kbuf[p, QB:, :] = kc_ref[p, _rows(r, rows, dil), :].astype(BF16)
            vbuf[p, 0:QB, :] = vp_ref[p, _rows(r, QB, dil), :].astype(BF16)
            vbuf[p, QB:, :] = vc_ref[p, _rows(r, rows, dil), :].astype(BF16)
        pkbuf[0] = pkp_ref[r]
        for j in range(nsub):
            pkbuf[j + 1] = pkc_ref[r, :, j * QB:(j + 1) * QB]

        def sub(sb, carry2):
            r0 = pl.multiple_of(sb * QB, QB)
            t0 = r + r0 * dil
            pq = pq_ref[r, pl.ds(r0, QB), :]
            pk = jnp.concatenate([pkbuf[sb], pkbuf[sb + 1]], axis=1)
            dist = (pq - pk).astype(F32)
            mask = tri & (c >= jnp.where((sb > 0) | (n > 0), 0, QB))
            lse = jnp.zeros((QB, LANES), F32)
            for p in range(npair):
                qp = q_ref[p, _rows(t0, QB, dil), :] * (HEAD_DIM ** -0.5)
                kp = kbuf[p, pl.ds(r0, 2 * QB), :]
                vp = vbuf[p, pl.ds(r0, 2 * QB), :]
                outs = []
                for hh in range(2):
                    h = 2 * p + hh
                    qm = jnp.where(low if hh == 0 else ~low, qp, 0.0).astype(BF16)
                    s = lax.dot_general(qm, kp, (((1,), (1,)), ((), ())), preferred_element_type=F32)
                    s = jnp.where(mask, s - slopes[h] * dist, NEG)
                    m = jnp.max(s, axis=-1, keepdims=True)
                    e = jnp.exp(s - m)
                    l = jnp.sum(e, axis=-1, keepdims=True)
                    o = jnp.dot(e.astype(BF16), vp, preferred_element_type=F32)
                    outs.append(o / l)
                    lse = jnp.where(lane == h, m + jnp.log(l), lse)
                o_ref[p, _rows(t0, QB, dil), :] = jnp.where(low, outs[0], outs[1])
            lse_ref[_rows(t0, QB, dil), :] = lse
            return carry2

        lax.fori_loop(0, nsub, sub, 0)
        return carry

    lax.fori_loop(0, dil, residue, 0)


def _att_branch(slabs, posc, posr, dil, batch, seq):
    sub_len = seq // dil
    rows = min(512 if dil == 1 else 2048 // dil, sub_len)
    tok = rows * dil
    nsteps = seq // tok
    prev_tok = QB * dil
    ratio = tok // prev_tok
    npair = ATT_HEADS // 2
    slopes = tuple(2.0 ** (-8.0 * (i + 1) / ATT_HEADS) for i in range(ATT_HEADS))

    def cur(which):
        return pl.BlockSpec((npair, tok, LANES), lambda b, n: (which, b * nsteps + n, 0))

    def prev(which):
        return pl.BlockSpec((npair, prev_tok, LANES),
                            lambda b, n: (which, jnp.maximum((b * nsteps + n) * ratio - 1, 0), 0))

    t = batch * seq
    return pl.pallas_call(
        functools.partial(_att_body, dil=dil, rows=rows, slopes=slopes),
        grid=(batch, nsteps),
        in_specs=[cur(0), cur(1), prev(1), cur(2), prev(2),
                  pl.BlockSpec((None, dil, rows, 1), lambda b, n: (b, 0, n, 0)),
                  pl.BlockSpec((None, dil, 1, rows), lambda b, n: (b, 0, 0, n)),
                  pl.BlockSpec((None, dil, 1, QB),
                               lambda b, n: (b, 0, 0, jnp.maximum(n * (rows // QB) - 1, 0)))],
        out_specs=[pl.BlockSpec((npair, tok, LANES), lambda b, n: (0, b * nsteps + n, 0)),
                   pl.BlockSpec((tok, LANES), lambda b, n: (b * nsteps + n, 0))],
        out_shape=[jax.ShapeDtypeStruct((npair, t, LANES), F32),
                   jax.ShapeDtypeStruct((t, LANES), F32)],
        scratch_shapes=[pltpu.VMEM((npair, rows + QB, LANES), BF16),
                        pltpu.VMEM((npair, rows + QB, LANES), BF16),
                        pltpu.VMEM((rows // QB + 1, 1, QB), I32)],
        compiler_params=_params("parallel", "arbitrary"),
        name=f"dilated_att_d{dil}")(slabs, slabs, slabs, slabs, slabs, posc, posr, posr)


def _merge_body(o1, o2, o3, l1, l2, l3, out):
    ls = [l1[...], l2[...], l3[...]]
    m = jnp.maximum(jnp.maximum(ls[0], ls[1]), ls[2])
    ws = [jnp.exp(l - m) for l in ls]
    inv = 1.0 / (ws[0] + ws[1] + ws[2])
    ws = [w * inv for w in ws]
    lane = lax.broadcasted_iota(I32, (out.shape[0], LANES), 1)
    low = lane < HEAD_DIM
    for p in range(ATT_HEADS // 2):
        acc = None
        for w, o in zip(ws, (o1, o2, o3)):
            wp = jnp.where(low, w[:, 2 * p:2 * p + 1], w[:, 2 * p + 1:2 * p + 2])
            t = wp * o[p]
            acc = t if acc is None else acc + t
        out[:, p * LANES:(p + 1) * LANES] = acc.astype(out.dtype)


def _att_merge(os_, ls_, tm=1024):
    npair, t, _ = os_[0].shape
    ospec = pl.BlockSpec((npair, tm, LANES), lambda i: (0, i, 0))
    lspec = pl.BlockSpec((tm, LANES), lambda i: (i, 0))
    return pl.pallas_call(
        _merge_body, grid=(t // tm,), in_specs=[ospec] * 3 + [lspec] * 3,
        out_specs=pl.BlockSpec((tm, ATT_WIDTH), lambda i: (i, 0)),
        out_shape=jax.ShapeDtypeStruct((t, ATT_WIDTH), BF16),
        compiler_params=_params("parallel"), name="att_merge")(*os_, *ls_)


def _split2(a):
    hi = a.astype(BF16)
    lo = (a - hi.astype(F32)).astype(BF16)
    return hi, lo


def _dot(a, b):
    return jnp.dot(a, b, preferred_element_type=F32)


def _dot_bf16(a, b):
    return _dot(a.astype(BF16), b.astype(BF16))


def _dot_hp(a, b):
    ah, al = _split2(a)
    bh, bl = _split2(b)
    return _dot(ah, bh) + _dot(al, bh) + _dot(ah, bl)


def _dot_t(a, b):
    return lax.dot_general(a, b, (((1,), (1,)), ((), ())), preferred_element_type=F32)


def _dot_t_hp(a, b):
    ah, al = _split2(a)
    bh, bl = _split2(b)
    return _dot_t(ah, bh) + _dot_t(al, bh) + _dot_t(ah, bl)


def _dn_body(hq, hk, hv, pq, pk, pv, small, gate, convw, pvec, normw, out, xbuf, qkv, s_ref):
    n = pl.program_id(1)
    c_ = QB
    npair = DN_HEADS // 2

    @pl.when(n == 0)
    def _():
        s_ref[...] = jnp.zeros_like(s_ref)

    halo = 8
    has_prev = jnp.where(n > 0, 1.0, 0.0)
    for idx, (cur, prev) in enumerate(((hq, pq), (hk, pk), (hv, pv))):
        xbuf[idx, 0:halo, :] = prev[...] * has_prev
        xbuf[idx, halo:, :] = cur[...]
        acc = None
        for j in range(CONV_K):
            start = halo - (CONV_K - 1) + j
            t = xbuf[idx, start:start + c_, :] * convw[j:j + 1, idx * DN_WIDTH:(idx + 1) * DN_WIDTH]
            acc = t if acc is None else acc + t
        qkv[idx] = acc * _sigmoid(acc)

    lane = lax.broadcasted_iota(I32, (c_, LANES), 1)
    low = lane < HEAD_DIM
    row = lax.broadcasted_iota(I32, (c_, c_), 0)
    col = lax.broadcasted_iota(I32, (c_, c_), 1)
    lower = row >= col
    strict = row > col
    eye = jnp.where(row == col, 1.0, 0.0)
    blockdiag = (row < HEAD_DIM) == (col < HEAD_DIM)

    sm = small[...]
    beta_all = _sigmoid(sm)
    z = sm + pvec[1:2, :]
    softplus = jnp.maximum(z, 0.0) + jnp.log(1.0 + jnp.exp(-jnp.abs(z)))
    g_all = -jnp.exp(pvec[0:1, :]) * softplus
    ltri = jnp.where(lower, 1.0, 0.0).astype(BF16)
    g1 = g_all.astype(BF16)
    rem = g_all - g1.astype(F32)
    g2 = rem.astype(BF16)
    g3 = (rem - g2.astype(F32)).astype(BF16)
    gc_all = _dot(ltri, g1) + _dot(ltri, g2) + _dot(ltri, g3)
    gc_t = gc_all.T

    def headsum(x):
        sa = jnp.sum(jnp.where(low, x, 0.0), axis=-1, keepdims=True)
        sb = jnp.sum(jnp.where(low, 0.0, x), axis=-1, keepdims=True)
        return jnp.where(low, sa, sb)

    def col_of(arr, j):
        return arr[:, j:j + 1]

    pair = []
    for p in range(npair):
        cs = slice(p * LANES, (p + 1) * LANES)
        q, k, v = qkv[0, :, cs], qkv[1, :, cs], qkv[2, :, cs]
        q = q * lax.rsqrt(headsum(q * q) + NORM_EPS) * (HEAD_DIM ** -0.5)
        k = k * lax.rsqrt(headsum(k * k) + NORM_EPS)
        beta = jnp.where(low, col_of(beta_all, 2 * p), col_of(beta_all, 2 * p + 1))
        gc = jnp.where(low, col_of(gc_all, DN_HEADS + 2 * p), col_of(gc_all, DN_HEADS + 2 * p + 1))
        eg = jnp.exp(gc)
        rhs = jnp.concatenate([v * beta, k * beta * eg], axis=1)
        pair.append(dict(q=q, k=k, kb=k.astype(BF16), gc=gc, eg=eg, rhs=rhs))

    xs, pws, qks = [], [], []
    for h in range(DN_HEADS):
        pr = pair[h // 2]
        sel = low if h % 2 == 0 else ~low
        gcol = col_of(gc_all, DN_HEADS + h)
        grow = gc_t[DN_HEADS + h:DN_HEADS + h + 1, :]
        dec = jnp.where(lower, jnp.exp(jnp.minimum(gcol - grow, 0.0)), 0.0)
        kk = _dot_t(jnp.where(sel, pr["k"], 0.0).astype(BF16), pr["kb"])
        qk = _dot_t_hp(jnp.where(sel, pr["q"], 0.0), pr["k"])
        nm = jnp.where(strict, col_of(beta_all, h) * kk * dec, 0.0)
        xs.append(eye - nm)
        pws.append(nm)
        qks.append((qk * dec).astype(BF16))

    for level in range(int(math.log2(c_)) - 1):
        mm = _dot_hp if level < 3 else _dot_bf16
        for h in range(DN_HEADS):
            pws[h] = mm(pws[h], pws[h])
        for h in range(DN_HEADS):
            xs[h] = xs[h] + mm(xs[h], pws[h])

    for p in range(npair):
        pr = pair[p]
        cs = slice(p * LANES, (p + 1) * LANES)
        uw = [_dot_hp(xs[2 * p + hh], pr["rhs"]) for hh in range(2)]
        u = jnp.where(low, uw[0][:, :LANES], uw[1][:, :LANES])
        w = jnp.where(low, uw[0][:, LANES:], uw[1][:, LANES:])
        state = s_ref[p]
        sb16 = state.astype(BF16)
        vnew = u - _dot(w.astype(BF16), sb16)
        vn16 = vnew.astype(BF16)
        o = _dot((pr["q"] * pr["eg"]).astype(BF16), sb16)
        o = o + jnp.where(low, _dot(qks[2 * p], vn16), _dot(qks[2 * p + 1], vn16))
        glast = pr["gc"][c_ - 1:c_, :]
        kd = pr["k"] * jnp.exp(glast - pr["gc"])
        upd = _dot(kd.T.astype(BF16), vn16)
        s_ref[p] = state * jnp.exp(glast) + jnp.where(blockdiag, upd, 0.0)
        ms = headsum(o * o) * (1.0 / HEAD_DIM)
        gt = gate[:, cs]
        out[:, cs] = (o * lax.rsqrt(ms + NORM_EPS) * normw[:, cs] * (gt * _sigmoid(gt))).astype(out.dtype)


def _delta_net(hr, convw, pvec, normw, batch, seq):
    h3 = hr.reshape(batch, seq, R_COLS)
    nchunks = seq // QB
    qcol = R_DN // DN_WIDTH

    def cur(off):
        return pl.BlockSpec((None, QB, DN_WIDTH), lambda b, n: (b, n, qcol + off))

    def prev(off):
        return pl.BlockSpec((None, 8, DN_WIDTH),
                            lambda b, n: (b, jnp.maximum(n * (QB // 8) - 1, 0), qcol + off))

    out = pl.pallas_call(
        _dn_body, grid=(batch, nchunks),
        in_specs=[cur(0), cur(1), cur(2), prev(0), prev(1), prev(2),
                  pl.BlockSpec((None, QB, LANES), lambda b, n: (b, n, R_SMALL // LANES)),
                  pl.BlockSpec((None, QB, DN_WIDTH), lambda b, n: (b, n, R_GATE // DN_WIDTH)),
                  pl.BlockSpec((CONV_K, 3 * DN_WIDTH), lambda b, n: (0, 0)),
                  pl.BlockSpec((2, LANES), lambda b, n: (0, 0)),
                  pl.BlockSpec((1, DN_WIDTH), lambda b, n: (0, 0))],
        out_specs=pl.BlockSpec((None, QB, DN_WIDTH), lambda b, n: (b, n, 0)),
        out_shape=jax.ShapeDtypeStruct((batch, seq, DN_WIDTH), BF16),
        scratch_shapes=[pltpu.VMEM((3, QB + 8, DN_WIDTH), F32),
                        pltpu.VMEM((3, QB, DN_WIDTH), F32),
                        pltpu.VMEM((DN_HEADS // 2, LANES, LANES), F32)],
        compiler_params=_params("parallel", "arbitrary"),
        name="gated_delta_rule")(h3, h3, h3, h3, h3, h3, h3, h3, convw, pvec, normw)
    return out.reshape(batch * seq, DN_WIDTH)


def _pool_body(cur, prev, wbd, scale, out, xbuf, *, rows):
    n = pl.program_id(1)
    halo = 16
    xbuf[0:halo, :] = prev[...] * jnp.where(n > 0, 1.0, 0.0)
    xbuf[halo:, :] = cur[...]
    u = cur[...]
    lane = lax.broadcasted_iota(I32, (rows, POOL_WIDTH), 1)
    tpos = n * rows + lax.broadcasted_iota(I32, (rows, POOL_WIDTH), 0)
    run = u
    pooled = jnp.zeros_like(u)
    for j in range(1, max(POOL_WINDOWS)):
        run = run + xbuf[halo - j:halo - j + rows, :]
        if j + 1 in POOL_WINDOWS:
            gi = POOL_WINDOWS.index(j + 1)
            cnt = jnp.minimum(tpos + 1, j + 1).astype(F32)
            sel = (lane >= gi * POOL_GDIM) & (lane < (gi + 1) * POOL_GDIM)
            pooled = jnp.where(sel, run / cnt - u, pooled)
    mixed = jnp.dot(pooled.astype(BF16), wbd[...], preferred_element_type=F32)
    out[...] = (mixed * scale[...]).astype(out.dtype)


def _pool(hr, wbd, scale, batch, seq, rows=512):
    h3 = hr.reshape(batch, seq, R_COLS)
    out = pl.pallas_call(
        functools.partial(_pool_body, rows=rows), grid=(batch, seq // rows),
        in_specs=[pl.BlockSpec((None, rows, POOL_WIDTH), lambda b, n: (b, n, R_POOL // POOL_WIDTH)),
                  pl.BlockSpec((None, 16, POOL_WIDTH),
                               lambda b, n: (b, jnp.maximum(n * (rows // 16) - 1, 0), R_POOL // POOL_WIDTH)),
                  pl.BlockSpec((POOL_WIDTH, POOL_WIDTH), lambda b, n: (0, 0)),
                  pl.BlockSpec((1, POOL_WIDTH), lambda b, n: (0, 0))],
        out_specs=pl.BlockSpec((None, rows, POOL_WIDTH), lambda b, n: (b, n, 0)),
        out_shape=jax.ShapeDtypeStruct((batch, seq, POOL_WIDTH), BF16),
        scratch_shapes=[pltpu.VMEM((rows + 16, POOL_WIDTH), F32)],
        compiler_params=_params("parallel", "arbitrary"), name="multiscale_pool")(h3, h3, wbd, scale)
    return out.reshape(batch * seq, POOL_WIDTH)


def _proj_ln_body(a, d, p, x, wa, wd, wp, g, b, o):
    y = jnp.dot(a[...], wa[...], preferred_element_type=F32)
    y = y + jnp.dot(d[...], wd[...], preferred_element_type=F32)
    y = y + jnp.dot(p[...], wp[...], preferred_element_type=F32)
    o[...] = _layer_norm_rows(ALPHA * x[...] + y, g[...], b[...])


def _proj_ln(att, dn, pool, x, wa, wd, wp, g, b, tm=512):
    t = x.shape[0]

    def rows(wd_):
        return pl.BlockSpec((tm, wd_), lambda i: (i, 0))

    def full(a_):
        return pl.BlockSpec(a_.shape, lambda i: (0, 0))

    return pl.pallas_call(
        _proj_ln_body, grid=(t // tm,),
        in_specs=[rows(ATT_WIDTH), rows(DN_WIDTH), rows(POOL_WIDTH), rows(D_MODEL),
                  full(wa), full(wd), full(wp), full(g), full(b)],
        out_specs=rows(D_MODEL), out_shape=jax.ShapeDtypeStruct((t, D_MODEL), F32),
        compiler_params=_params("parallel"), name="out_proj_ln")(att, dn, pool, x, wa, wd, wp, g, b)


def _xattn_body(x, wq, k, v, wo, g, b, o):
    xv = x[...]
    q = jnp.dot(xv.astype(BF16), wq[...], preferred_element_type=F32).astype(BF16)
    heads = []
    for h in range(X_HEADS):
        cs = slice(h * X_HEAD_DIM, (h + 1) * X_HEAD_DIM)
        s = lax.dot_general(q[:, cs], k[:, cs], (((1,), (1,)), ((), ())), preferred_element_type=F32)
        m = jnp.max(s, axis=-1, keepdims=True)
        e = jnp.exp(s - m)
        l = jnp.sum(e, axis=-1, keepdims=True)
        oh = jnp.dot(e.astype(BF16), v[:, cs], preferred_element_type=F32) / l
        heads.append(oh.astype(BF16))
    oc = jnp.concatenate(heads, axis=1)
    y = jnp.dot(oc, wo[...], preferred_element_type=F32)
    o[...] = _layer_norm_rows(ALPHA * xv + y, g[...], b[...])


def _cross_attention(x, wq, k, v, wo, g, b, seq, tm=512):
    t = x.shape[0]
    mem_len = k.shape[1]
    per_batch = seq // tm

    def full(a_):
        return pl.BlockSpec(a_.shape, lambda i: (0, 0))

    kv = pl.BlockSpec((None, mem_len, D_MODEL), lambda i: (i // per_batch, 0, 0))
    rows = pl.BlockSpec((tm, D_MODEL), lambda i: (i, 0))
    return pl.pallas_call(
        _xattn_body, grid=(t // tm,),
        in_specs=[rows, full(wq), kv, kv, full(wo), full(g), full(b)],
        out_specs=rows, out_shape=jax.ShapeDtypeStruct((t, D_MODEL), F32),
        compiler_params=_params("parallel"), name="cross_attention_ln")(x, wq, k, v, wo, g, b)


def _ffn_body(x, w1, w3, w2, g, b, o, xb, acc):
    f = pl.program_id(1)

    @pl.when(f == 0)
    def _():
        xb[...] = x[...].astype(BF16)
        acc[...] = jnp.zeros_like(acc)

    xv = xb[...]
    h1 = jnp.dot(xv, w1[...], preferred_element_type=F32)
    h3 = jnp.dot(xv, w3[...], preferred_element_type=F32)
    hh = (h1 * _sigmoid(h1) * h3).astype(BF16)
    acc[...] += jnp.dot(hh, w2[...], preferred_element_type=F32)

    @pl.when(f == pl.num_programs(1) - 1)
    def _():
        o[...] = _layer_norm_rows(ALPHA * x[...] + acc[...], g[...], b[...])


def _ffn_dense(x, w1, w3, w2, g, b, tm=1024, tf=256):
    t = x.shape[0]
    dff = w1.shape[1]
    rows = pl.BlockSpec((tm, D_MODEL), lambda i, f: (i, 0))
    vec = pl.BlockSpec((1, D_MODEL), lambda i, f: (0, 0))
    return pl.pallas_call(
        _ffn_body, grid=(t // tm, dff // tf),
        in_specs=[rows, pl.BlockSpec((D_MODEL, tf), lambda i, f: (0, f)),
                  pl.BlockSpec((D_MODEL, tf), lambda i, f: (0, f)),
                  pl.BlockSpec((tf, D_MODEL), lambda i, f: (f, 0)), vec, vec],
        out_specs=rows, out_shape=jax.ShapeDtypeStruct((t, D_MODEL), F32),
        scratch_shapes=[pltpu.VMEM((tm, D_MODEL), BF16), pltpu.VMEM((tm, D_MODEL), F32)],
        compiler_params=_params("parallel", "arbitrary"), name="ffn_swiglu_ln")(x, w1, w3, w2, g, b)


def _router_body(x, rw, eidx, gates, ranks, counts, carry, *, tm):
    i = pl.program_id(0)

    @pl.when(i == 0)
    def _():
        carry[...] = jnp.zeros_like(carry)

    logits = jnp.dot(x[...], rw[...], preferred_element_type=F32, precision=lax.Precision.HIGHEST)
    lane = lax.broadcasted_iota(I32, (tm, LANES), 1)
    lanef = lane.astype(F32)
    lg = jnp.where(lane < N_EXPERTS, logits, NEG)
    m1 = jnp.max(lg, axis=-1, keepdims=True)
    i1 = jnp.min(jnp.where(lg == m1, lanef, float(LANES)), axis=-1, keepdims=True)
    lg2 = jnp.where(lanef == i1, NEG, lg)
    m2 = jnp.max(lg2, axis=-1, keepdims=True)
    i2 = jnp.min(jnp.where(lg2 == m2, lanef, float(LANES)), axis=-1, keepdims=True)
    e21 = jnp.exp(m2 - m1)
    g1 = 1.0 / (1.0 + e21)
    g2 = e21 * g1
    hot1 = lanef == i1
    hot2 = lanef == i2
    onehot = jnp.where(hot1 | hot2, 1.0, 0.0)
    r = lax.broadcasted_iota(I32, (tm, tm), 0)
    c = lax.broadcasted_iota(I32, (tm, tm), 1)
    before = jnp.dot(jnp.where(r > c, 1.0, 0.0).astype(BF16), onehot.astype(BF16),
                     preferred_element_type=F32) + carry[...]
    rank1 = jnp.sum(jnp.where(hot1, before, 0.0), axis=-1, keepdims=True)
    rank2 = jnp.sum(jnp.where(hot2, before, 0.0), axis=-1, keepdims=True)
    carry[...] = carry[...] + jnp.sum(onehot, axis=0, keepdims=True)
    eidx[...] = jnp.where(lane == 0, i1, jnp.where(lane == 1, i2, 0.0)).astype(I32)
    gates[...] = jnp.where(lane == 0, g1, jnp.where(lane == 1, g2, 0.0))
    ranks[...] = jnp.where(lane == 0, rank1, jnp.where(lane == 1, rank2, 0.0)).astype(I32)
    counts[...] = jnp.broadcast_to(carry[...], counts.shape).astype(I32)


def _router(x, rw, tm=512):
    t = x.shape[0]
    rows = pl.BlockSpec((tm, LANES), lambda i: (i, 0))
    return pl.pallas_call(
        functools.partial(_router_body, tm=tm), grid=(t // tm,),
        in_specs=[pl.BlockSpec((tm, D_MODEL), lambda i: (i, 0)),
                  pl.BlockSpec((D_MODEL, LANES), lambda i: (0, 0))],
        out_specs=[rows, rows, rows, pl.BlockSpec((8, LANES), lambda i: (0, 0))],
        out_shape=[jax.ShapeDtypeStruct((t, LANES), I32), jax.ShapeDtypeStruct((t, LANES), F32),
                   jax.ShapeDtypeStruct((t, LANES), I32), jax.ShapeDtypeStruct((8, LANES), I32)],
        scratch_shapes=[pltpu.VMEM((1, LANES), F32)],
        compiler_params=_params("arbitrary"), name="router_top2")(x, rw)


def _moe_body(be, nused, xs, w1, w3, w2, ys, acc):
    i = pl.program_id(0)
    f = pl.program_id(1)
    last = f == pl.num_programs(1) - 1
    used = i < nused[0]

    @pl.when(used & (f == 0))
    def _():
        acc[...] = jnp.zeros_like(acc)

    @pl.when(used)
    def _():
        xv = xs[...]
        h1 = jnp.dot(xv, w1[...], preferred_element_type=F32)
        h3 = jnp.dot(xv, w3[...], preferred_element_type=F32)
        hh = (h1 * _sigmoid(h1) * h3).astype(BF16)
        acc[...] += jnp.dot(hh, w2[...], preferred_element_type=F32)

    @pl.when(used & last)
    def _():
        ys[...] = acc[...]

    @pl.when(jnp.logical_not(used) & last)
    def _():
        ys[...] = jnp.zeros_like(ys)


def _moe_experts(block_e, n_used, xs, w1, w3, w2, tm, tf=896):
    n_slots = xs.shape[0]
    dex = w1.shape[2]
    nblk = n_slots // tm

    def live(i, nu):
        return jnp.minimum(i, nu[0] - 1)

    grid_spec = pltpu.PrefetchScalarGridSpec(
        num_scalar_prefetch=2, grid=(nblk, dex // tf),
        in_specs=[pl.BlockSpec((tm, D_MODEL), lambda i, f, be, nu: (live(i, nu), 0)),
                  pl.BlockSpec((None, D_MODEL, tf), lambda i, f, be, nu: (be[i], 0, jnp.where(i < nu[0], f, dex // tf - 1))),
                  pl.BlockSpec((None, D_MODEL, tf), lambda i, f, be, nu: (be[i], 0, jnp.where(i < nu[0], f, dex // tf - 1))),
                  pl.BlockSpec((None, tf, D_MODEL), lambda i, f, be, nu: (be[i], jnp.where(i < nu[0], f, dex // tf - 1), 0))],
        out_specs=pl.BlockSpec((tm, D_MODEL), lambda i, f, be, nu: (i, 0)),
        scratch_shapes=[pltpu.VMEM((tm, D_MODEL), F32)])
    return pl.pallas_call(
        _moe_body, grid_spec=grid_spec, out_shape=jax.ShapeDtypeStruct((n_slots, D_MODEL), F32),
        compiler_params=_params("arbitrary", "arbitrary"), name="moe_experts")(block_e, n_used, xs, w1, w3, w2)


def _combine_ln_body(x, ya, yb, gates, g, b, o):
    gt = gates[...]
    y = ya[...] * gt[:, 0:1] + yb[...] * gt[:, 1:2]
    o[...] = _layer_norm_rows(ALPHA * x[...] + y, g[...], b[...])


def _combine_ln(x, ya, yb, gates, g, b, tm=512):
    t = x.shape[0]
    rows = pl.BlockSpec((tm, D_MODEL), lambda i: (i, 0))
    vec = pl.BlockSpec((1, D_MODEL), lambda i: (0, 0))
    return pl.pallas_call(
        _combine_ln_body, grid=(t // tm,),
        in_specs=[rows, rows, rows, pl.BlockSpec((tm, LANES), lambda i: (i, 0)), vec, vec],
        out_specs=rows, out_shape=jax.ShapeDtypeStruct((t, D_MODEL), F32),
        compiler_params=_params("parallel"), name="moe_combine_ln")(x, ya, yb, gates, g, b)


def _moe_layer(x, rw_pad, w1, w3, w2, g, b, tm=1024):
    t = x.shape[0]
    eidx, gates, ranks, counts = _router(x, rw_pad)
    cnt = counts[0, :N_EXPERTS]
    padded = (cnt + tm - 1) // tm * tm
    pad_end = jnp.cumsum(padded)
    pad_start = pad_end - padded
    e2 = eidx[:, :2]
    dest = pad_start[e2] + ranks[:, :2]
    nblk = (2 * t) // tm + N_EXPERTS
    n_slots = nblk * tm
    tok = jnp.broadcast_to(jnp.arange(t, dtype=I32)[:, None], (t, 2))
    slot_tok = jnp.zeros((n_slots,), I32).at[dest.reshape(-1)].set(tok.reshape(-1))
    n_used = (pad_end[-1] // tm).astype(I32).reshape(1)
    blk = jnp.arange(nblk, dtype=I32) * tm
    block_e = jnp.minimum(jnp.sum(pad_end[None, :] <= blk[:, None], axis=1), N_EXPERTS - 1).astype(I32)
    block_e = jnp.where(jnp.arange(nblk) < n_used[0], block_e, block_e[jnp.maximum(n_used[0] - 1, 0)])
    xs = x.astype(BF16)[slot_tok]
    ys = _moe_experts(block_e, n_used, xs, w1, w3, w2, tm)
    return _combine_ln(x, ys[dest[:, 0]], ys[dest[:, 1]], gates, g, b)


def _block_diag(w):
    g, c, _ = w.shape
    eye = jnp.eye(g, dtype=w.dtype)
    return (eye[:, None, :, None] * w[:, :, None, :]).reshape(g * c, g * c)


def kernel(x, mem, positions, w_in, conv_w, a_log, dt_bias, dn_norm_w, pool_w, pool_scale, w_out, ln_mix_g, ln_mix_b, xq_w, xk_w, xv_w, xo_w, ln_x_g, ln_x_b, ffn_w1, ffn_w3, ffn_w2, router_w, moe_w1, moe_w3, moe_w2, ln_ffn_g, ln_ffn_b):
    batch, seq, _ = x.shape
    t = batch * seq
    depth = w_in.shape[0]
    xf = x.reshape(t, D_MODEL)
    memf = mem.reshape(batch * mem.shape[1], D_MODEL)

    posc, posr = {}, {}
    for dil in DILATIONS:
        pt = positions.reshape(batch, seq // dil, dil).transpose(0, 2, 1)
        posc[dil] = pt[..., None]
        posr[dil] = pt[:, :, None, :]

    for l in range(depth):
        wl = w_in[l]
        small = jnp.pad(wl[:, OFF_BETA:OFF_GATE], ((0, 0), (0, LANES - 2 * DN_HEADS)))
        w_rest = jnp.concatenate([wl[:, OFF_DN:OFF_BETA], wl[:, OFF_GATE:OFF_POOL], wl[:, OFF_POOL:], small],
                                 axis=1).astype(BF16)
        slabs = _matmul_slabs(xf, wl[:, :OFF_DN].astype(BF16), 512)
        hr = _matmul(xf, w_rest, F32, 1024, R_COLS // 3)

        branches = [_att_branch(slabs, posc[d], posr[d], d, batch, seq) for d in DILATIONS]
        att = _att_merge([o for o, _ in branches], [s for _, s in branches])

        pvec = jnp.zeros((2, LANES), F32)
        pvec = pvec.at[0, DN_HEADS:2 * DN_HEADS].set(a_log[l]).at[1, DN_HEADS:2 * DN_HEADS].set(dt_bias[l])
        normw = jnp.tile(dn_norm_w[l], DN_HEADS)[None, :]
        dn = _delta_net(hr, conv_w[l], pvec, normw, batch, seq)

        pool = _pool(hr, _block_diag(pool_w[l]).astype(BF16), pool_scale[l][None, :], batch, seq)

        wo = w_out[l].astype(BF16)
        xf = _proj_ln(att, dn, pool, xf, wo[:ATT_WIDTH], wo[ATT_WIDTH:ATT_WIDTH + DN_WIDTH],
                      wo[ATT_WIDTH + DN_WIDTH:], ln_mix_g[l][None, :], ln_mix_b[l][None, :])

        wkv = jnp.concatenate([xk_w[l], xv_w[l]], axis=1).astype(BF16)
        kv = _matmul(memf, wkv, BF16, 512, 1024).reshape(batch, mem.shape[1], 2 * D_MODEL)
        xf = _cross_attention(xf, (xq_w[l] * (X_HEAD_DIM ** -0.5)).astype(BF16), kv[..., :D_MODEL],
                              kv[..., D_MODEL:], xo_w[l].astype(BF16), ln_x_g[l][None, :],
                              ln_x_b[l][None, :], seq)

        j = l // 2
        gl, bl = ln_ffn_g[l][None, :], ln_ffn_b[l][None, :]
        if l % 2 == 0:
            xf = _ffn_dense(xf, ffn_w1[j].astype(BF16), ffn_w3[j].astype(BF16), ffn_w2[j].astype(BF16), gl, bl)
        else:
            rw_pad = jnp.pad(router_w[j], ((0, 0), (0, LANES - N_EXPERTS)))
            xf = _moe_layer(xf, rw_pad, moe_w1[j].astype(BF16), moe_w3[j].astype(BF16),
                            moe_w2[j].astype(BF16), gl, bl)
    return xf.reshape(batch, seq, D_MODEL)
```

```python
import functools
import math

import jax
import jax.numpy as jnp
from jax import lax
from jax.experimental import pallas as pl
from jax.experimental.pallas import tpu as pltpu

F32, BF16, I32 = jnp.float32, jnp.bfloat16, jnp.int32

D_MODEL = 1024
HEAD_DIM = 64
ATT_HEADS = 6
ATT_WIDTH = ATT_HEADS * HEAD_DIM
DN_HEADS = 6
DN_WIDTH = DN_HEADS * HEAD_DIM
POOL_WIDTH = 256
POOL_WINDOWS = (2, 4, 8, 16)
POOL_GDIM = POOL_WIDTH // len(POOL_WINDOWS)
DILATIONS = (1, 4, 16)
CONV_K = 4
X_HEADS = 4
X_HEAD_DIM = D_MODEL // X_HEADS
N_EXPERTS = 8
DEPTH = 4
ALPHA = (2 * DEPTH) ** 0.25
LN_EPS = 1e-5
NORM_EPS = 1e-6

OFF_DN = 3 * ATT_WIDTH
OFF_BETA = OFF_DN + 3 * DN_WIDTH
OFF_DECAY = OFF_BETA + DN_HEADS
OFF_GATE = OFF_DECAY + DN_HEADS
OFF_POOL = OFF_GATE + DN_WIDTH

LANES = 128
R_DN = 0
R_GATE = R_DN + 3 * DN_WIDTH
R_POOL = R_GATE + DN_WIDTH
R_SMALL = R_POOL + POOL_WIDTH
R_COLS = R_SMALL + LANES

QB = 128
NEG = -1e30
VMEM_LIMIT = 48 * 1024 * 1024


def _params(*sem):
    return pltpu.CompilerParams(dimension_semantics=sem, vmem_limit_bytes=VMEM_LIMIT)


def _sigmoid(x):
    return 1.0 / (1.0 + jnp.exp(-x))


def _layer_norm_rows(z, g, b):
    mu = jnp.mean(z, axis=-1, keepdims=True)
    zc = z - mu
    var = jnp.mean(zc * zc, axis=-1, keepdims=True)
    return zc * lax.rsqrt(var + LN_EPS) * g + b


def _rows(start, size, stride):
    return pl.ds(start, size) if stride == 1 else pl.ds(start, size, stride=stride)


def _mm_body(x_ref, w_ref, o_ref):
    o_ref[...] = jnp.dot(x_ref[...].astype(BF16), w_ref[...],
                         preferred_element_type=F32).astype(o_ref.dtype)


def _matmul(x, w, out_dtype, tm, tn):
    m, k = x.shape
    n = w.shape[1]
    return pl.pallas_call(
        _mm_body, grid=(m // tm, n // tn),
        in_specs=[pl.BlockSpec((tm, k), lambda i, j: (i, 0)),
                  pl.BlockSpec((k, tn), lambda i, j: (0, j))],
        out_specs=pl.BlockSpec((tm, tn), lambda i, j: (i, j)),
        out_shape=jax.ShapeDtypeStruct((m, n), out_dtype),
        compiler_params=_params("parallel", "arbitrary"), name="matmul")(x, w)


def _mm_slab_body(x_ref, w_ref, o_ref):
    res = jnp.dot(x_ref[...].astype(BF16), w_ref[...], preferred_element_type=F32)
    for j in range(o_ref.shape[0]):
        o_ref[j] = res[:, j * LANES:(j + 1) * LANES]


def _matmul_slabs(x, w, tm):
    m, k = x.shape
    n = w.shape[1]
    return pl.pallas_call(
        _mm_slab_body, grid=(m // tm,),
        in_specs=[pl.BlockSpec((tm, k), lambda i: (i, 0)), pl.BlockSpec((k, n), lambda i: (0, 0))],
        out_specs=pl.BlockSpec((n // LANES, tm, LANES), lambda i: (0, i, 0)),
        out_shape=jax.ShapeDtypeStruct((n // LANES, m, LANES), F32),
        compiler_params=_params("parallel"), name="matmul_slabs")(x, w)


def _att_body(q_ref, kc_ref, kp_ref, vc_ref, vp_ref, pq_ref, pkc_ref, pkp_ref, o_ref, lse_ref,
              kbuf, vbuf, pkbuf, *, dil, rows, slopes):
    n = pl.program_id(1)
    nsub = rows // QB
    npair = ATT_HEADS // 2
    a = lax.broadcasted_iota(I32, (QB, 2 * QB), 0)
    c = lax.broadcasted_iota(I32, (QB, 2 * QB), 1)
    tri = jnp.where(c < QB, c - a, a - (c - QB)) >= 0
    lane = lax.broadcasted_iota(I32, (QB, LANES), 1)
    low = lane < HEAD_DIM

    def residue(r, carry):
        for p in range(npair):
            kbuf[p, 0:QB, :] = kp_ref[p, _rows(r, QB, dil), :].astype(BF16)
            kbuf[p, QB:, :] = kc_ref[p, _rows(r, rows, dil), :].astype(BF16)
            vbuf[p, 0:QB, :] = vp_ref[p, _rows(r, QB, dil), :].astype(BF16)
            vbuf[p, QB:, :] = vc_ref[p, _rows(r, rows, dil), :].astype(BF16)
        pkbuf[0] = pkp_ref[r]
        for j in range(nsub):
            pkbuf[j + 1] = pkc_ref[r, :, j * QB:(j + 1) * QB]

        def sub(sb, carry2):
            r0 = pl.multiple_of(sb * QB, QB)
            t0 = r + r0 * dil
            pq = pq_ref[r, pl.ds(r0, QB), :]
            pk = jnp.concatenate([pkbuf[sb], pkbuf[sb + 1]], axis=1)
            dist = (pq - pk).astype(F32)
            mask = tri & (c >= jnp.where((sb > 0) | (n > 0), 0, QB))
            heads = range(ATT_HEADS)
            ks = [kbuf[p, pl.ds(r0, 2 * QB), :] for p in range(npair)]
            vs = [vbuf[p, pl.ds(r0, 2 * QB), :] for p in range(npair)]
            qs = [q_ref[p, _rows(t0, QB, dil), :] * (HEAD_DIM ** -0.5) for p in range(npair)]
            qms = [jnp.where(low if h % 2 == 0 else ~low, qs[h // 2], 0.0).astype(BF16) for h in heads]
            ss = [lax.dot_general(qms[h], ks[h // 2], (((1,), (1,)), ((), ())), preferred_element_type=F32)
                  for h in heads]
            ss = [jnp.where(mask, ss[h] - slopes[h] * dist, NEG) for h in heads]
            ms = [jnp.max(ss[h], axis=-1, keepdims=True) for h in heads]
            es = [jnp.exp(ss[h] - ms[h]) for h in heads]
            ls = [jnp.sum(es[h], axis=-1, keepdims=True) for h in heads]
            outs = [jnp.dot(es[h].astype(BF16), vs[h // 2], preferred_element_type=F32) / ls[h] for h in heads]
            lse = jnp.zeros((QB, LANES), F32)
            for h in heads:
                lse = jnp.where(lane == h, ms[h] + jnp.log(ls[h]), lse)
            for p in range(npair):
                o_ref[p, _rows(t0, QB, dil), :] = jnp.where(low, outs[2 * p], outs[2 * p + 1])
            lse_ref[_rows(t0, QB, dil), :] = lse
            return carry2

        lax.fori_loop(0, nsub, sub, 0)
        return carry

    lax.fori_loop(0, dil, residue, 0)


def _att_branch(slabs, posc, posr, dil, batch, seq):
    sub_len = seq // dil
    rows = min(512 if dil == 1 else 2048 // dil, sub_len)
    tok = rows * dil
    nsteps = seq // tok
    prev_tok = QB * dil
    ratio = tok // prev_tok
    npair = ATT_HEADS // 2
    slopes = tuple(2.0 ** (-8.0 * (i + 1) / ATT_HEADS) for i in range(ATT_HEADS))

    def cur(which):
        return pl.BlockSpec((npair, tok, LANES), lambda b, n: (which, b * nsteps + n, 0))

    def prev(which):
        return pl.BlockSpec((npair, prev_tok, LANES),
                            lambda b, n: (which, jnp.maximum((b * nsteps + n) * ratio - 1, 0), 0))

    t = batch * seq
    return pl.pallas_call(
        functools.partial(_att_body, dil=dil, rows=rows, slopes=slopes),
        grid=(batch, nsteps),
        in_specs=[cur(0), cur(1), prev(1), cur(2), prev(2),
                  pl.BlockSpec((None, dil, rows, 1), lambda b, n: (b, 0, n, 0)),
                  pl.BlockSpec((None, dil, 1, rows), lambda b, n: (b, 0, 0, n)),
                  pl.BlockSpec((None, dil, 1, QB),
                               lambda b, n: (b, 0, 0, jnp.maximum(n * (rows // QB) - 1, 0)))],
        out_specs=[pl.BlockSpec((npair, tok, LANES), lambda b, n: (0, b * nsteps + n, 0)),
                   pl.BlockSpec((tok, LANES), lambda b, n: (b * nsteps + n, 0))],
        out_shape=[jax.ShapeDtypeStruct((npair, t, LANES), F32),
                   jax.ShapeDtypeStruct((t, LANES), F32)],
        scratch_shapes=[pltpu.VMEM((npair, rows + QB, LANES), BF16),
                        pltpu.VMEM((npair, rows + QB, LANES), BF16),
                        pltpu.VMEM((rows // QB + 1, 1, QB), I32)],
        compiler_params=_params("parallel", "arbitrary"),
        name=f"dilated_att_d{dil}")(slabs, slabs, slabs, slabs, slabs, posc, posr, posr)


def _merge_body(o1, o2, o3, l1, l2, l3, out):
    ls = [l1[...], l2[...], l3[...]]
    m = jnp.maximum(jnp.maximum(ls[0], ls[1]), ls[2])
    ws = [jnp.exp(l - m) for l in ls]
    inv = 1.0 / (ws[0] + ws[1] + ws[2])
    ws = [w * inv for w in ws]
    lane = lax.broadcasted_iota(I32, (out.shape[0], LANES), 1)
    low = lane < HEAD_DIM
    for p in range(ATT_HEADS // 2):
        acc = None
        for w, o in zip(ws, (o1, o2, o3)):
            wp = jnp.where(low, w[:, 2 * p:2 * p + 1], w[:, 2 * p + 1:2 * p + 2])
            t = wp * o[p]
            acc = t if acc is None else acc + t
        out[:, p * LANES:(p + 1) * LANES] = acc.astype(out.dtype)


def _att_merge(os_, ls_, tm=1024):
    npair, t, _ = os_[0].shape
    ospec = pl.BlockSpec((npair, tm, LANES), lambda i: (0, i, 0))
    lspec = pl.BlockSpec((tm, LANES), lambda i: (i, 0))
    return pl.pallas_call(
        _merge_body, grid=(t // tm,), in_specs=[ospec] * 3 + [lspec] * 3,
        out_specs=pl.BlockSpec((tm, ATT_WIDTH), lambda i: (i, 0)),
        out_shape=jax.ShapeDtypeStruct((t, ATT_WIDTH), BF16),
        compiler_params=_params("parallel"), name="att_merge")(*os_, *ls_)


def _split2(a):
    hi = a.astype(BF16)
    lo = (a - hi.astype(F32)).astype(BF16)
    return hi, lo


def _dot(a, b):
    return jnp.dot(a, b, preferred_element_type=F32)


def _dot_bf16(a, b):
    return _dot(a.astype(BF16), b.astype(BF16))


def _dot_hp(a, b):
    ah, al = _split2(a)
    bh, bl = _split2(b)
    return _dot(jnp.concatenate([ah, al, ah], axis=1), jnp.concatenate([bh, bh, bl], axis=0))


def _dot_t(a, b):
    return lax.dot_general(a, b, (((1,), (1,)), ((), ())), preferred_element_type=F32)


def _dot_t_hp(a, b):
    ah, al = _split2(a)
    bh, bl = _split2(b)
    return _dot_t(jnp.concatenate([ah, al, ah], axis=1), jnp.concatenate([bh, bh, bl], axis=1))


def _dn_body(hq, hk, hv, pq, pk, pv, small, gate, convw, pvec, normw, out, xbuf, qkv, s_ref, *, nch):
    n = pl.program_id(1)
    c_ = QB
    npair = DN_HEADS // 2

    @pl.when(n == 0)
    def _():
        s_ref[...] = jnp.zeros_like(s_ref)

    halo = 8
    has_prev = jnp.where(n > 0, 1.0, 0.0)
    for idx, (cur, prev) in enumerate(((hq, pq), (hk, pk), (hv, pv))):
        xbuf[idx, 0:halo, :] = prev[...] * has_prev
        xbuf[idx, halo:, :] = cur[...]
        acc = None
        for j in range(CONV_K):
            start = halo - (CONV_K - 1) + j
            t = xbuf[idx, start:start + nch * c_, :] * convw[j:j + 1, idx * DN_WIDTH:(idx + 1) * DN_WIDTH]
            acc = t if acc is None else acc + t
        qkv[idx] = acc * _sigmoid(acc)

    lane = lax.broadcasted_iota(I32, (c_, LANES), 1)
    low = lane < HEAD_DIM
    row = lax.broadcasted_iota(I32, (c_, c_), 0)
    col = lax.broadcasted_iota(I32, (c_, c_), 1)
    lower = row >= col
    strict = row > col
    eye = jnp.where(row == col, 1.0, 0.0)
    blockdiag = (row < HEAD_DIM) == (col < HEAD_DIM)

    def headsum(x):
        sa = jnp.sum(jnp.where(low, x, 0.0), axis=-1, keepdims=True)
        sb = jnp.sum(jnp.where(low, 0.0, x), axis=-1, keepdims=True)
        return jnp.where(low, sa, sb)

    def col_of(arr, j):
        return arr[:, j:j + 1]

    ltri = jnp.where(lower, 1.0, 0.0).astype(BF16)
    ltri3 = jnp.concatenate([ltri, ltri, ltri], axis=1)
    pair, xs, pws, qks = [], [], [], []
    for ch in range(nch):
        rs = slice(ch * c_, (ch + 1) * c_)
        sm = small[rs, :]
        beta_all = _sigmoid(sm)
        z = sm + pvec[1:2, :]
        softplus = jnp.maximum(z, 0.0) + jnp.log(1.0 + jnp.exp(-jnp.abs(z)))
        g_all = -jnp.exp(pvec[0:1, :]) * softplus
        g1 = g_all.astype(BF16)
        rem = g_all - g1.astype(F32)
        g2 = rem.astype(BF16)
        g3 = (rem - g2.astype(F32)).astype(BF16)
        gc_all = _dot(ltri3, jnp.concatenate([g1, g2, g3], axis=0))
        gc_t = gc_all.T

        for p in range(npair):
            cs = slice(p * LANES, (p + 1) * LANES)
            q, k, v = qkv[0, rs, cs], qkv[1, rs, cs], qkv[2, rs, cs]
            q = q * lax.rsqrt(headsum(q * q) + NORM_EPS) * (HEAD_DIM ** -0.5)
            k = k * lax.rsqrt(headsum(k * k) + NORM_EPS)
            beta = jnp.where(low, col_of(beta_all, 2 * p), col_of(beta_all, 2 * p + 1))
            gc = jnp.where(low, col_of(gc_all, DN_HEADS + 2 * p), col_of(gc_all, DN_HEADS + 2 * p + 1))
            eg = jnp.exp(gc)
            rhs = jnp.concatenate([v * beta, k * beta * eg], axis=1)
            pair.append(dict(q=q, k=k, kb=k.astype(BF16), gc=gc, eg=eg, rhs=rhs))

        for h in range(DN_HEADS):
            pr = pair[ch * npair + h // 2]
            sel = low if h % 2 == 0 else ~low
            gcol = col_of(gc_all, DN_HEADS + h)
            grow = gc_t[DN_HEADS + h:DN_HEADS + h + 1, :]
            dec = jnp.where(lower, jnp.exp(jnp.minimum(gcol - grow, 0.0)), 0.0)
            kk = _dot_t(jnp.where(sel, pr["k"], 0.0).astype(BF16), pr["kb"])
            qk = _dot_t_hp(jnp.where(sel, pr["q"], 0.0), pr["k"])
            nm = jnp.where(strict, col_of(beta_all, h) * kk * dec, 0.0)
            xs.append(eye - nm)
            pws.append(nm)
            qks.append((qk * dec).astype(BF16))

    for level in range(int(math.log2(c_)) - 1):
        mm = _dot_hp if level < 3 else _dot_bf16
        for i in range(len(xs)):
            pws[i] = mm(pws[i], pws[i])
        for i in range(len(xs)):
            xs[i] = xs[i] + mm(xs[i], pws[i])
    uws = [_dot_hp(xs[i], pair[i // 2]["rhs"]) for i in range(len(xs))]

    for ch in range(nch):
        rs = slice(ch * c_, (ch + 1) * c_)
        for p in range(npair):
            pr = pair[ch * npair + p]
            i0 = ch * DN_HEADS + 2 * p
            cs = slice(p * LANES, (p + 1) * LANES)
            u = jnp.where(low, uws[i0][:, :LANES], uws[i0 + 1][:, :LANES])
            w = jnp.where(low, uws[i0][:, LANES:], uws[i0 + 1][:, LANES:])
            state = s_ref[p]
            sb16 = state.astype(BF16)
            vnew = u - _dot(w.astype(BF16), sb16)
            vn16 = vnew.astype(BF16)
            o = _dot((pr["q"] * pr["eg"]).astype(BF16), sb16)
            o = o + jnp.where(low, _dot(qks[i0], vn16), _dot(qks[i0 + 1], vn16))
            glast = pr["gc"][c_ - 1:c_, :]
            kd = pr["k"] * jnp.exp(glast - pr["gc"])
            upd = _dot(kd.T.astype(BF16), vn16)
            s_ref[p] = state * jnp.exp(glast) + jnp.where(blockdiag, upd, 0.0)
            ms = headsum(o * o) * (1.0 / HEAD_DIM)
            gt = gate[rs, cs]
            out[rs, cs] = (o * lax.rsqrt(ms + NORM_EPS) * normw[:, cs] * (gt * _sigmoid(gt))).astype(out.dtype)


def _delta_net(hr, convw, pvec, normw, batch, seq, nch=2):
    h3 = hr.reshape(batch, seq, R_COLS)
    rows = nch * QB
    qcol = R_DN // DN_WIDTH

    def cur(off):
        return pl.BlockSpec((None, rows, DN_WIDTH), lambda b, n: (b, n, qcol + off))

    def prev(off):
        return pl.BlockSpec((None, 8, DN_WIDTH),
                            lambda b, n: (b, jnp.maximum(n * (rows // 8) - 1, 0), qcol + off))

    out = pl.pallas_call(
        functools.partial(_dn_body, nch=nch), grid=(batch, seq // rows),
        in_specs=[cur(0), cur(1), cur(2), prev(0), prev(1), prev(2),
                  pl.BlockSpec((None, rows, LANES), lambda b, n: (b, n, R_SMALL // LANES)),
                  pl.BlockSpec((None, rows, DN_WIDTH), lambda b, n: (b, n, R_GATE // DN_WIDTH)),
                  pl.BlockSpec((CONV_K, 3 * DN_WIDTH), lambda b, n: (0, 0)),
                  pl.BlockSpec((2, LANES), lambda b, n: (0, 0)),
                  pl.BlockSpec((1, DN_WIDTH), lambda b, n: (0, 0))],
        out_specs=pl.BlockSpec((None, rows, DN_WIDTH), lambda b, n: (b, n, 0)),
        out_shape=jax.ShapeDtypeStruct((batch, seq, DN_WIDTH), BF16),
        scratch_shapes=[pltpu.VMEM((3, rows + 8, DN_WIDTH), F32),
                        pltpu.VMEM((3, rows, DN_WIDTH), F32),
                        pltpu.VMEM((DN_HEADS // 2, LANES, LANES), F32)],
        compiler_params=_params("parallel", "arbitrary"),
        name="gated_delta_rule")(h3, h3, h3, h3, h3, h3, h3, h3, convw, pvec, normw)
    return out.reshape(batch * seq, DN_WIDTH)


def _pool_body(cur, prev, wbd, scale, out, xbuf, *, rows):
    n = pl.program_id(1)
    halo = 16
    xbuf[0:halo, :] = prev[...] * jnp.where(n > 0, 1.0, 0.0)
    xbuf[halo:, :] = cur[...]
    u = cur[...]
    lane = lax.broadcasted_iota(I32, (rows, POOL_WIDTH), 1)
    tpos = n * rows + lax.broadcasted_iota(I32, (rows, POOL_WIDTH), 0)
    run = u
    pooled = jnp.zeros_like(u)
    for j in range(1, max(POOL_WINDOWS)):
        run = run + xbuf[halo - j:halo - j + rows, :]
        if j + 1 in POOL_WINDOWS:
            gi = POOL_WINDOWS.index(j + 1)
            cnt = jnp.minimum(tpos + 1, j + 1).astype(F32)
            sel = (lane >= gi * POOL_GDIM) & (lane < (gi + 1) * POOL_GDIM)
            pooled = jnp.where(sel, run / cnt - u, pooled)
    mixed = jnp.dot(pooled.astype(BF16), wbd[...], preferred_element_type=F32)
    out[...] = (mixed * scale[...]).astype(out.dtype)


def _pool(hr, wbd, scale, batch, seq, rows=512):
    h3 = hr.reshape(batch, seq, R_COLS)
    out = pl.pallas_call(
        functools.partial(_pool_body, rows=rows), grid=(batch, seq // rows),
        in_specs=[pl.BlockSpec((None, rows, POOL_WIDTH), lambda b, n: (b, n, R_POOL // POOL_WIDTH)),
                  pl.BlockSpec((None, 16, POOL_WIDTH),
                               lambda b, n: (b, jnp.maximum(n * (rows // 16) - 1, 0), R_POOL // POOL_WIDTH)),
                  pl.BlockSpec((POOL_WIDTH, POOL_WIDTH), lambda b, n: (0, 0)),
                  pl.BlockSpec((1, POOL_WIDTH), lambda b, n: (0, 0))],
        out_specs=pl.BlockSpec((None, rows, POOL_WIDTH), lambda b, n: (b, n, 0)),
        out_shape=jax.ShapeDtypeStruct((batch, seq, POOL_WIDTH), BF16),
        scratch_shapes=[pltpu.VMEM((rows + 16, POOL_WIDTH), F32)],
        compiler_params=_params("parallel", "arbitrary"), name="multiscale_pool")(h3, h3, wbd, scale)
    return out.reshape(batch * seq, POOL_WIDTH)


def _proj_ln_body(a, d, p, x, wa, wd, wp, g, b, o):
    y = jnp.dot(a[...], wa[...], preferred_element_type=F32)
    y = y + jnp.dot(d[...], wd[...], preferred_element_type=F32)
    y = y + jnp.dot(p[...], wp[...], preferred_element_type=F32)
    o[...] = _layer_norm_rows(ALPHA * x[...] + y, g[...], b[...])


def _proj_ln(att, dn, pool, x, wa, wd, wp, g, b, tm=512):
    t = x.shape[0]

    def rows(wd_):
        return pl.BlockSpec((tm, wd_), lambda i: (i, 0))

    def full(a_):
        return pl.BlockSpec(a_.shape, lambda i: (0, 0))

    return pl.pallas_call(
        _proj_ln_body, grid=(t // tm,),
        in_specs=[rows(ATT_WIDTH), rows(DN_WIDTH), rows(POOL_WIDTH), rows(D_MODEL),
                  full(wa), full(wd), full(wp), full(g), full(b)],
        out_specs=rows(D_MODEL), out_shape=jax.ShapeDtypeStruct((t, D_MODEL), F32),
        compiler_params=_params("parallel"), name="out_proj_ln")(att, dn, pool, x, wa, wd, wp, g, b)


def _xattn_body(x, wq, k, v, wo, g, b, o):
    xv = x[...]
    q = jnp.dot(xv.astype(BF16), wq[...], preferred_element_type=F32).astype(BF16)
    heads = []
    for h in range(X_HEADS):
        cs = slice(h * X_HEAD_DIM, (h + 1) * X_HEAD_DIM)
        s = lax.dot_general(q[:, cs], k[:, cs], (((1,), (1,)), ((), ())), preferred_element_type=F32)
        m = jnp.max(s, axis=-1, keepdims=True)
        e = jnp.exp(s - m)
        l = jnp.sum(e, axis=-1, keepdims=True)
        oh = jnp.dot(e.astype(BF16), v[:, cs], preferred_element_type=F32) / l
        heads.append(oh.astype(BF16))
    oc = jnp.concatenate(heads, axis=1)
    y = jnp.dot(oc, wo[...], preferred_element_type=F32)
    o[...] = _layer_norm_rows(ALPHA * xv + y, g[...], b[...])


def _cross_attention(x, wq, k, v, wo, g, b, seq, tm=512):
    t = x.shape[0]
    mem_len = k.shape[1]
    per_batch = seq // tm

    def full(a_):
        return pl.BlockSpec(a_.shape, lambda i: (0, 0))

    kv = pl.BlockSpec((None, mem_len, D_MODEL), lambda i: (i // per_batch, 0, 0))
    rows = pl.BlockSpec((tm, D_MODEL), lambda i: (i, 0))
    return pl.pallas_call(
        _xattn_body, grid=(t // tm,),
        in_specs=[rows, full(wq), kv, kv, full(wo), full(g), full(b)],
        out_specs=rows, out_shape=jax.ShapeDtypeStruct((t, D_MODEL), F32),
        compiler_params=_params("parallel"), name="cross_attention_ln")(x, wq, k, v, wo, g, b)


def _ffn_body(x, w1, w3, w2, g, b, o, xb, acc):
    f = pl.program_id(1)

    @pl.when(f == 0)
    def _():
        xb[...] = x[...].astype(BF16)
        acc[...] = jnp.zeros_like(acc)

    xv = xb[...]
    h1 = jnp.dot(xv, w1[...], preferred_element_type=F32)
    h3 = jnp.dot(xv, w3[...], preferred_element_type=F32)
    hh = (h1 * _sigmoid(h1) * h3).astype(BF16)
    acc[...] += jnp.dot(hh, w2[...], preferred_element_type=F32)

    @pl.when(f == pl.num_programs(1) - 1)
    def _():
        o[...] = _layer_norm_rows(ALPHA * x[...] + acc[...], g[...], b[...])


def _ffn_dense(x, w1, w3, w2, g, b, tm=1024, tf=256):
    t = x.shape[0]
    dff = w1.shape[1]
    rows = pl.BlockSpec((tm, D_MODEL), lambda i, f: (i, 0))
    vec = pl.BlockSpec((1, D_MODEL), lambda i, f: (0, 0))
    return pl.pallas_call(
        _ffn_body, grid=(t // tm, dff // tf),
        in_specs=[rows, pl.BlockSpec((D_MODEL, tf), lambda i, f: (0, f)),
                  pl.BlockSpec((D_MODEL, tf), lambda i, f: (0, f)),
                  pl.BlockSpec((tf, D_MODEL), lambda i, f: (f, 0)), vec, vec],
        out_specs=rows, out_shape=jax.ShapeDtypeStruct((t, D_MODEL), F32),
        scratch_shapes=[pltpu.VMEM((tm, D_MODEL), BF16), pltpu.VMEM((tm, D_MODEL), F32)],
        compiler_params=_params("parallel", "arbitrary"), name="ffn_swiglu_ln")(x, w1, w3, w2, g, b)


def _router_body(x, rw, eidx, gates, ranks, counts, carry, *, tm):
    i = pl.program_id(0)

    @pl.when(i == 0)
    def _():
        carry[...] = jnp.zeros_like(carry)

    logits = jnp.dot(x[...], rw[...], preferred_element_type=F32, precision=lax.Precision.HIGHEST)
    lane = lax.broadcasted_iota(I32, (tm, LANES), 1)
    lanef = lane.astype(F32)
    lg = jnp.where(lane < N_EXPERTS, logits, NEG)
    m1 = jnp.max(lg, axis=-1, keepdims=True)
    i1 = jnp.min(jnp.where(lg == m1, lanef, float(LANES)), axis=-1, keepdims=True)
    lg2 = jnp.where(lanef == i1, NEG, lg)
    m2 = jnp.max(lg2, axis=-1, keepdims=True)
    i2 = jnp.min(jnp.where(lg2 == m2, lanef, float(LANES)), axis=-1, keepdims=True)
    e21 = jnp.exp(m2 - m1)
    g1 = 1.0 / (1.0 + e21)
    g2 = e21 * g1
    hot1 = lanef == i1
    hot2 = lanef == i2
    onehot = jnp.where(hot1 | hot2, 1.0, 0.0)
    r = lax.broadcasted_iota(I32, (tm, tm), 0)
    c = lax.broadcasted_iota(I32, (tm, tm), 1)
    before = jnp.dot(jnp.where(r > c, 1.0, 0.0).astype(BF16), onehot.astype(BF16),
                     preferred_element_type=F32) + carry[...]
    rank1 = jnp.sum(jnp.where(hot1, before, 0.0), axis=-1, keepdims=True)
    rank2 = jnp.sum(jnp.where(hot2, before, 0.0), axis=-1, keepdims=True)
    carry[...] = carry[...] + jnp.sum(onehot, axis=0, keepdims=True)
    eidx[...] = jnp.where(lane == 0, i1, jnp.where(lane == 1, i2, 0.0)).astype(I32)
    gates[...] = jnp.where(lane == 0, g1, jnp.where(lane == 1, g2, 0.0))
    ranks[...] = jnp.where(lane == 0, rank1, jnp.where(lane == 1, rank2, 0.0)).astype(I32)
    counts[...] = jnp.broadcast_to(carry[...], counts.shape).astype(I32)


def _router(x, rw, tm=512):
    t = x.shape[0]
    rows = pl.BlockSpec((tm, LANES), lambda i: (i, 0))
    return pl.pallas_call(
        functools.partial(_router_body, tm=tm), grid=(t // tm,),
        in_specs=[pl.BlockSpec((tm, D_MODEL), lambda i: (i, 0)),
                  pl.BlockSpec((D_MODEL, LANES), lambda i: (0, 0))],
        out_specs=[rows, rows, rows, pl.BlockSpec((8, LANES), lambda i: (0, 0))],
        out_shape=[jax.ShapeDtypeStruct((t, LANES), I32), jax.ShapeDtypeStruct((t, LANES), F32),
                   jax.ShapeDtypeStruct((t, LANES), I32), jax.ShapeDtypeStruct((8, LANES), I32)],
        scratch_shapes=[pltpu.VMEM((1, LANES), F32)],
        compiler_params=_params("arbitrary"), name="router_top2")(x, rw)


def _moe_body(be, nused, xs, w1, w3, w2, ys, acc):
    i = pl.program_id(0)
    f = pl.program_id(1)
    last = f == pl.num_programs(1) - 1
    used = i < nused[0]

    @pl.when(used & (f == 0))
    def _():
        acc[...] = jnp.zeros_like(acc)

    @pl.when(used)
    def _():
        xv = xs[...]
        h1 = jnp.dot(xv, w1[...], preferred_element_type=F32)
        h3 = jnp.dot(xv, w3[...], preferred_element_type=F32)
        hh = (h1 * _sigmoid(h1) * h3).astype(BF16)
        acc[...] += jnp.dot(hh, w2[...], preferred_element_type=F32)

    @pl.when(used & last)
    def _():
        ys[...] = acc[...]

    @pl.when(jnp.logical_not(used) & last)
    def _():
        ys[...] = jnp.zeros_like(ys)


def _moe_experts(block_e, n_used, xs, w1, w3, w2, tm, tf=896):
    n_slots = xs.shape[0]
    dex = w1.shape[2]
    nblk = n_slots // tm

    def live(i, nu):
        return jnp.minimum(i, nu[0] - 1)

    grid_spec = pltpu.PrefetchScalarGridSpec(
        num_scalar_prefetch=2, grid=(nblk, dex // tf),
        in_specs=[pl.BlockSpec((tm, D_MODEL), lambda i, f, be, nu: (live(i, nu), 0)),
                  pl.BlockSpec((None, D_MODEL, tf), lambda i, f, be, nu: (be[i], 0, jnp.where(i < nu[0], f, dex // tf - 1))),
                  pl.BlockSpec((None, D_MODEL, tf), lambda i, f, be, nu: (be[i], 0, jnp.where(i < nu[0], f, dex // tf - 1))),
                  pl.BlockSpec((None, tf, D_MODEL), lambda i, f, be, nu: (be[i], jnp.where(i < nu[0], f, dex // tf - 1), 0))],
        out_specs=pl.BlockSpec((tm, D_MODEL), lambda i, f, be, nu: (i, 0)),
        scratch_shapes=[pltpu.VMEM((tm, D_MODEL), F32)])
    return pl.pallas_call(
        _moe_body, grid_spec=grid_spec, out_shape=jax.ShapeDtypeStruct((n_slots, D_MODEL), F32),
        compiler_params=_params("arbitrary", "arbitrary"), name="moe_experts")(block_e, n_used, xs, w1, w3, w2)


GATHER_UNROLL = 8


def _row_gather_copy(src, idx, buf, sem, slot, r):
    return pltpu.make_async_copy(src.at[pl.ds(idx, 1), :], buf.at[slot, pl.ds(r, 1), :], sem.at[slot])


def _start_row_gather(src, ids, buf, sem, slot, nrows):
    def issue(j, carry):
        for u in range(GATHER_UNROLL):
            r = j * GATHER_UNROLL + u
            _row_gather_copy(src, ids[0, r], buf, sem, slot, r).start()
        return carry

    lax.fori_loop(0, nrows // GATHER_UNROLL, issue, 0)


def _wait_row_gather(src, buf, sem, slot, nrows):
    pltpu.make_async_copy(src.at[pl.ds(0, nrows), :], buf.at[slot], sem.at[slot]).wait()


def _pipelined_row_gather(src, ids_cur, ids_next, buf, sem, nrows):
    i = pl.program_id(0)
    slot = lax.rem(i, 2)

    @pl.when(i == 0)
    def _():
        _start_row_gather(src, ids_cur, buf, sem, 0, nrows)

    @pl.when(i + 1 < pl.num_programs(0))
    def _():
        _start_row_gather(src, ids_next, buf, sem, 1 - slot, nrows)

    _wait_row_gather(src, buf, sem, slot, nrows)
    return slot


def _dispatch_body(ids_cur, ids_next, src, o, buf, sem, *, tm):
    slot = _pipelined_row_gather(src, ids_cur, ids_next, buf, sem, tm)
    o[...] = buf[slot].astype(o.dtype)


def _moe_dispatch(x, slot_tok, tm=512):
    n_slots = slot_tok.shape[0]
    nblk = n_slots // tm
    ids = slot_tok.reshape(nblk, 1, tm)
    smem = functools.partial(pl.BlockSpec, (None, 1, tm), memory_space=pltpu.SMEM)
    return pl.pallas_call(
        functools.partial(_dispatch_body, tm=tm), grid=(nblk,),
        in_specs=[smem(lambda i: (i, 0, 0)), smem(lambda i: (jnp.minimum(i + 1, nblk - 1), 0, 0)),
                  pl.BlockSpec(memory_space=pl.ANY)],
        out_specs=pl.BlockSpec((tm, D_MODEL), lambda i: (i, 0)),
        out_shape=jax.ShapeDtypeStruct((n_slots, D_MODEL), BF16),
        scratch_shapes=[pltpu.VMEM((2, tm, D_MODEL), F32), pltpu.SemaphoreType.DMA((2,))],
        compiler_params=_params("arbitrary"), name="moe_dispatch_gather")(ids, ids, x)


def _combine_ln_body(ids_cur, ids_next, x, ys, gates, g, b, o, buf, sem, *, tm):
    slot = _pipelined_row_gather(ys, ids_cur, ids_next, buf, sem, 2 * tm)
    gt = gates[...]
    y = buf[slot, 0:tm, :] * gt[:, 0:1] + buf[slot, tm:, :] * gt[:, 1:2]
    o[...] = _layer_norm_rows(ALPHA * x[...] + y, g[...], b[...])


def _combine_ln(x, ys, dest, gates, g, b, tm=512):
    t = x.shape[0]
    nt = t // tm
    ids = dest.reshape(nt, tm, 2).transpose(0, 2, 1).reshape(nt, 1, 2 * tm)
    smem = functools.partial(pl.BlockSpec, (None, 1, 2 * tm), memory_space=pltpu.SMEM)
    rows = pl.BlockSpec((tm, D_MODEL), lambda i: (i, 0))
    vec = pl.BlockSpec((1, D_MODEL), lambda i: (0, 0))
    return pl.pallas_call(
        functools.partial(_combine_ln_body, tm=tm), grid=(nt,),
        in_specs=[smem(lambda i: (i, 0, 0)), smem(lambda i: (jnp.minimum(i + 1, nt - 1), 0, 0)),
                  rows, pl.BlockSpec(memory_space=pl.ANY),
                  pl.BlockSpec((tm, LANES), lambda i: (i, 0)), vec, vec],
        out_specs=rows, out_shape=jax.ShapeDtypeStruct((t, D_MODEL), F32),
        scratch_shapes=[pltpu.VMEM((2, 2 * tm, D_MODEL), F32), pltpu.SemaphoreType.DMA((2,))],
        compiler_params=_params("arbitrary"), name="moe_combine_ln")(ids, ids, x, ys, gates, g, b)


def _moe_layer(x, rw_pad, w1, w3, w2, g, b, tm=512):
    t = x.shape[0]
    eidx, gates, ranks, counts = _router(x, rw_pad)
    cnt = counts[0, :N_EXPERTS]
    padded = (cnt + tm - 1) // tm * tm
    pad_end = jnp.cumsum(padded)
    pad_start = pad_end - padded
    e2 = eidx[:, :2]
    dest = pad_start[e2] + ranks[:, :2]
    nblk = (2 * t) // tm + N_EXPERTS
    n_slots = nblk * tm
    tok = jnp.broadcast_to(jnp.arange(t, dtype=I32)[:, None], (t, 2))
    slot_tok = jnp.zeros((n_slots,), I32).at[dest.reshape(-1)].set(tok.reshape(-1))
    n_used = (pad_end[-1] // tm).astype(I32).reshape(1)
    blk = jnp.arange(nblk, dtype=I32) * tm
    block_e = jnp.minimum(jnp.sum(pad_end[None, :] <= blk[:, None], axis=1), N_EXPERTS - 1).astype(I32)
    block_e = jnp.where(jnp.arange(nblk) < n_used[0], block_e, block_e[jnp.maximum(n_used[0] - 1, 0)])
    xs = _moe_dispatch(x, slot_tok)
    ys = _moe_experts(block_e, n_used, xs, w1, w3, w2, tm)
    return _combine_ln(x, ys, dest, gates, g, b)


def _block_diag(w):
    g, c, _ = w.shape
    eye = jnp.eye(g, dtype=w.dtype)
    return (eye[:, None, :, None] * w[:, :, None, :]).reshape(g * c, g * c)


def kernel(x, mem, positions, w_in, conv_w, a_log, dt_bias, dn_norm_w, pool_w, pool_scale, w_out, ln_mix_g, ln_mix_b, xq_w, xk_w, xv_w, xo_w, ln_x_g, ln_x_b, ffn_w1, ffn_w3, ffn_w2, router_w, moe_w1, moe_w3, moe_w2, ln_ffn_g, ln_ffn_b):
    batch, seq, _ = x.shape
    t = batch * seq
    depth = w_in.shape[0]
    xf = x.reshape(t, D_MODEL)
    memf = mem.reshape(batch * mem.shape[1], D_MODEL)

    posc, posr = {}, {}
    for dil in DILATIONS:
        pt = positions.reshape(batch, seq // dil, dil).transpose(0, 2, 1)
        posc[dil] = pt[..., None]
        posr[dil] = pt[:, :, None, :]

    for l in range(depth):
        wl = w_in[l]
        small = jnp.pad(wl[:, OFF_BETA:OFF_GATE], ((0, 0), (0, LANES - 2 * DN_HEADS)))
        w_rest = jnp.concatenate([wl[:, OFF_DN:OFF_BETA], wl[:, OFF_GATE:OFF_POOL], wl[:, OFF_POOL:], small],
                                 axis=1).astype(BF16)
        slabs = _matmul_slabs(xf, wl[:, :OFF_DN].astype(BF16), 512)
        hr = _matmul(xf, w_rest, F32, 1024, R_COLS // 3)

        branches = [_att_branch(slabs, posc[d], posr[d], d, batch, seq) for d in DILATIONS]
        att = _att_merge([o for o, _ in branches], [s for _, s in branches])

        pvec = jnp.zeros((2, LANES), F32)
        pvec = pvec.at[0, DN_HEADS:2 * DN_HEADS].set(a_log[l]).at[1, DN_HEADS:2 * DN_HEADS].set(dt_bias[l])
        normw = jnp.tile(dn_norm_w[l], DN_HEADS)[None, :]
        dn = _delta_net(hr, conv_w[l], pvec, normw, batch, seq)

        pool = _pool(hr, _block_diag(pool_w[l]).astype(BF16), pool_scale[l][None, :], batch, seq)

        wo = w_out[l].astype(BF16)
        xf = _proj_ln(att, dn, pool, xf, wo[:ATT_WIDTH], wo[ATT_WIDTH:ATT_WIDTH + DN_WIDTH],
                      wo[ATT_WIDTH + DN_WIDTH:], ln_mix_g[l][None, :], ln_mix_b[l][None, :])

        wkv = jnp.concatenate([xk_w[l], xv_w[l]], axis=1).astype(BF16)
        kv = _matmul(memf, wkv, BF16, 512, 1024).reshape(batch, mem.shape[1], 2 * D_MODEL)
        xf = _cross_attention(xf, (xq_w[l] * (X_HEAD_DIM ** -0.5)).astype(BF16), kv[..., :D_MODEL],
                              kv[..., D_MODEL:], xo_w[l].astype(BF16), ln_x_g[l][None, :],
                              ln_x_b[l][None, :], seq)

        j = l // 2
        gl, bl = ln_ffn_g[l][None, :], ln_ffn_b[l][None, :]
        if l % 2 == 0:
            xf = _ffn_dense(xf, ffn_w1[j].astype(BF16), ffn_w3[j].astype(BF16), ffn_w2[j].astype(BF16), gl, bl)
        else:
            rw_pad = jnp.pad(router_w[j], ((0, 0), (0, LANES - N_EXPERTS)))
            xf = _moe_layer(xf, rw_pad, moe_w1[j].astype(BF16), moe_w3[j].astype(BF16),
                            moe_w2[j].astype(BF16), gl, bl)
    return xf.reshape(batch, seq, D_MODEL)
```

```python
import functools
import math

import jax
import jax.numpy as jnp
from jax import lax
from jax.experimental import pallas as pl
from jax.experimental.pallas import tpu as pltpu

F32, BF16, I32 = jnp.float32, jnp.bfloat16, jnp.int32

D_MODEL = 1024
HEAD_DIM = 64
ATT_HEADS = 6
ATT_WIDTH = ATT_HEADS * HEAD_DIM
DN_HEADS = 6
DN_WIDTH = DN_HEADS * HEAD_DIM
POOL_WIDTH = 256
POOL_WINDOWS = (2, 4, 8, 16)
POOL_GDIM = POOL_WIDTH // len(POOL_WINDOWS)
DILATIONS = (1, 4, 16)
CONV_K = 4
X_HEADS = 4
X_HEAD_DIM = D_MODEL // X_HEADS
N_EXPERTS = 8
DEPTH = 4
ALPHA = (2 * DEPTH) ** 0.25
LN_EPS = 1e-5
NORM_EPS = 1e-6

OFF_DN = 3 * ATT_WIDTH
OFF_BETA = OFF_DN + 3 * DN_WIDTH
OFF_DECAY = OFF_BETA + DN_HEADS
OFF_GATE = OFF_DECAY + DN_HEADS
OFF_POOL = OFF_GATE + DN_WIDTH

LANES = 128
R_DN = 0
R_GATE = R_DN + 3 * DN_WIDTH
R_POOL = R_GATE + DN_WIDTH
R_SMALL = R_POOL + POOL_WIDTH
R_COLS = R_SMALL + LANES

QB = 128
NEG = -1e30
VMEM_LIMIT = 48 * 1024 * 1024


def _params(*sem):
    return pltpu.CompilerParams(dimension_semantics=sem, vmem_limit_bytes=VMEM_LIMIT)


def _sigmoid(x):
    return 1.0 / (1.0 + jnp.exp(-x))


def _layer_norm_rows(z, g, b):
    mu = jnp.mean(z, axis=-1, keepdims=True)
    zc = z - mu
    var = jnp.mean(zc * zc, axis=-1, keepdims=True)
    return zc * lax.rsqrt(var + LN_EPS) * g + b


def _rows(start, size, stride):
    return pl.ds(start, size) if stride == 1 else pl.ds(start, size, stride=stride)


def _mm_body(x_ref, w_ref, o_ref):
    o_ref[...] = jnp.dot(x_ref[...].astype(BF16), w_ref[...],
                         preferred_element_type=F32).astype(o_ref.dtype)


def _matmul(x, w, out_dtype, tm, tn):
    m, k = x.shape
    n = w.shape[1]
    return pl.pallas_call(
        _mm_body, grid=(m // tm, n // tn),
        in_specs=[pl.BlockSpec((tm, k), lambda i, j: (i, 0)),
                  pl.BlockSpec((k, tn), lambda i, j: (0, j))],
        out_specs=pl.BlockSpec((tm, tn), lambda i, j: (i, j)),
        out_shape=jax.ShapeDtypeStruct((m, n), out_dtype),
        compiler_params=_params("parallel", "arbitrary"), name="matmul")(x, w)


def _mm_slab_body(x_ref, w_ref, o_ref):
    res = jnp.dot(x_ref[...].astype(BF16), w_ref[...], preferred_element_type=F32)
    for j in range(o_ref.shape[0]):
        o_ref[j] = res[:, j * LANES:(j + 1) * LANES]


def _matmul_slabs(x, w, tm):
    m, k = x.shape
    n = w.shape[1]
    return pl.pallas_call(
        _mm_slab_body, grid=(m // tm,),
        in_specs=[pl.BlockSpec((tm, k), lambda i: (i, 0)), pl.BlockSpec((k, n), lambda i: (0, 0))],
        out_specs=pl.BlockSpec((n // LANES, tm, LANES), lambda i: (0, i, 0)),
        out_shape=jax.ShapeDtypeStruct((n // LANES, m, LANES), F32),
        compiler_params=_params("parallel"), name="matmul_slabs")(x, w)


def _att_body(q_ref, kc_ref, kp_ref, vc_ref, vp_ref, pq_ref, pkc_ref, pkp_ref, o_ref, lse_ref,
              kbuf, vbuf, pkbuf, *, dil, rows, slopes):
    n = pl.program_id(1)
    nsub = rows // QB
    npair = ATT_HEADS // 2
    a = lax.broadcasted_iota(I32, (QB, 2 * QB), 0)
    c = lax.broadcasted_iota(I32, (QB, 2 * QB), 1)
    tri = jnp.where(c < QB, c - a, a - (c - QB)) >= 0
    lane = lax.broadcasted_iota(I32, (QB, LANES), 1)
    low = lane < HEAD_DIM

    def residue(r, carry):
        for p in range(npair):
            kbuf[p, 0:QB, :] = kp_ref[p, _rows(r, QB, dil), :].astype(BF16)
            kbuf[p, QB:, :] = kc_ref[p, _rows(r, rows, dil), :].astype(BF16)
            vbuf[p, 0:QB, :] = vp_ref[p, _rows(r, QB, dil), :].astype(BF16)
            vbuf[p, QB:, :] = vc_ref[p, _rows(r, rows, dil), :].astype(BF16)
        pkbuf[0] = pkp_ref[r]
        for j in range(nsub):
            pkbuf[j + 1] = pkc_ref[r, :, j * QB:(j + 1) * QB]

        def sub(sb, carry2):
            r0 = pl.multiple_of(sb * QB, QB)
            t0 = r + r0 * dil
            pq = pq_ref[r, pl.ds(r0, QB), :]
            pk = jnp.concatenate([pkbuf[sb], pkbuf[sb + 1]], axis=1)
            dist = (pq - pk).astype(F32)
            mask = tri & (c >= jnp.where((sb > 0) | (n > 0), 0, QB))
            heads = range(ATT_HEADS)
            ks = [kbuf[p, pl.ds(r0, 2 * QB), :] for p in range(npair)]
            vs = [vbuf[p, pl.ds(r0, 2 * QB), :] for p in range(npair)]
            qs = [q_ref[p, _rows(t0, QB, dil), :] * (HEAD_DIM ** -0.5) for p in range(npair)]
            qms = [jnp.where(low if h % 2 == 0 else ~low, qs[h // 2], 0.0).astype(BF16) for h in heads]
            ss = [lax.dot_general(qms[h], ks[h // 2], (((1,), (1,)), ((), ())), preferred_element_type=F32)
                  for h in heads]
            ss = [jnp.where(mask, ss[h] - slopes[h] * dist, NEG) for h in heads]
            ms = [jnp.max(ss[h], axis=-1, keepdims=True) for h in heads]
            es = [jnp.exp(ss[h] - ms[h]) for h in heads]
            ls = [jnp.sum(es[h], axis=-1, keepdims=True) for h in heads]
            outs = [jnp.dot(es[h].astype(BF16), vs[h // 2], preferred_element_type=F32) / ls[h] for h in heads]
            lse = jnp.zeros((QB, LANES), F32)
            for h in heads:
                lse = jnp.where(lane == h, ms[h] + jnp.log(ls[h]), lse)
            for p in range(npair):
                o_ref[p, _rows(t0, QB, dil), :] = jnp.where(low, outs[2 * p], outs[2 * p + 1])
            lse_ref[_rows(t0, QB, dil), :] = lse
            return carry2

        lax.fori_loop(0, nsub, sub, 0)
        return carry

    lax.fori_loop(0, dil, residue, 0)


def _att_branch(slabs, posc, posr, dil, batch, seq):
    sub_len = seq // dil
    rows = min(512 if dil == 1 else 2048 // dil, sub_len)
    tok = rows * dil
    nsteps = seq // tok
    prev_tok = QB * dil
    ratio = tok // prev_tok
    npair = ATT_HEADS // 2
    slopes = tuple(2.0 ** (-8.0 * (i + 1) / ATT_HEADS) for i in range(ATT_HEADS))

    def cur(which):
        return pl.BlockSpec((npair, tok, LANES), lambda b, n: (which, b * nsteps + n, 0))

    def prev(which):
        return pl.BlockSpec((npair, prev_tok, LANES),
                            lambda b, n: (which, jnp.maximum((b * nsteps + n) * ratio - 1, 0), 0))

    t = batch * seq
    return pl.pallas_call(
        functools.partial(_att_body, dil=dil, rows=rows, slopes=slopes),
        grid=(batch, nsteps),
        in_specs=[cur(0), cur(1), prev(1), cur(2), prev(2),
                  pl.BlockSpec((None, dil, rows, 1), lambda b, n: (b, 0, n, 0)),
                  pl.BlockSpec((None, dil, 1, rows), lambda b, n: (b, 0, 0, n)),
                  pl.BlockSpec((None, dil, 1, QB),
                               lambda b, n: (b, 0, 0, jnp.maximum(n * (rows // QB) - 1, 0)))],
        out_specs=[pl.BlockSpec((npair, tok, LANES), lambda b, n: (0, b * nsteps + n, 0)),
                   pl.BlockSpec((tok, LANES), lambda b, n: (b * nsteps + n, 0))],
        out_shape=[jax.ShapeDtypeStruct((npair, t, LANES), F32),
                   jax.ShapeDtypeStruct((t, LANES), F32)],
        scratch_shapes=[pltpu.VMEM((npair, rows + QB, LANES), BF16),
                        pltpu.VMEM((npair, rows + QB, LANES), BF16),
                        pltpu.VMEM((rows // QB + 1, 1, QB), I32)],
        compiler_params=_params("parallel", "arbitrary"),
        name=f"dilated_att_d{dil}")(slabs, slabs, slabs, slabs, slabs, posc, posr, posr)


def _merge_body(o1, o2, o3, l1, l2, l3, out):
    ls = [l1[...], l2[...], l3[...]]
    m = jnp.maximum(jnp.maximum(ls[0], ls[1]), ls[2])
    ws = [jnp.exp(l - m) for l in ls]
    inv = 1.0 / (ws[0] + ws[1] + ws[2])
    ws = [w * inv for w in ws]
    lane = lax.broadcasted_iota(I32, (out.shape[0], LANES), 1)
    low = lane < HEAD_DIM
    for p in range(ATT_HEADS // 2):
        acc = None
        for w, o in zip(ws, (o1, o2, o3)):
            wp = jnp.where(low, w[:, 2 * p:2 * p + 1], w[:, 2 * p + 1:2 * p + 2])
            t = wp * o[p]
            acc = t if acc is None else acc + t
        out[:, p * LANES:(p + 1) * LANES] = acc.astype(out.dtype)


def _att_merge(os_, ls_, tm=1024):
    npair, t, _ = os_[0].shape
    ospec = pl.BlockSpec((npair, tm, LANES), lambda i: (0, i, 0))
    lspec = pl.BlockSpec((tm, LANES), lambda i: (i, 0))
    return pl.pallas_call(
        _merge_body, grid=(t // tm,), in_specs=[ospec] * 3 + [lspec] * 3,
        out_specs=pl.BlockSpec((tm, ATT_WIDTH), lambda i: (i, 0)),
        out_shape=jax.ShapeDtypeStruct((t, ATT_WIDTH), BF16),
        compiler_params=_params("parallel"), name="att_merge")(*os_, *ls_)


def _split2(a):
    hi = a.astype(BF16)
    lo = (a - hi.astype(F32)).astype(BF16)
    return hi, lo


def _dot(a, b):
    return jnp.dot(a, b, preferred_element_type=F32)


def _dot_bf16(a, b):
    return _dot(a.astype(BF16), b.astype(BF16))


def _dot_hp(a, b):
    ah, al = _split2(a)
    bh, bl = _split2(b)
    return _dot(jnp.concatenate([ah, al, ah], axis=1), jnp.concatenate([bh, bh, bl], axis=0))


def _dot_t(a, b):
    return lax.dot_general(a, b, (((1,), (1,)), ((), ())), preferred_element_type=F32)


def _dot_t_hp(a, b):
    ah, al = _split2(a)
    bh, bl = _split2(b)
    return _dot_t(jnp.concatenate([ah, al, ah], axis=1), jnp.concatenate([bh, bh, bl], axis=1))


def _dn_body(hq, hk, hv, pq, pk, pv, small, gate, convw, pvec, normw, out, xbuf, qkv, s_ref, *, nch):
    n = pl.program_id(1)
    c_ = QB
    npair = DN_HEADS // 2

    @pl.when(n == 0)
    def _():
        s_ref[...] = jnp.zeros_like(s_ref)

    halo = 8
    has_prev = jnp.where(n > 0, 1.0, 0.0)
    for idx, (cur, prev) in enumerate(((hq, pq), (hk, pk), (hv, pv))):
        xbuf[idx, 0:halo, :] = prev[...] * has_prev
        xbuf[idx, halo:, :] = cur[...]
        acc = None
        for j in range(CONV_K):
            start = halo - (CONV_K - 1) + j
            t = xbuf[idx, start:start + nch * c_, :] * convw[j:j + 1, idx * DN_WIDTH:(idx + 1) * DN_WIDTH]
            acc = t if acc is None else acc + t
        qkv[idx] = acc * _sigmoid(acc)

    lane = lax.broadcasted_iota(I32, (c_, LANES), 1)
    low = lane < HEAD_DIM
    row = lax.broadcasted_iota(I32, (c_, c_), 0)
    col = lax.broadcasted_iota(I32, (c_, c_), 1)
    lower = row >= col
    strict = row > col
    eye = jnp.where(row == col, 1.0, 0.0)
    blockdiag = (row < HEAD_DIM) == (col < HEAD_DIM)

    def headsum(x):
        sa = jnp.sum(jnp.where(low, x, 0.0), axis=-1, keepdims=True)
        sb = jnp.sum(jnp.where(low, 0.0, x), axis=-1, keepdims=True)
        return jnp.where(low, sa, sb)

    def col_of(arr, j):
        return arr[:, j:j + 1]

    ltri = jnp.where(lower, 1.0, 0.0).astype(BF16)
    ltri3 = jnp.concatenate([ltri, ltri, ltri], axis=1)
    pair, xs, pws, qks = [], [], [], []
    for ch in range(nch):
        rs = slice(ch * c_, (ch + 1) * c_)
        sm = small[rs, :]
        beta_all = _sigmoid(sm)
        z = sm + pvec[1:2, :]
        softplus = jnp.maximum(z, 0.0) + jnp.log(1.0 + jnp.exp(-jnp.abs(z)))
        g_all = -jnp.exp(pvec[0:1, :]) * softplus
        g1 = g_all.astype(BF16)
        rem = g_all - g1.astype(F32)
        g2 = rem.astype(BF16)
        g3 = (rem - g2.astype(F32)).astype(BF16)
        gc_all = _dot(ltri3, jnp.concatenate([g1, g2, g3], axis=0))
        gc_t = gc_all.T

        for p in range(npair):
            cs = slice(p * LANES, (p + 1) * LANES)
            q, k, v = qkv[0, rs, cs], qkv[1, rs, cs], qkv[2, rs, cs]
            q = q * lax.rsqrt(headsum(q * q) + NORM_EPS) * (HEAD_DIM ** -0.5)
            k = k * lax.rsqrt(headsum(k * k) + NORM_EPS)
            beta = jnp.where(low, col_of(beta_all, 2 * p), col_of(beta_all, 2 * p + 1))
            gc = jnp.where(low, col_of(gc_all, DN_HEADS + 2 * p), col_of(gc_all, DN_HEADS + 2 * p + 1))
            eg = jnp.exp(gc)
            rhs = jnp.concatenate([v * beta, k * beta * eg], axis=1)
            pair.append(dict(q=q, k=k, kb=k.astype(BF16), gc=gc, eg=eg, rhs=rhs))

        for h in range(DN_HEADS):
            pr = pair[ch * npair + h // 2]
            sel = low if h % 2 == 0 else ~low
            gcol = col_of(gc_all, DN_HEADS + h)
            grow = gc_t[DN_HEADS + h:DN_HEADS + h + 1, :]
            dec = jnp.where(lower, jnp.exp(jnp.minimum(gcol - grow, 0.0)), 0.0)
            kk = _dot_t(jnp.where(sel, pr["k"], 0.0).astype(BF16), pr["kb"])
            qk = _dot_t_hp(jnp.where(sel, pr["q"], 0.0), pr["k"])
            nm = jnp.where(strict, col_of(beta_all, h) * kk * dec, 0.0)
            xs.append(eye - nm)
            pws.append(nm)
            qks.append((qk * dec).astype(BF16))

    for level in range(int(math.log2(c_)) - 1):
        mm = _dot_hp if level < 3 else _dot_bf16
        for i in range(len(xs)):
            pws[i] = mm(pws[i], pws[i])
        for i in range(len(xs)):
            xs[i] = xs[i] + mm(xs[i], pws[i])
    uws = [_dot_hp(xs[i], pair[i // 2]["rhs"]) for i in range(len(xs))]

    for ch in range(nch):
        rs = slice(ch * c_, (ch + 1) * c_)
        for p in range(npair):
            pr = pair[ch * npair + p]
            i0 = ch * DN_HEADS + 2 * p
            cs = slice(p * LANES, (p + 1) * LANES)
            u = jnp.where(low, uws[i0][:, :LANES], uws[i0 + 1][:, :LANES])
            w = jnp.where(low, uws[i0][:, LANES:], uws[i0 + 1][:, LANES:])
            state = s_ref[p]
            sb16 = state.astype(BF16)
            vnew = u - _dot(w.astype(BF16), sb16)
            vn16 = vnew.astype(BF16)
            o = _dot((pr["q"] * pr["eg"]).astype(BF16), sb16)
            o = o + jnp.where(low, _dot(qks[i0], vn16), _dot(qks[i0 + 1], vn16))
            glast = pr["gc"][c_ - 1:c_, :]
            kd = pr["k"] * jnp.exp(glast - pr["gc"])
            upd = _dot(kd.T.astype(BF16), vn16)
            s_ref[p] = state * jnp.exp(glast) + jnp.where(blockdiag, upd, 0.0)
            ms = headsum(o * o) * (1.0 / HEAD_DIM)
            gt = gate[rs, cs]
            out[rs, cs] = (o * lax.rsqrt(ms + NORM_EPS) * normw[:, cs] * (gt * _sigmoid(gt))).astype(out.dtype)


def _delta_net(hr, convw, pvec, normw, batch, seq, nch=2):
    h3 = hr.reshape(batch, seq, R_COLS)
    rows = nch * QB
    qcol = R_DN // DN_WIDTH

    def cur(off):
        return pl.BlockSpec((None, rows, DN_WIDTH), lambda b, n: (b, n, qcol + off))

    def prev(off):
        return pl.BlockSpec((None, 8, DN_WIDTH),
                            lambda b, n: (b, jnp.maximum(n * (rows // 8) - 1, 0), qcol + off))

    out = pl.pallas_call(
        functools.partial(_dn_body, nch=nch), grid=(batch, seq // rows),
        in_specs=[cur(0), cur(1), cur(2), prev(0), prev(1), prev(2),
                  pl.BlockSpec((None, rows, LANES), lambda b, n: (b, n, R_SMALL // LANES)),
                  pl.BlockSpec((None, rows, DN_WIDTH), lambda b, n: (b, n, R_GATE // DN_WIDTH)),
                  pl.BlockSpec((CONV_K, 3 * DN_WIDTH), lambda b, n: (0, 0)),
                  pl.BlockSpec((2, LANES), lambda b, n: (0, 0)),
                  pl.BlockSpec((1, DN_WIDTH), lambda b, n: (0, 0))],
        out_specs=pl.BlockSpec((None, rows, DN_WIDTH), lambda b, n: (b, n, 0)),
        out_shape=jax.ShapeDtypeStruct((batch, seq, DN_WIDTH), BF16),
        scratch_shapes=[pltpu.VMEM((3, rows + 8, DN_WIDTH), F32),
                        pltpu.VMEM((3, rows, DN_WIDTH), F32),
                        pltpu.VMEM((DN_HEADS // 2, LANES, LANES), F32)],
        compiler_params=_params("parallel", "arbitrary"),
        name="gated_delta_rule")(h3, h3, h3, h3, h3, h3, h3, h3, convw, pvec, normw)
    return out.reshape(batch * seq, DN_WIDTH)


def _pool_body(cur, prev, wbd, scale, out, xbuf, *, rows):
    n = pl.program_id(1)
    halo = 16
    xbuf[0:halo, :] = prev[...] * jnp.where(n > 0, 1.0, 0.0)
    xbuf[halo:, :] = cur[...]
    u = cur[...]
    lane = lax.broadcasted_iota(I32, (rows, POOL_WIDTH), 1)
    tpos = n * rows + lax.broadcasted_iota(I32, (rows, POOL_WIDTH), 0)
    run = u
    pooled = jnp.zeros_like(u)
    for j in range(1, max(POOL_WINDOWS)):
        run = run + xbuf[halo - j:halo - j + rows, :]
        if j + 1 in POOL_WINDOWS:
            gi = POOL_WINDOWS.index(j + 1)
            cnt = jnp.minimum(tpos + 1, j + 1).astype(F32)
            sel = (lane >= gi * POOL_GDIM) & (lane < (gi + 1) * POOL_GDIM)
            pooled = jnp.where(sel, run / cnt - u, pooled)
    mixed = jnp.dot(pooled.astype(BF16), wbd[...], preferred_element_type=F32)
    out[...] = (mixed * scale[...]).astype(out.dtype)


def _pool(hr, wbd, scale, batch, seq, rows=512):
    h3 = hr.reshape(batch, seq, R_COLS)
    out = pl.pallas_call(
        functools.partial(_pool_body, rows=rows), grid=(batch, seq // rows),
        in_specs=[pl.BlockSpec((None, rows, POOL_WIDTH), lambda b, n: (b, n, R_POOL // POOL_WIDTH)),
                  pl.BlockSpec((None, 16, POOL_WIDTH),
                               lambda b, n: (b, jnp.maximum(n * (rows // 16) - 1, 0), R_POOL // POOL_WIDTH)),
                  pl.BlockSpec((POOL_WIDTH, POOL_WIDTH), lambda b, n: (0, 0)),
                  pl.BlockSpec((1, POOL_WIDTH), lambda b, n: (0, 0))],
        out_specs=pl.BlockSpec((None, rows, POOL_WIDTH), lambda b, n: (b, n, 0)),
        out_shape=jax.ShapeDtypeStruct((batch, seq, POOL_WIDTH), BF16),
        scratch_shapes=[pltpu.VMEM((rows + 16, POOL_WIDTH), F32)],
        compiler_params=_params("parallel", "arbitrary"), name="multiscale_pool")(h3, h3, wbd, scale)
    return out.reshape(batch * seq, POOL_WIDTH)


def _proj_ln_body(a, d, p, x, wa, wd, wp, g, b, o):
    y = jnp.dot(a[...], wa[...], preferred_element_type=F32)
    y = y + jnp.dot(d[...], wd[...], preferred_element_type=F32)
    y = y + jnp.dot(p[...], wp[...], preferred_element_type=F32)
    o[...] = _layer_norm_rows(ALPHA * x[...] + y, g[...], b[...])


def _proj_ln(att, dn, pool, x, wa, wd, wp, g, b, tm=512):
    t = x.shape[0]

    def rows(wd_):
        return pl.BlockSpec((tm, wd_), lambda i: (i, 0))

    def full(a_):
        return pl.BlockSpec(a_.shape, lambda i: (0, 0))

    return pl.pallas_call(
        _proj_ln_body, grid=(t // tm,),
        in_specs=[rows(ATT_WIDTH), rows(DN_WIDTH), rows(POOL_WIDTH), rows(D_MODEL),
                  full(wa), full(wd), full(wp), full(g), full(b)],
        out_specs=rows(D_MODEL), out_shape=jax.ShapeDtypeStruct((t, D_MODEL), F32),
        compiler_params=_params("parallel"), name="out_proj_ln")(att, dn, pool, x, wa, wd, wp, g, b)


def _xattn_body(x, wq, k, v, wo, g, b, o):
    xv = x[...]
    q = jnp.dot(xv.astype(BF16), wq[...], preferred_element_type=F32).astype(BF16)
    heads = []
    for h in range(X_HEADS):
        cs = slice(h * X_HEAD_DIM, (h + 1) * X_HEAD_DIM)
        s = lax.dot_general(q[:, cs], k[:, cs], (((1,), (1,)), ((), ())), preferred_element_type=F32)
        m = jnp.max(s, axis=-1, keepdims=True)
        e = jnp.exp(s - m)
        l = jnp.sum(e, axis=-1, keepdims=True)
        oh = jnp.dot(e.astype(BF16), v[:, cs], preferred_element_type=F32) / l
        heads.append(oh.astype(BF16))
    oc = jnp.concatenate(heads, axis=1)
    y = jnp.dot(oc, wo[...], preferred_element_type=F32)
    o[...] = _layer_norm_rows(ALPHA * xv + y, g[...], b[...])


def _cross_attention(x, wq, k, v, wo, g, b, seq, tm=512):
    t = x.shape[0]
    mem_len = k.shape[1]
    per_batch = seq // tm

    def full(a_):
        return pl.BlockSpec(a_.shape, lambda i: (0, 0))

    kv = pl.BlockSpec((None, mem_len, D_MODEL), lambda i: (i // per_batch, 0, 0))
    rows = pl.BlockSpec((tm, D_MODEL), lambda i: (i, 0))
    return pl.pallas_call(
        _xattn_body, grid=(t // tm,),
        in_specs=[rows, full(wq), kv, kv, full(wo), full(g), full(b)],
        out_specs=rows, out_shape=jax.ShapeDtypeStruct((t, D_MODEL), F32),
        compiler_params=_params("parallel"), name="cross_attention_ln")(x, wq, k, v, wo, g, b)


def _ffn_body(x, w1, w3, w2, g, b, o, xb, acc):
    f = pl.program_id(1)

    @pl.when(f == 0)
    def _():
        xb[...] = x[...].astype(BF16)
        acc[...] = jnp.zeros_like(acc)

    xv = xb[...]
    h1 = jnp.dot(xv, w1[...], preferred_element_type=F32)
    h3 = jnp.dot(xv, w3[...], preferred_element_type=F32)
    hh = (h1 * _sigmoid(h1) * h3).astype(BF16)
    acc[...] += jnp.dot(hh, w2[...], preferred_element_type=F32)

    @pl.when(f == pl.num_programs(1) - 1)
    def _():
        o[...] = _layer_norm_rows(ALPHA * x[...] + acc[...], g[...], b[...])


def _ffn_dense(x, w1, w3, w2, g, b, tm=1024, tf=256):
    t = x.shape[0]
    dff = w1.shape[1]
    rows = pl.BlockSpec((tm, D_MODEL), lambda i, f: (i, 0))
    vec = pl.BlockSpec((1, D_MODEL), lambda i, f: (0, 0))
    return pl.pallas_call(
        _ffn_body, grid=(t // tm, dff // tf),
        in_specs=[rows, pl.BlockSpec((D_MODEL, tf), lambda i, f: (0, f)),
                  pl.BlockSpec((D_MODEL, tf), lambda i, f: (0, f)),
                  pl.BlockSpec((tf, D_MODEL), lambda i, f: (f, 0)), vec, vec],
        out_specs=rows, out_shape=jax.ShapeDtypeStruct((t, D_MODEL), F32),
        scratch_shapes=[pltpu.VMEM((tm, D_MODEL), BF16), pltpu.VMEM((tm, D_MODEL), F32)],
        compiler_params=_params("parallel", "arbitrary"), name="ffn_swiglu_ln")(x, w1, w3, w2, g, b)


def _router_body(x, rw, eidx, gates, ranks, counts, carry, *, tm):
    i = pl.program_id(0)

    @pl.when(i == 0)
    def _():
        carry[...] = jnp.zeros_like(carry)

    logits = jnp.dot(x[...], rw[...], preferred_element_type=F32, precision=lax.Precision.HIGHEST)
    lane = lax.broadcasted_iota(I32, (tm, LANES), 1)
    lanef = lane.astype(F32)
    lg = jnp.where(lane < N_EXPERTS, logits, NEG)
    m1 = jnp.max(lg, axis=-1, keepdims=True)
    i1 = jnp.min(jnp.where(lg == m1, lanef, float(LANES)), axis=-1, keepdims=True)
    lg2 = jnp.where(lanef == i1, NEG, lg)
    m2 = jnp.max(lg2, axis=-1, keepdims=True)
    i2 = jnp.min(jnp.where(lg2 == m2, lanef, float(LANES)), axis=-1, keepdims=True)
    e21 = jnp.exp(m2 - m1)
    g1 = 1.0 / (1.0 + e21)
    g2 = e21 * g1
    hot1 = lanef == i1
    hot2 = lanef == i2
    onehot = jnp.where(hot1 | hot2, 1.0, 0.0)
    r = lax.broadcasted_iota(I32, (tm, tm), 0)
    c = lax.broadcasted_iota(I32, (tm, tm), 1)
    before = jnp.dot(jnp.where(r > c, 1.0, 0.0).astype(BF16), onehot.astype(BF16),
                     preferred_element_type=F32) + carry[...]
    rank1 = jnp.sum(jnp.where(hot1, before, 0.0), axis=-1, keepdims=True)
    rank2 = jnp.sum(jnp.where(hot2, before, 0.0), axis=-1, keepdims=True)
    carry[...] = carry[...] + jnp.sum(onehot, axis=0, keepdims=True)
    eidx[...] = jnp.where(lane == 0, i1, jnp.where(lane == 1, i2, 0.0)).astype(I32)
    gates[...] = jnp.where(lane == 0, g1, jnp.where(lane == 1, g2, 0.0))
    ranks[...] = jnp.where(lane == 0, rank1, jnp.where(lane == 1, rank2, 0.0)).astype(I32)
    counts[...] = jnp.broadcast_to(carry[...], counts.shape).astype(I32)


def _router(x, rw, tm=512):
    t = x.shape[0]
    rows = pl.BlockSpec((tm, LANES), lambda i: (i, 0))
    return pl.pallas_call(
        functools.partial(_router_body, tm=tm), grid=(t // tm,),
        in_specs=[pl.BlockSpec((tm, D_MODEL), lambda i: (i, 0)),
                  pl.BlockSpec((D_MODEL, LANES), lambda i: (0, 0))],
        out_specs=[rows, rows, rows, pl.BlockSpec((8, LANES), lambda i: (0, 0))],
        out_shape=[jax.ShapeDtypeStruct((t, LANES), I32), jax.ShapeDtypeStruct((t, LANES), F32),
                   jax.ShapeDtypeStruct((t, LANES), I32), jax.ShapeDtypeStruct((8, LANES), I32)],
        scratch_shapes=[pltpu.VMEM((1, LANES), F32)],
        compiler_params=_params("arbitrary"), name="router_top2")(x, rw)


def _moe_body(be, nused, xs, w1, w3, w2, ys, xb, acc):
    i = pl.program_id(0)
    f = pl.program_id(1)
    last = f == pl.num_programs(1) - 1
    used = i < nused[0]

    @pl.when(used & (f == 0))
    def _():
        xb[...] = xs[...].astype(BF16)
        acc[...] = jnp.zeros_like(acc)

    @pl.when(used)
    def _():
        xv = xb[...]
        h1 = jnp.dot(xv, w1[...], preferred_element_type=F32)
        h3 = jnp.dot(xv, w3[...], preferred_element_type=F32)
        hh = (h1 * _sigmoid(h1) * h3).astype(BF16)
        acc[...] += jnp.dot(hh, w2[...], preferred_element_type=F32)

    @pl.when(used & last)
    def _():
        ys[...] = acc[...]

    @pl.when(jnp.logical_not(used) & last)
    def _():
        ys[...] = jnp.zeros_like(ys)


def _moe_experts(block_e, n_used, xs, w1, w3, w2, tm, tf=896):
    n_slots = xs.shape[0]
    dex = w1.shape[2]
    nblk = n_slots // tm

    def live(i, nu):
        return jnp.minimum(i, nu[0] - 1)

    grid_spec = pltpu.PrefetchScalarGridSpec(
        num_scalar_prefetch=2, grid=(nblk, dex // tf),
        in_specs=[pl.BlockSpec((tm, D_MODEL), lambda i, f, be, nu: (live(i, nu), 0)),
                  pl.BlockSpec((None, D_MODEL, tf), lambda i, f, be, nu: (be[i], 0, jnp.where(i < nu[0], f, dex // tf - 1))),
                  pl.BlockSpec((None, D_MODEL, tf), lambda i, f, be, nu: (be[i], 0, jnp.where(i < nu[0], f, dex // tf - 1))),
                  pl.BlockSpec((None, tf, D_MODEL), lambda i, f, be, nu: (be[i], jnp.where(i < nu[0], f, dex // tf - 1), 0))],
        out_specs=pl.BlockSpec((tm, D_MODEL), lambda i, f, be, nu: (i, 0)),
        scratch_shapes=[pltpu.VMEM((tm, D_MODEL), BF16), pltpu.VMEM((tm, D_MODEL), F32)])
    return pl.pallas_call(
        _moe_body, grid_spec=grid_spec, out_shape=jax.ShapeDtypeStruct((n_slots, D_MODEL), F32),
        compiler_params=_params("arbitrary", "arbitrary"), name="moe_experts")(block_e, n_used, xs, w1, w3, w2)


SUBLANES = 8


def _tile_row(ref3, idx):
    return ref3.at[lax.shift_right_logical(idx, 3), pl.ds(idx & (SUBLANES - 1), 1), :]


def _start_row_gather(src, ids, buf, sem, slot, nrows):
    def issue(j, carry):
        for u in range(SUBLANES):
            idx = ids[0, j * SUBLANES + u]
            pltpu.make_async_copy(_tile_row(src, idx), buf.at[slot, j, pl.ds(u, 1), :], sem.at[slot]).start()
        return carry

    lax.fori_loop(0, nrows // SUBLANES, issue, 0)


def _wait_row_gather(src, buf, sem, slot, nrows):
    pltpu.make_async_copy(src.at[pl.ds(0, nrows // SUBLANES)], buf.at[slot], sem.at[slot]).wait()


def _pipelined_row_gather(src, ids_cur, ids_next, buf, sem, nrows):
    i = pl.program_id(0)
    slot = lax.rem(i, 2)

    @pl.when(i == 0)
    def _():
        _start_row_gather(src, ids_cur, buf, sem, 0, nrows)

    @pl.when(i + 1 < pl.num_programs(0))
    def _():
        _start_row_gather(src, ids_next, buf, sem, 1 - slot, nrows)

    _wait_row_gather(src, buf, sem, slot, nrows)
    return slot


def _dispatch_body(ids, x, init, xs, sem, *, tm):
    del init

    def issue(j, carry):
        for u in range(SUBLANES):
            for k in range(2):
                dst = _tile_row(xs, ids[0, k * tm + j * SUBLANES + u])
                pltpu.make_async_copy(x.at[j, pl.ds(u, 1), :], dst, sem).start()
        return carry

    lax.fori_loop(0, tm // SUBLANES, issue, 0)
    for _ in range(2):
        pltpu.make_async_copy(x, xs.at[pl.ds(0, tm // SUBLANES)], sem).wait()


def _moe_dispatch(x, ids, n_slots, tm):
    t = x.shape[0]
    xs = pl.pallas_call(
        functools.partial(_dispatch_body, tm=tm), grid=(t // tm,),
        in_specs=[pl.BlockSpec((None, 1, 2 * tm), lambda i: (i, 0, 0), memory_space=pltpu.SMEM),
                  pl.BlockSpec((tm // SUBLANES, SUBLANES, D_MODEL), lambda i: (i, 0, 0)),
                  pl.BlockSpec(memory_space=pl.ANY)],
        out_specs=pl.BlockSpec(memory_space=pl.ANY),
        out_shape=jax.ShapeDtypeStruct((n_slots // SUBLANES, SUBLANES, D_MODEL), F32),
        scratch_shapes=[pltpu.SemaphoreType.DMA(())],
        input_output_aliases={2: 0},
        compiler_params=_params("arbitrary"),
        name="moe_dispatch_scatter")(ids, x.reshape(t // SUBLANES, SUBLANES, D_MODEL),
                                     jnp.zeros((n_slots // SUBLANES, SUBLANES, D_MODEL), F32))
    return xs.reshape(n_slots, D_MODEL)


def _combine_ln_body(ids_cur, ids_next, x, ys, gates, g, b, o, buf, sem, *, tm):
    slot = _pipelined_row_gather(ys, ids_cur, ids_next, buf, sem, 2 * tm)
    gt = gates[...]
    nt = tm // SUBLANES
    ya = buf[slot, 0:nt].reshape(tm, D_MODEL)
    yb = buf[slot, nt:].reshape(tm, D_MODEL)
    o[...] = _layer_norm_rows(ALPHA * x[...] + (ya * gt[:, 0:1] + yb * gt[:, 1:2]), g[...], b[...])


def _combine_ln(x, ys, ids, gates, g, b, tm):
    t = x.shape[0]
    nt = t // tm
    smem = functools.partial(pl.BlockSpec, (None, 1, 2 * tm), memory_space=pltpu.SMEM)
    rows = pl.BlockSpec((tm, D_MODEL), lambda i: (i, 0))
    vec = pl.BlockSpec((1, D_MODEL), lambda i: (0, 0))
    return pl.pallas_call(
        functools.partial(_combine_ln_body, tm=tm), grid=(nt,),
        in_specs=[smem(lambda i: (i, 0, 0)), smem(lambda i: (jnp.minimum(i + 1, nt - 1), 0, 0)),
                  rows, pl.BlockSpec(memory_space=pl.ANY),
                  pl.BlockSpec((tm, LANES), lambda i: (i, 0)), vec, vec],
        out_specs=rows, out_shape=jax.ShapeDtypeStruct((t, D_MODEL), F32),
        scratch_shapes=[pltpu.VMEM((2, 2 * tm // SUBLANES, SUBLANES, D_MODEL), F32),
                        pltpu.SemaphoreType.DMA((2,))],
        compiler_params=_params("arbitrary"),
        name="moe_combine_ln")(ids, ids, x, ys.reshape(-1, SUBLANES, D_MODEL), gates, g, b)


def _moe_layer(x, rw_pad, w1, w3, w2, g, b, tm=512):
    t = x.shape[0]
    eidx, gates, ranks, counts = _router(x, rw_pad)
    cnt = counts[0, :N_EXPERTS]
    padded = (cnt + tm - 1) // tm * tm
    pad_end = jnp.cumsum(padded)
    pad_start = pad_end - padded
    e2 = eidx[:, :2]
    dest = pad_start[e2] + ranks[:, :2]
    nblk = (2 * t) // tm + N_EXPERTS
    n_slots = nblk * tm
    tt = 512
    ids = dest.reshape(t // tt, tt, 2).transpose(0, 2, 1).reshape(t // tt, 1, 2 * tt)
    n_used = (pad_end[-1] // tm).astype(I32).reshape(1)
    blk = jnp.arange(nblk, dtype=I32) * tm
    block_e = jnp.minimum(jnp.sum(pad_end[None, :] <= blk[:, None], axis=1), N_EXPERTS - 1).astype(I32)
    block_e = jnp.where(jnp.arange(nblk) < n_used[0], block_e, block_e[jnp.maximum(n_used[0] - 1, 0)])
    xs = _moe_dispatch(x, ids, n_slots, tt)
    ys = _moe_experts(block_e, n_used, xs, w1, w3, w2, tm)
    return _combine_ln(x, ys, ids, gates, g, b, tt)


def _block_diag(w):
    g, c, _ = w.shape
    eye = jnp.eye(g, dtype=w.dtype)
    return (eye[:, None, :, None] * w[:, :, None, :]).reshape(g * c, g * c)


def kernel(x, mem, positions, w_in, conv_w, a_log, dt_bias, dn_norm_w, pool_w, pool_scale, w_out, ln_mix_g, ln_mix_b, xq_w, xk_w, xv_w, xo_w, ln_x_g, ln_x_b, ffn_w1, ffn_w3, ffn_w2, router_w, moe_w1, moe_w3, moe_w2, ln_ffn_g, ln_ffn_b):
    batch, seq, _ = x.shape
    t = batch * seq
    depth = w_in.shape[0]
    xf = x.reshape(t, D_MODEL)
    memf = mem.reshape(batch * mem.shape[1], D_MODEL)

    posc, posr = {}, {}
    for dil in DILATIONS:
        pt = positions.reshape(batch, seq // dil, dil).transpose(0, 2, 1)
        posc[dil] = pt[..., None]
        posr[dil] = pt[:, :, None, :]

    for l in range(depth):
        wl = w_in[l]
        small = jnp.pad(wl[:, OFF_BETA:OFF_GATE], ((0, 0), (0, LANES - 2 * DN_HEADS)))
        w_rest = jnp.concatenate([wl[:, OFF_DN:OFF_BETA], wl[:, OFF_GATE:OFF_POOL], wl[:, OFF_POOL:], small],
                                 axis=1).astype(BF16)
        slabs = _matmul_slabs(xf, wl[:, :OFF_DN].astype(BF16), 512)
        hr = _matmul(xf, w_rest, F32, 1024, R_COLS // 3)

        branches = [_att_branch(slabs, posc[d], posr[d], d, batch, seq) for d in DILATIONS]
        att = _att_merge([o for o, _ in branches], [s for _, s in branches])

        pvec = jnp.zeros((2, LANES), F32)
        pvec = pvec.at[0, DN_HEADS:2 * DN_HEADS].set(a_log[l]).at[1, DN_HEADS:2 * DN_HEADS].set(dt_bias[l])
        normw = jnp.tile(dn_norm_w[l], DN_HEADS)[None, :]
        dn = _delta_net(hr, conv_w[l], pvec, normw, batch, seq)

        pool = _pool(hr, _block_diag(pool_w[l]).astype(BF16), pool_scale[l][None, :], batch, seq)

        wo = w_out[l].astype(BF16)
        xf = _proj_ln(att, dn, pool, xf, wo[:ATT_WIDTH], wo[ATT_WIDTH:ATT_WIDTH + DN_WIDTH],
                      wo[ATT_WIDTH + DN_WIDTH:], ln_mix_g[l][None, :], ln_mix_b[l][None, :])

        wkv = jnp.concatenate([xk_w[l], xv_w[l]], axis=1).astype(BF16)
        kv = _matmul(memf, wkv, BF16, 512, 1024).reshape(batch, mem.shape[1], 2 * D_MODEL)
        xf = _cross_attention(xf, (xq_w[l] * (X_HEAD_DIM ** -0.5)).astype(BF16), kv[..., :D_MODEL],
                              kv[..., D_MODEL:], xo_w[l].astype(BF16), ln_x_g[l][None, :],
                              ln_x_b[l][None, :], seq)

        j = l // 2
        gl, bl = ln_ffn_g[l][None, :], ln_ffn_b[l][None, :]
        if l % 2 == 0:
            xf = _ffn_dense(xf, ffn_w1[j].astype(BF16), ffn_w3[j].astype(BF16), ffn_w2[j].astype(BF16), gl, bl)
        else:
            rw_pad = jnp.pad(router_w[j], ((0, 0), (0, LANES - N_EXPERTS)))
            xf = _moe_layer(xf, rw_pad, moe_w1[j].astype(BF16), moe_w3[j].astype(BF16),
                            moe_w2[j].astype(BF16), gl, bl)
    return xf.reshape(batch, seq, D_MODEL)
```

```python
import functools
import math

import jax
import jax.numpy as jnp
from jax import lax
from jax.experimental import pallas as pl
from jax.experimental.pallas import tpu as pltpu

F32, BF16, I32 = jnp.float32, jnp.bfloat16, jnp.int32

D_MODEL = 1024
HEAD_DIM = 64
ATT_HEADS = 6
ATT_WIDTH = ATT_HEADS * HEAD_DIM
DN_HEADS = 6
DN_WIDTH = DN_HEADS * HEAD_DIM
POOL_WIDTH = 256
POOL_WINDOWS = (2, 4, 8, 16)
POOL_GDIM = POOL_WIDTH // len(POOL_WINDOWS)
DILATIONS = (1, 4, 16)
CONV_K = 4
X_HEADS = 4
X_HEAD_DIM = D_MODEL // X_HEADS
N_EXPERTS = 8
DEPTH = 4
ALPHA = (2 * DEPTH) ** 0.25
LN_EPS = 1e-5
NORM_EPS = 1e-6

OFF_DN = 3 * ATT_WIDTH
OFF_BETA = OFF_DN + 3 * DN_WIDTH
OFF_DECAY = OFF_BETA + DN_HEADS
OFF_GATE = OFF_DECAY + DN_HEADS
OFF_POOL = OFF_GATE + DN_WIDTH

LANES = 128
R_DN = 0
R_GATE = R_DN + 3 * DN_WIDTH
R_POOL = R_GATE + DN_WIDTH
R_SMALL = R_POOL + POOL_WIDTH
R_COLS = R_SMALL + LANES

QB = 128
NEG = -1e30
VMEM_LIMIT = 48 * 1024 * 1024


def _params(*sem):
    return pltpu.CompilerParams(dimension_semantics=sem, vmem_limit_bytes=VMEM_LIMIT)


def _sigmoid(x):
    return 1.0 / (1.0 + jnp.exp(-x))


def _layer_norm_rows(z, g, b):
    mu = jnp.mean(z, axis=-1, keepdims=True)
    zc = z - mu
    var = jnp.mean(zc * zc, axis=-1, keepdims=True)
    return zc * lax.rsqrt(var + LN_EPS) * g + b


def _rows(start, size, stride):
    return pl.ds(start, size) if stride == 1 else pl.ds(start, size, stride=stride)


def _mm_body(x_ref, w_ref, o_ref):
    o_ref[...] = jnp.dot(x_ref[...].astype(BF16), w_ref[...],
                         preferred_element_type=F32).astype(o_ref.dtype)


def _matmul(x, w, out_dtype, tm, tn):
    m, k = x.shape
    n = w.shape[1]
    return pl.pallas_call(
        _mm_body, grid=(m // tm, n // tn),
        in_specs=[pl.BlockSpec((tm, k), lambda i, j: (i, 0)),
                  pl.BlockSpec((k, tn), lambda i, j: (0, j))],
        out_specs=pl.BlockSpec((tm, tn), lambda i, j: (i, j)),
        out_shape=jax.ShapeDtypeStruct((m, n), out_dtype),
        compiler_params=_params("parallel", "arbitrary"), name="matmul")(x, w)


def _mm_slab_body(x_ref, w_ref, o_ref):
    res = jnp.dot(x_ref[...].astype(BF16), w_ref[...], preferred_element_type=F32)
    for j in range(o_ref.shape[0]):
        o_ref[j] = res[:, j * LANES:(j + 1) * LANES]


def _matmul_slabs(x, w, tm):
    m, k = x.shape
    n = w.shape[1]
    return pl.pallas_call(
        _mm_slab_body, grid=(m // tm,),
        in_specs=[pl.BlockSpec((tm, k), lambda i: (i, 0)), pl.BlockSpec((k, n), lambda i: (0, 0))],
        out_specs=pl.BlockSpec((n // LANES, tm, LANES), lambda i: (0, i, 0)),
        out_shape=jax.ShapeDtypeStruct((n // LANES, m, LANES), F32),
        compiler_params=_params("parallel"), name="matmul_slabs")(x, w)


def _att_body(q_ref, kc_ref, vc_ref, pq_ref, pkc_ref, o_ref, lse_ref,
              kbuf, vbuf, pkbuf, kprev, vprev, pkprev, *, dil, rows, slopes):
    n = pl.program_id(1)
    nsub = rows // QB
    npair = ATT_HEADS // 2

    @pl.when(n == 0)
    def _():
        kprev[...] = jnp.zeros_like(kprev)
        vprev[...] = jnp.zeros_like(vprev)
        pkprev[...] = jnp.zeros_like(pkprev)

    a = lax.broadcasted_iota(I32, (QB, 2 * QB), 0)
    c = lax.broadcasted_iota(I32, (QB, 2 * QB), 1)
    tri = jnp.where(c < QB, c - a, a - (c - QB)) >= 0
    lane = lax.broadcasted_iota(I32, (QB, LANES), 1)
    low = lane < HEAD_DIM

    def residue(r, carry):
        for p in range(npair):
            kbuf[p, 0:QB, :] = kprev[r, p]
            kbuf[p, QB:, :] = kc_ref[p, _rows(r, rows, dil), :].astype(BF16)
            vbuf[p, 0:QB, :] = vprev[r, p]
            vbuf[p, QB:, :] = vc_ref[p, _rows(r, rows, dil), :].astype(BF16)
            kprev[r, p] = kbuf[p, rows:, :]
            vprev[r, p] = vbuf[p, rows:, :]
        pkbuf[0] = pkprev[r]
        for j in range(nsub):
            pkbuf[j + 1] = pkc_ref[r, :, j * QB:(j + 1) * QB]
        pkprev[r] = pkbuf[nsub]

        def sub(sb, carry2):
            r0 = pl.multiple_of(sb * QB, QB)
            t0 = r + r0 * dil
            pq = pq_ref[r, pl.ds(r0, QB), :]
            pk = jnp.concatenate([pkbuf[sb], pkbuf[sb + 1]], axis=1)
            dist = (pq - pk).astype(F32)
            mask = tri & (c >= jnp.where((sb > 0) | (n > 0), 0, QB))
            heads = range(ATT_HEADS)
            ks = [kbuf[p, pl.ds(r0, 2 * QB), :] for p in range(npair)]
            vs = [vbuf[p, pl.ds(r0, 2 * QB), :] for p in range(npair)]
            qs = [q_ref[p, _rows(t0, QB, dil), :] * (HEAD_DIM ** -0.5) for p in range(npair)]
            qms = [jnp.where(low if h % 2 == 0 else ~low, qs[h // 2], 0.0).astype(BF16) for h in heads]
            ss = [lax.dot_general(qms[h], ks[h // 2], (((1,), (1,)), ((), ())), preferred_element_type=F32)
                  for h in heads]
            ss = [jnp.where(mask, ss[h] - slopes[h] * dist, NEG) for h in heads]
            ms = [jnp.max(ss[h], axis=-1, keepdims=True) for h in heads]
            es = [jnp.exp(ss[h] - ms[h]) for h in heads]
            ls = [jnp.sum(es[h], axis=-1, keepdims=True) for h in heads]
            outs = [jnp.dot(es[h].astype(BF16), vs[h // 2], preferred_element_type=F32) / ls[h] for h in heads]
            lse = jnp.zeros((QB, LANES), F32)
            for h in heads:
                lse = jnp.where(lane == h, ms[h] + jnp.log(ls[h]), lse)
            for p in range(npair):
                o_ref[p, _rows(t0, QB, dil), :] = jnp.where(low, outs[2 * p], outs[2 * p + 1])
            lse_ref[_rows(t0, QB, dil), :] = lse
            return carry2

        lax.fori_loop(0, nsub, sub, 0)
        return carry

    lax.fori_loop(0, dil, residue, 0)


def _att_branch(slabs, posc, posr, dil, batch, seq):
    sub_len = seq // dil
    rows = min(512 if dil == 1 else 2048 // dil, sub_len)
    tok = rows * dil
    nsteps = seq // tok
    npair = ATT_HEADS // 2
    slopes = tuple(2.0 ** (-8.0 * (i + 1) / ATT_HEADS) for i in range(ATT_HEADS))

    def cur(which):
        return pl.BlockSpec((npair, tok, LANES), lambda b, n: (which, b * nsteps + n, 0))

    t = batch * seq
    return pl.pallas_call(
        functools.partial(_att_body, dil=dil, rows=rows, slopes=slopes),
        grid=(batch, nsteps),
        in_specs=[cur(0), cur(1), cur(2),
                  pl.BlockSpec((None, dil, rows, 1), lambda b, n: (b, 0, n, 0)),
                  pl.BlockSpec((None, dil, 1, rows), lambda b, n: (b, 0, 0, n))],
        out_specs=[pl.BlockSpec((npair, tok, LANES), lambda b, n: (0, b * nsteps + n, 0)),
                   pl.BlockSpec((tok, LANES), lambda b, n: (b * nsteps + n, 0))],
        out_shape=[jax.ShapeDtypeStruct((npair, t, LANES), F32),
                   jax.ShapeDtypeStruct((t, LANES), F32)],
        scratch_shapes=[pltpu.VMEM((npair, rows + QB, LANES), BF16),
                        pltpu.VMEM((npair, rows + QB, LANES), BF16),
                        pltpu.VMEM((rows // QB + 1, 1, QB), I32),
                        pltpu.VMEM((dil, npair, QB, LANES), BF16),
                        pltpu.VMEM((dil, npair, QB, LANES), BF16),
                        pltpu.VMEM((dil, 1, QB), I32)],
        compiler_params=_params("arbitrary", "arbitrary"),
        name=f"dilated_att_d{dil}")(slabs, slabs, slabs, posc, posr)


def _merge_body(o1, o2, o3, l1, l2, l3, out):
    ls = [l1[...], l2[...], l3[...]]
    m = jnp.maximum(jnp.maximum(ls[0], ls[1]), ls[2])
    ws = [jnp.exp(l - m) for l in ls]
    inv = 1.0 / (ws[0] + ws[1] + ws[2])
    ws = [w * inv for w in ws]
    lane = lax.broadcasted_iota(I32, (out.shape[0], LANES), 1)
    low = lane < HEAD_DIM
    for p in range(ATT_HEADS // 2):
        acc = None
        for w, o in zip(ws, (o1, o2, o3)):
            wp = jnp.where(low, w[:, 2 * p:2 * p + 1], w[:, 2 * p + 1:2 * p + 2])
            t = wp * o[p]
            acc = t if acc is None else acc + t
        out[:, p * LANES:(p + 1) * LANES] = acc.astype(out.dtype)


def _att_merge(os_, ls_, tm=1024):
    npair, t, _ = os_[0].shape
    ospec = pl.BlockSpec((npair, tm, LANES), lambda i: (0, i, 0))
    lspec = pl.BlockSpec((tm, LANES), lambda i: (i, 0))
    return pl.pallas_call(
        _merge_body, grid=(t // tm,), in_specs=[ospec] * 3 + [lspec] * 3,
        out_specs=pl.BlockSpec((tm, ATT_WIDTH), lambda i: (i, 0)),
        out_shape=jax.ShapeDtypeStruct((t, ATT_WIDTH), BF16),
        compiler_params=_params("parallel"), name="att_merge")(*os_, *ls_)


def _split2(a):
    hi = a.astype(BF16)
    lo = (a - hi.astype(F32)).astype(BF16)
    return hi, lo


def _dot(a, b):
    return jnp.dot(a, b, preferred_element_type=F32)


def _dot_bf16(a, b):
    return _dot(a.astype(BF16), b.astype(BF16))


def _dot_hp(a, b):
    ah, al = _split2(a)
    bh, bl = _split2(b)
    return _dot(jnp.concatenate([ah, al, ah], axis=1), jnp.concatenate([bh, bh, bl], axis=0))


def _dot_t(a, b):
    return lax.dot_general(a, b, (((1,), (1,)), ((), ())), preferred_element_type=F32)


def _dot_t_hp(a, b):
    ah, al = _split2(a)
    bh, bl = _split2(b)
    return _dot_t(jnp.concatenate([ah, al, ah], axis=1), jnp.concatenate([bh, bh, bl], axis=1))


def _dn_body(hq, hk, hv, pq, pk, pv, small, gate, convw, pvec, normw, out, xbuf, qkv, s_ref, *, nch):
    n = pl.program_id(1)
    c_ = QB
    npair = DN_HEADS // 2

    @pl.when(n == 0)
    def _():
        s_ref[...] = jnp.zeros_like(s_ref)

    halo = 8
    has_prev = jnp.where(n > 0, 1.0, 0.0)
    for idx, (cur, prev) in enumerate(((hq, pq), (hk, pk), (hv, pv))):
        xbuf[idx, 0:halo, :] = prev[...] * has_prev
        xbuf[idx, halo:, :] = cur[...]
        acc = None
        for j in range(CONV_K):
            start = halo - (CONV_K - 1) + j
            t = xbuf[idx, start:start + nch * c_, :] * convw[j:j + 1, idx * DN_WIDTH:(idx + 1) * DN_WIDTH]
            acc = t if acc is None else acc + t
        qkv[idx] = acc * _sigmoid(acc)

    lane = lax.broadcasted_iota(I32, (c_, LANES), 1)
    low = lane < HEAD_DIM
    row = lax.broadcasted_iota(I32, (c_, c_), 0)
    col = lax.broadcasted_iota(I32, (c_, c_), 1)
    lower = row >= col
    strict = row > col
    eye = jnp.where(row == col, 1.0, 0.0)
    blockdiag = (row < HEAD_DIM) == (col < HEAD_DIM)

    def headsum(x):
        sa = jnp.sum(jnp.where(low, x, 0.0), axis=-1, keepdims=True)
        sb = jnp.sum(jnp.where(low, 0.0, x), axis=-1, keepdims=True)
        return jnp.where(low, sa, sb)

    def col_of(arr, j):
        return arr[:, j:j + 1]

    ltri = jnp.where(lower, 1.0, 0.0).astype(BF16)
    ltri3 = jnp.concatenate([ltri, ltri, ltri], axis=1)
    pair, xs, pws, qks = [], [], [], []
    for ch in range(nch):
        rs = slice(ch * c_, (ch + 1) * c_)
        sm = small[rs, :]
        beta_all = _sigmoid(sm)
        z = sm + pvec[1:2, :]
        softplus = jnp.maximum(z, 0.0) + jnp.log(1.0 + jnp.exp(-jnp.abs(z)))
        g_all = -jnp.exp(pvec[0:1, :]) * softplus
        g1 = g_all.astype(BF16)
        rem = g_all - g1.astype(F32)
        g2 = rem.astype(BF16)
        g3 = (rem - g2.astype(F32)).astype(BF16)
        gc_all = _dot(ltri3, jnp.concatenate([g1, g2, g3], axis=0))
        gc_t = gc_all.T

        for p in range(npair):
            cs = slice(p * LANES, (p + 1) * LANES)
            q, k, v = qkv[0, rs, cs], qkv[1, rs, cs], qkv[2, rs, cs]
            q = q * lax.rsqrt(headsum(q * q) + NORM_EPS) * (HEAD_DIM ** -0.5)
            k = k * lax.rsqrt(headsum(k * k) + NORM_EPS)
            beta = jnp.where(low, col_of(beta_all, 2 * p), col_of(beta_all, 2 * p + 1))
            gc = jnp.where(low, col_of(gc_all, DN_HEADS + 2 * p), col_of(gc_all, DN_HEADS + 2 * p + 1))
            eg = jnp.exp(gc)
            rhs = jnp.concatenate([v * beta, k * beta * eg], axis=1)
            pair.append(dict(q=q, k=k, kb=k.astype(BF16), gc=gc, eg=eg, rhs=rhs))

        for h in range(DN_HEADS):
            pr = pair[ch * npair + h // 2]
            sel = low if h % 2 == 0 else ~low
            gcol = col_of(gc_all, DN_HEADS + h)
            grow = gc_t[DN_HEADS + h:DN_HEADS + h + 1, :]
            dec = jnp.where(lower, jnp.exp(jnp.minimum(gcol - grow, 0.0)), 0.0)
            kk = _dot_t(jnp.where(sel, pr["k"], 0.0).astype(BF16), pr["kb"])
            qk = _dot_t_hp(jnp.where(sel, pr["q"], 0.0), pr["k"])
            nm = jnp.where(strict, col_of(beta_all, h) * kk * dec, 0.0)
            xs.append(eye - nm)
            pws.append(nm)
            qks.append((qk * dec).astype(BF16))

    for level in range(int(math.log2(c_)) - 1):
        mm = _dot_hp if level < 3 else _dot_bf16
        for i in range(len(xs)):
            pws[i] = mm(pws[i], pws[i])
        for i in range(len(xs)):
            xs[i] = xs[i] + mm(xs[i], pws[i])
    uws = [_dot_hp(xs[i], pair[i // 2]["rhs"]) for i in range(len(xs))]

    for ch in range(nch):
        rs = slice(ch * c_, (ch + 1) * c_)
        for p in range(npair):
            pr = pair[ch * npair + p]
            i0 = ch * DN_HEADS + 2 * p
            cs = slice(p * LANES, (p + 1) * LANES)
            u = jnp.where(low, uws[i0][:, :LANES], uws[i0 + 1][:, :LANES])
            w = jnp.where(low, uws[i0][:, LANES:], uws[i0 + 1][:, LANES:])
            state = s_ref[p]
            sb16 = state.astype(BF16)
            vnew = u - _dot(w.astype(BF16), sb16)
            vn16 = vnew.astype(BF16)
            o = _dot((pr["q"] * pr["eg"]).astype(BF16), sb16)
            o = o + jnp.where(low, _dot(qks[i0], vn16), _dot(qks[i0 + 1], vn16))
            glast = pr["gc"][c_ - 1:c_, :]
            kd = pr["k"] * jnp.exp(glast - pr["gc"])
            upd = _dot(kd.T.astype(BF16), vn16)
            s_ref[p] = state * jnp.exp(glast) + jnp.where(blockdiag, upd, 0.0)
            ms = headsum(o * o) * (1.0 / HEAD_DIM)
            gt = gate[rs, cs]
            out[rs, cs] = (o * lax.rsqrt(ms + NORM_EPS) * normw[:, cs] * (gt * _sigmoid(gt))).astype(out.dtype)


def _delta_net(hr, convw, pvec, normw, batch, seq, nch=2):
    h3 = hr.reshape(batch, seq, R_COLS)
    rows = nch * QB
    qcol = R_DN // DN_WIDTH

    def cur(off):
        return pl.BlockSpec((None, rows, DN_WIDTH), lambda b, n: (b, n, qcol + off))

    def prev(off):
        return pl.BlockSpec((None, 8, DN_WIDTH),
                            lambda b, n: (b, jnp.maximum(n * (rows // 8) - 1, 0), qcol + off))

    out = pl.pallas_call(
        functools.partial(_dn_body, nch=nch), grid=(batch, seq // rows),
        in_specs=[cur(0), cur(1), cur(2), prev(0), prev(1), prev(2),
                  pl.BlockSpec((None, rows, LANES), lambda b, n: (b, n, R_SMALL // LANES)),
                  pl.BlockSpec((None, rows, DN_WIDTH), lambda b, n: (b, n, R_GATE // DN_WIDTH)),
                  pl.BlockSpec((CONV_K, 3 * DN_WIDTH), lambda b, n: (0, 0)),
                  pl.BlockSpec((2, LANES), lambda b, n: (0, 0)),
                  pl.BlockSpec((1, DN_WIDTH), lambda b, n: (0, 0))],
        out_specs=pl.BlockSpec((None, rows, DN_WIDTH), lambda b, n: (b, n, 0)),
        out_shape=jax.ShapeDtypeStruct((batch, seq, DN_WIDTH), BF16),
        scratch_shapes=[pltpu.VMEM((3, rows + 8, DN_WIDTH), F32),
                        pltpu.VMEM((3, rows, DN_WIDTH), F32),
                        pltpu.VMEM((DN_HEADS // 2, LANES, LANES), F32)],
        compiler_params=_params("parallel", "arbitrary"),
        name="gated_delta_rule")(h3, h3, h3, h3, h3, h3, h3, h3, convw, pvec, normw)
    return out.reshape(batch * seq, DN_WIDTH)


def _pool_body(cur, prev, wbd, scale, out, xbuf, *, rows):
    n = pl.program_id(1)
    halo = 16
    xbuf[0:halo, :] = prev[...] * jnp.where(n > 0, 1.0, 0.0)
    xbuf[halo:, :] = cur[...]
    u = cur[...]
    lane = lax.broadcasted_iota(I32, (rows, POOL_WIDTH), 1)
    tpos = n * rows + lax.broadcasted_iota(I32, (rows, POOL_WIDTH), 0)
    run = u
    pooled = jnp.zeros_like(u)
    for j in range(1, max(POOL_WINDOWS)):
        run = run + xbuf[halo - j:halo - j + rows, :]
        if j + 1 in POOL_WINDOWS:
            gi = POOL_WINDOWS.index(j + 1)
            cnt = jnp.minimum(tpos + 1, j + 1).astype(F32)
            sel = (lane >= gi * POOL_GDIM) & (lane < (gi + 1) * POOL_GDIM)
            pooled = jnp.where(sel, run / cnt - u, pooled)
    mixed = jnp.dot(pooled.astype(BF16), wbd[...], preferred_element_type=F32)
    out[...] = (mixed * scale[...]).astype(out.dtype)


def _pool(hr, wbd, scale, batch, seq, rows=512):
    h3 = hr.reshape(batch, seq, R_COLS)
    out = pl.pallas_call(
        functools.partial(_pool_body, rows=rows), grid=(batch, seq // rows),
        in_specs=[pl.BlockSpec((None, rows, POOL_WIDTH), lambda b, n: (b, n, R_POOL // POOL_WIDTH)),
                  pl.BlockSpec((None, 16, POOL_WIDTH),
                               lambda b, n: (b, jnp.maximum(n * (rows // 16) - 1, 0), R_POOL // POOL_WIDTH)),
                  pl.BlockSpec((POOL_WIDTH, POOL_WIDTH), lambda b, n: (0, 0)),
                  pl.BlockSpec((1, POOL_WIDTH), lambda b, n: (0, 0))],
        out_specs=pl.BlockSpec((None, rows, POOL_WIDTH), lambda b, n: (b, n, 0)),
        out_shape=jax.ShapeDtypeStruct((batch, seq, POOL_WIDTH), BF16),
        scratch_shapes=[pltpu.VMEM((rows + 16, POOL_WIDTH), F32)],
        compiler_params=_params("parallel", "arbitrary"), name="multiscale_pool")(h3, h3, wbd, scale)
    return out.reshape(batch * seq, POOL_WIDTH)


def _proj_ln_body(a, d, p, x, wa, wd, wp, g, b, o):
    y = jnp.dot(a[...], wa[...], preferred_element_type=F32)
    y = y + jnp.dot(d[...], wd[...], preferred_element_type=F32)
    y = y + jnp.dot(p[...], wp[...], preferred_element_type=F32)
    o[...] = _layer_norm_rows(ALPHA * x[...] + y, g[...], b[...])


def _proj_ln(att, dn, pool, x, wa, wd, wp, g, b, tm=512):
    t = x.shape[0]

    def rows(wd_):
        return pl.BlockSpec((tm, wd_), lambda i: (i, 0))

    def full(a_):
        return pl.BlockSpec(a_.shape, lambda i: (0, 0))

    return pl.pallas_call(
        _proj_ln_body, grid=(t // tm,),
        in_specs=[rows(ATT_WIDTH), rows(DN_WIDTH), rows(POOL_WIDTH), rows(D_MODEL),
                  full(wa), full(wd), full(wp), full(g), full(b)],
        out_specs=rows(D_MODEL), out_shape=jax.ShapeDtypeStruct((t, D_MODEL), F32),
        compiler_params=_params("parallel"), name="out_proj_ln")(att, dn, pool, x, wa, wd, wp, g, b)


def _xattn_body(x, wq, k, v, wo, g, b, o):
    xv = x[...]
    q = jnp.dot(xv.astype(BF16), wq[...], preferred_element_type=F32).astype(BF16)
    heads = []
    for h in range(X_HEADS):
        cs = slice(h * X_HEAD_DIM, (h + 1) * X_HEAD_DIM)
        s = lax.dot_general(q[:, cs], k[:, cs], (((1,), (1,)), ((), ())), preferred_element_type=F32)
        m = jnp.max(s, axis=-1, keepdims=True)
        e = jnp.exp(s - m)
        l = jnp.sum(e, axis=-1, keepdims=True)
        oh = jnp.dot(e.astype(BF16), v[:, cs], preferred_element_type=F32) / l
        heads.append(oh.astype(BF16))
    oc = jnp.concatenate(heads, axis=1)
    y = jnp.dot(oc, wo[...], preferred_element_type=F32)
    o[...] = _layer_norm_rows(ALPHA * xv + y, g[...], b[...])


def _cross_attention(x, wq, k, v, wo, g, b, seq, tm=512):
    t = x.shape[0]
    mem_len = k.shape[1]
    per_batch = seq // tm

    def full(a_):
        return pl.BlockSpec(a_.shape, lambda i: (0, 0))

    kv = pl.BlockSpec((None, mem_len, D_MODEL), lambda i: (i // per_batch, 0, 0))
    rows = pl.BlockSpec((tm, D_MODEL), lambda i: (i, 0))
    return pl.pallas_call(
        _xattn_body, grid=(t // tm,),
        in_specs=[rows, full(wq), kv, kv, full(wo), full(g), full(b)],
        out_specs=rows, out_shape=jax.ShapeDtypeStruct((t, D_MODEL), F32),
        compiler_params=_params("parallel"), name="cross_attention_ln")(x, wq, k, v, wo, g, b)


def _ffn_body(x, w1, w3, w2, g, b, o, xb, acc):
    f = pl.program_id(1)

    @pl.when(f == 0)
    def _():
        xb[...] = x[...].astype(BF16)
        acc[...] = jnp.zeros_like(acc)

    xv = xb[...]
    h1 = jnp.dot(xv, w1[...], preferred_element_type=F32)
    h3 = jnp.dot(xv, w3[...], preferred_element_type=F32)
    hh = (h1 * _sigmoid(h1) * h3).astype(BF16)
    acc[...] += jnp.dot(hh, w2[...], preferred_element_type=F32)

    @pl.when(f == pl.num_programs(1) - 1)
    def _():
        o[...] = _layer_norm_rows(ALPHA * x[...] + acc[...], g[...], b[...])


def _ffn_dense(x, w1, w3, w2, g, b, tm=1024, tf=256):
    t = x.shape[0]
    dff = w1.shape[1]
    rows = pl.BlockSpec((tm, D_MODEL), lambda i, f: (i, 0))
    vec = pl.BlockSpec((1, D_MODEL), lambda i, f: (0, 0))
    return pl.pallas_call(
        _ffn_body, grid=(t // tm, dff // tf),
        in_specs=[rows, pl.BlockSpec((D_MODEL, tf), lambda i, f: (0, f)),
                  pl.BlockSpec((D_MODEL, tf), lambda i, f: (0, f)),
                  pl.BlockSpec((tf, D_MODEL), lambda i, f: (f, 0)), vec, vec],
        out_specs=rows, out_shape=jax.ShapeDtypeStruct((t, D_MODEL), F32),
        scratch_shapes=[pltpu.VMEM((tm, D_MODEL), BF16), pltpu.VMEM((tm, D_MODEL), F32)],
        compiler_params=_params("parallel", "arbitrary"), name="ffn_swiglu_ln")(x, w1, w3, w2, g, b)


def _router_body(x, rw, eidx, gates, ranks, counts, carry, *, tm):
    i = pl.program_id(0)

    @pl.when(i == 0)
    def _():
        carry[...] = jnp.zeros_like(carry)

    logits = jnp.dot(x[...], rw[...], preferred_element_type=F32, precision=lax.Precision.HIGHEST)
    lane = lax.broadcasted_iota(I32, (tm, LANES), 1)
    lanef = lane.astype(F32)
    lg = jnp.where(lane < N_EXPERTS, logits, NEG)
    m1 = jnp.max(lg, axis=-1, keepdims=True)
    i1 = jnp.min(jnp.where(lg == m1, lanef, float(LANES)), axis=-1, keepdims=True)
    lg2 = jnp.where(lanef == i1, NEG, lg)
    m2 = jnp.max(lg2, axis=-1, keepdims=True)
    i2 = jnp.min(jnp.where(lg2 == m2, lanef, float(LANES)), axis=-1, keepdims=True)
    e21 = jnp.exp(m2 - m1)
    g1 = 1.0 / (1.0 + e21)
    g2 = e21 * g1
    hot1 = lanef == i1
    hot2 = lanef == i2
    onehot = jnp.where(hot1 | hot2, 1.0, 0.0)
    r = lax.broadcasted_iota(I32, (tm, tm), 0)
    c = lax.broadcasted_iota(I32, (tm, tm), 1)
    before = jnp.dot(jnp.where(r > c, 1.0, 0.0).astype(BF16), onehot.astype(BF16),
                     preferred_element_type=F32) + carry[...]
    rank1 = jnp.sum(jnp.where(hot1, before, 0.0), axis=-1, keepdims=True)
    rank2 = jnp.sum(jnp.where(hot2, before, 0.0), axis=-1, keepdims=True)
    carry[...] = carry[...] + jnp.sum(onehot, axis=0, keepdims=True)
    eidx[...] = jnp.where(lane == 0, i1, jnp.where(lane == 1, i2, 0.0)).astype(I32)
    gates[...] = jnp.where(lane == 0, g1, jnp.where(lane == 1, g2, 0.0))
    ranks[...] = jnp.where(lane == 0, rank1, jnp.where(lane == 1, rank2, 0.0)).astype(I32)
    counts[...] = jnp.broadcast_to(carry[...], counts.shape).astype(I32)


def _router(x, rw, tm=512):
    t = x.shape[0]
    rows = pl.BlockSpec((tm, LANES), lambda i: (i, 0))
    return pl.pallas_call(
        functools.partial(_router_body, tm=tm), grid=(t // tm,),
        in_specs=[pl.BlockSpec((tm, D_MODEL), lambda i: (i, 0)),
                  pl.BlockSpec((D_MODEL, LANES), lambda i: (0, 0))],
        out_specs=[rows, rows, rows, pl.BlockSpec((8, LANES), lambda i: (0, 0))],
        out_shape=[jax.ShapeDtypeStruct((t, LANES), I32), jax.ShapeDtypeStruct((t, LANES), F32),
                   jax.ShapeDtypeStruct((t, LANES), I32), jax.ShapeDtypeStruct((8, LANES), I32)],
        scratch_shapes=[pltpu.VMEM((1, LANES), F32)],
        compiler_params=_params("arbitrary"), name="router_top2")(x, rw)


def _moe_body(be, nused, xs, w1, w3, w2, ys, xb, acc):
    i = pl.program_id(0)
    f = pl.program_id(1)
    last = f == pl.num_programs(1) - 1
    used = i < nused[0]

    @pl.when(used & (f == 0))
    def _():
        xb[...] = xs[...].astype(BF16)
        acc[...] = jnp.zeros_like(acc)

    @pl.when(used)
    def _():
        xv = xb[...]
        h1 = jnp.dot(xv, w1[...], preferred_element_type=F32)
        h3 = jnp.dot(xv, w3[...], preferred_element_type=F32)
        hh = (h1 * _sigmoid(h1) * h3).astype(BF16)
        acc[...] += jnp.dot(hh, w2[...], preferred_element_type=F32)

    @pl.when(used & last)
    def _():
        ys[...] = acc[...]

    @pl.when(jnp.logical_not(used) & last)
    def _():
        ys[...] = jnp.zeros_like(ys)


def _moe_experts(block_e, n_used, xs, w1, w3, w2, tm, tf=896):
    n_slots = xs.shape[0]
    dex = w1.shape[2]
    nblk = n_slots // tm

    def live(i, nu):
        return jnp.minimum(i, nu[0] - 1)

    grid_spec = pltpu.PrefetchScalarGridSpec(
        num_scalar_prefetch=2, grid=(nblk, dex // tf),
        in_specs=[pl.BlockSpec((tm, D_MODEL), lambda i, f, be, nu: (live(i, nu), 0)),
                  pl.BlockSpec((None, D_MODEL, tf), lambda i, f, be, nu: (be[i], 0, jnp.where(i < nu[0], f, dex // tf - 1))),
                  pl.BlockSpec((None, D_MODEL, tf), lambda i, f, be, nu: (be[i], 0, jnp.where(i < nu[0], f, dex // tf - 1))),
                  pl.BlockSpec((None, tf, D_MODEL), lambda i, f, be, nu: (be[i], jnp.where(i < nu[0], f, dex // tf - 1), 0))],
        out_specs=pl.BlockSpec((tm, D_MODEL), lambda i, f, be, nu: (i, 0)),
        scratch_shapes=[pltpu.VMEM((tm, D_MODEL), BF16), pltpu.VMEM((tm, D_MODEL), F32)])
    return pl.pallas_call(
        _moe_body, grid_spec=grid_spec, out_shape=jax.ShapeDtypeStruct((n_slots, D_MODEL), F32),
        compiler_params=_params("arbitrary", "arbitrary"), name="moe_experts")(block_e, n_used, xs, w1, w3, w2)


SUBLANES = 8


def _tile_row(ref3, idx):
    return ref3.at[lax.shift_right_logical(idx, 3), pl.ds(idx & (SUBLANES - 1), 1), :]


def _start_row_gather(src, ids, buf, sem, slot, nrows):
    def issue(j, carry):
        for u in range(SUBLANES):
            idx = ids[0, j * SUBLANES + u]
            pltpu.make_async_copy(_tile_row(src, idx), buf.at[slot, j, pl.ds(u, 1), :], sem.at[slot]).start()
        return carry

    lax.fori_loop(0, nrows // SUBLANES, issue, 0)


def _wait_row_gather(src, buf, sem, slot, nrows):
    pltpu.make_async_copy(src.at[pl.ds(0, nrows // SUBLANES)], buf.at[slot], sem.at[slot]).wait()


def _pipelined_row_gather(src, ids_cur, ids_next, buf, sem, nrows):
    i = pl.program_id(0)
    slot = lax.rem(i, 2)

    @pl.when(i == 0)
    def _():
        _start_row_gather(src, ids_cur, buf, sem, 0, nrows)

    @pl.when(i + 1 < pl.num_programs(0))
    def _():
        _start_row_gather(src, ids_next, buf, sem, 1 - slot, nrows)

    _wait_row_gather(src, buf, sem, slot, nrows)
    return slot


def _dispatch_body(ids, x, init, xs, sem, *, tm):
    del init

    def issue(j, carry):
        for u in range(SUBLANES):
            for k in range(2):
                dst = _tile_row(xs, ids[0, k * tm + j * SUBLANES + u])
                pltpu.make_async_copy(x.at[j, pl.ds(u, 1), :], dst, sem).start()
        return carry

    lax.fori_loop(0, tm // SUBLANES, issue, 0)
    for _ in range(2):
        pltpu.make_async_copy(x, xs.at[pl.ds(0, tm // SUBLANES)], sem).wait()


def _moe_dispatch(x, ids, n_slots, tm):
    t = x.shape[0]
    xs = pl.pallas_call(
        functools.partial(_dispatch_body, tm=tm), grid=(t // tm,),
        in_specs=[pl.BlockSpec((None, 1, 2 * tm), lambda i: (i, 0, 0), memory_space=pltpu.SMEM),
                  pl.BlockSpec((tm // SUBLANES, SUBLANES, D_MODEL), lambda i: (i, 0, 0)),
                  pl.BlockSpec(memory_space=pl.ANY)],
        out_specs=pl.BlockSpec(memory_space=pl.ANY),
        out_shape=jax.ShapeDtypeStruct((n_slots // SUBLANES, SUBLANES, D_MODEL), F32),
        scratch_shapes=[pltpu.SemaphoreType.DMA(())],
        input_output_aliases={2: 0},
        compiler_params=_params("arbitrary"),
        name="moe_dispatch_scatter")(ids, x.reshape(t // SUBLANES, SUBLANES, D_MODEL),
                                     jnp.zeros((n_slots // SUBLANES, SUBLANES, D_MODEL), F32))
    return xs.reshape(n_slots, D_MODEL)


def _combine_ln_body(ids_cur, ids_next, x, ys, gates, g, b, o, buf, sem, *, tm):
    slot = _pipelined_row_gather(ys, ids_cur, ids_next, buf, sem, 2 * tm)
    gt = gates[...]
    nt = tm // SUBLANES
    ya = buf[slot, 0:nt].reshape(tm, D_MODEL)
    yb = buf[slot, nt:].reshape(tm, D_MODEL)
    o[...] = _layer_norm_rows(ALPHA * x[...] + (ya * gt[:, 0:1] + yb * gt[:, 1:2]), g[...], b[...])


def _combine_ln(x, ys, ids, gates, g, b, tm):
    t = x.shape[0]
    nt = t // tm
    smem = functools.partial(pl.BlockSpec, (None, 1, 2 * tm), memory_space=pltpu.SMEM)
    rows = pl.BlockSpec((tm, D_MODEL), lambda i: (i, 0))
    vec = pl.BlockSpec((1, D_MODEL), lambda i: (0, 0))
    return pl.pallas_call(
        functools.partial(_combine_ln_body, tm=tm), grid=(nt,),
        in_specs=[smem(lambda i: (i, 0, 0)), smem(lambda i: (jnp.minimum(i + 1, nt - 1), 0, 0)),
                  rows, pl.BlockSpec(memory_space=pl.ANY),
                  pl.BlockSpec((tm, LANES), lambda i: (i, 0)), vec, vec],
        out_specs=rows, out_shape=jax.ShapeDtypeStruct((t, D_MODEL), F32),
        scratch_shapes=[pltpu.VMEM((2, 2 * tm // SUBLANES, SUBLANES, D_MODEL), F32),
                        pltpu.SemaphoreType.DMA((2,))],
        compiler_params=_params("arbitrary"),
        name="moe_combine_ln")(ids, ids, x, ys.reshape(-1, SUBLANES, D_MODEL), gates, g, b)


def _moe_layer(x, rw_pad, w1, w3, w2, g, b, tm=512):
    t = x.shape[0]
    eidx, gates, ranks, counts = _router(x, rw_pad)
    cnt = counts[0, :N_EXPERTS]
    padded = (cnt + tm - 1) // tm * tm
    pad_end = jnp.cumsum(padded)
    pad_start = pad_end - padded
    e2 = eidx[:, :2]
    dest = pad_start[e2] + ranks[:, :2]
    nblk = (2 * t) // tm + N_EXPERTS
    n_slots = nblk * tm
    tt = 512
    ids = dest.reshape(t // tt, tt, 2).transpose(0, 2, 1).reshape(t // tt, 1, 2 * tt)
    n_used = (pad_end[-1] // tm).astype(I32).reshape(1)
    blk = jnp.arange(nblk, dtype=I32) * tm
    block_e = jnp.minimum(jnp.sum(pad_end[None, :] <= blk[:, None], axis=1), N_EXPERTS - 1).astype(I32)
    block_e = jnp.where(jnp.arange(nblk) < n_used[0], block_e, block_e[jnp.maximum(n_used[0] - 1, 0)])
    xs = _moe_dispatch(x, ids, n_slots, tt)
    ys = _moe_experts(block_e, n_used, xs, w1, w3, w2, tm)
    return _combine_ln(x, ys, ids, gates, g, b, tt)


def _block_diag(w):
    g, c, _ = w.shape
    eye = jnp.eye(g, dtype=w.dtype)
    return (eye[:, None, :, None] * w[:, :, None, :]).reshape(g * c, g * c)


def kernel(x, mem, positions, w_in, conv_w, a_log, dt_bias, dn_norm_w, pool_w, pool_scale, w_out, ln_mix_g, ln_mix_b, xq_w, xk_w, xv_w, xo_w, ln_x_g, ln_x_b, ffn_w1, ffn_w3, ffn_w2, router_w, moe_w1, moe_w3, moe_w2, ln_ffn_g, ln_ffn_b):
    batch, seq, _ = x.shape
    t = batch * seq
    depth = w_in.shape[0]
    xf = x.reshape(t, D_MODEL)
    memf = mem.reshape(batch * mem.shape[1], D_MODEL)

    posc, posr = {}, {}
    for dil in DILATIONS:
        pt = positions.reshape(batch, seq // dil, dil).transpose(0, 2, 1)
        posc[dil] = pt[..., None]
        posr[dil] = pt[:, :, None, :]

    for l in range(depth):
        wl = w_in[l]
        small = jnp.pad(wl[:, OFF_BETA:OFF_GATE], ((0, 0), (0, LANES - 2 * DN_HEADS)))
        w_rest = jnp.concatenate([wl[:, OFF_DN:OFF_BETA], wl[:, OFF_GATE:OFF_POOL], wl[:, OFF_POOL:], small],
                                 axis=1).astype(BF16)
        slabs = _matmul_slabs(xf, wl[:, :OFF_DN].astype(BF16), 512)
        hr = _matmul(xf, w_rest, F32, 1024, R_COLS // 3)

        branches = [_att_branch(slabs, posc[d], posr[d], d, batch, seq) for d in DILATIONS]
        att = _att_merge([o for o, _ in branches], [s for _, s in branches])

        pvec = jnp.zeros((2, LANES), F32)
        pvec = pvec.at[0, DN_HEADS:2 * DN_HEADS].set(a_log[l]).at[1, DN_HEADS:2 * DN_HEADS].set(dt_bias[l])
        normw = jnp.tile(dn_norm_w[l], DN_HEADS)[None, :]
        dn = _delta_net(hr, conv_w[l], pvec, normw, batch, seq)

        pool = _pool(hr, _block_diag(pool_w[l]).astype(BF16), pool_scale[l][None, :], batch, seq)

        wo = w_out[l].astype(BF16)
        xf = _proj_ln(att, dn, pool, xf, wo[:ATT_WIDTH], wo[ATT_WIDTH:ATT_WIDTH + DN_WIDTH],
                      wo[ATT_WIDTH + DN_WIDTH:], ln_mix_g[l][None, :], ln_mix_b[l][None, :])

        wkv = jnp.concatenate([xk_w[l], xv_w[l]], axis=1).astype(BF16)
        kv = _matmul(memf, wkv, BF16, 512, 1024).reshape(batch, mem.shape[1], 2 * D_MODEL)
        xf = _cross_attention(xf, (xq_w[l] * (X_HEAD_DIM ** -0.5)).astype(BF16), kv[..., :D_MODEL],
                              kv[..., D_MODEL:], xo_w[l].astype(BF16), ln_x_g[l][None, :],
                              ln_x_b[l][None, :], seq)

        j = l // 2
        gl, bl = ln_ffn_g[l][None, :], ln_ffn_b[l][None, :]
        if l % 2 == 0:
            xf = _ffn_dense(xf, ffn_w1[j].astype(BF16), ffn_w3[j].astype(BF16), ffn_w2[j].astype(BF16), gl, bl)
        else:
            rw_pad = jnp.pad(router_w[j], ((0, 0), (0, LANES - N_EXPERTS)))
            xf = _moe_layer(xf, rw_pad, moe_w1[j].astype(BF16), moe_w3[j].astype(BF16),
                            moe_w2[j].astype(BF16), gl, bl)
    return xf.reshape(batch, seq, D_MODEL)
```

```python
import functools
import math

import jax
import jax.numpy as jnp
from jax import lax
from jax.experimental import pallas as pl
from jax.experimental.pallas import tpu as pltpu

F32, BF16, I32 = jnp.float32, jnp.bfloat16, jnp.int32

D_MODEL = 1024
HEAD_DIM = 64
ATT_HEADS = 6
ATT_WIDTH = ATT_HEADS * HEAD_DIM
DN_HEADS = 6
DN_WIDTH = DN_HEADS * HEAD_DIM
POOL_WIDTH = 256
POOL_WINDOWS = (2, 4, 8, 16)
POOL_GDIM = POOL_WIDTH // len(POOL_WINDOWS)
DILATIONS = (1, 4, 16)
CONV_K = 4
X_HEADS = 4
X_HEAD_DIM = D_MODEL // X_HEADS
N_EXPERTS = 8
DEPTH = 4
ALPHA = (2 * DEPTH) ** 0.25
LN_EPS = 1e-5
NORM_EPS = 1e-6

OFF_DN = 3 * ATT_WIDTH
OFF_BETA = OFF_DN + 3 * DN_WIDTH
OFF_DECAY = OFF_BETA + DN_HEADS
OFF_GATE = OFF_DECAY + DN_HEADS
OFF_POOL = OFF_GATE + DN_WIDTH

LANES = 128
R_DN = 0
R_GATE = R_DN + 3 * DN_WIDTH
R_POOL = R_GATE + DN_WIDTH
R_SMALL = R_POOL + POOL_WIDTH
R_COLS = R_SMALL + LANES

QB = 128
NEG = -1e30
VMEM_LIMIT = 48 * 1024 * 1024


def _params(*sem):
    return pltpu.CompilerParams(dimension_semantics=sem, vmem_limit_bytes=VMEM_LIMIT)


def _sigmoid(x):
    return 1.0 / (1.0 + jnp.exp(-x))


def _layer_norm_rows(z, g, b):
    mu = jnp.mean(z, axis=-1, keepdims=True)
    zc = z - mu
    var = jnp.mean(zc * zc, axis=-1, keepdims=True)
    return zc * lax.rsqrt(var + LN_EPS) * g + b


def _rows(start, size, stride):
    return pl.ds(start, size) if stride == 1 else pl.ds(start, size, stride=stride)


def _mm_body(x_ref, w_ref, o_ref):
    o_ref[...] = jnp.dot(x_ref[...].astype(BF16), w_ref[...],
                         preferred_element_type=F32).astype(o_ref.dtype)


def _matmul(x, w, out_dtype, tm, tn):
    m, k = x.shape
    n = w.shape[1]
    return pl.pallas_call(
        _mm_body, grid=(m // tm, n // tn),
        in_specs=[pl.BlockSpec((tm, k), lambda i, j: (i, 0)),
                  pl.BlockSpec((k, tn), lambda i, j: (0, j))],
        out_specs=pl.BlockSpec((tm, tn), lambda i, j: (i, j)),
        out_shape=jax.ShapeDtypeStruct((m, n), out_dtype),
        compiler_params=_params("parallel", "arbitrary"), name="matmul")(x, w)


CONV_HALO = 8


def _mm_conv_body(x_ref, w_ref, cw_ref, o_ref, hist, cbuf, *, tm, ncv, tiles_per_seq):
    i = pl.program_id(0)
    res = jnp.dot(x_ref[...].astype(BF16), w_ref[...], preferred_element_type=F32)
    o_ref[:, ncv:] = res[:, ncv:]

    @pl.when(lax.rem(i, tiles_per_seq) == 0)
    def _():
        hist[...] = jnp.zeros_like(hist)

    cbuf[0:CONV_HALO, :] = hist[...]
    cbuf[CONV_HALO:, :] = res[:, :ncv]
    hist[...] = cbuf[tm:, :]
    acc = None
    for j in range(CONV_K):
        start = CONV_HALO - (CONV_K - 1) + j
        t = cbuf[start:start + tm, :] * cw_ref[j:j + 1, :]
        acc = t if acc is None else acc + t
    o_ref[:, :ncv] = acc * _sigmoid(acc)


def _matmul_conv(x, w, conv_w, seq, tm):
    m, k = x.shape
    n = w.shape[1]
    ncv = conv_w.shape[1]
    return pl.pallas_call(
        functools.partial(_mm_conv_body, tm=tm, ncv=ncv, tiles_per_seq=seq // tm), grid=(m // tm,),
        in_specs=[pl.BlockSpec((tm, k), lambda i: (i, 0)), pl.BlockSpec((k, n), lambda i: (0, 0)),
                  pl.BlockSpec((CONV_K, ncv), lambda i: (0, 0))],
        out_specs=pl.BlockSpec((tm, n), lambda i: (i, 0)),
        out_shape=jax.ShapeDtypeStruct((m, n), F32),
        scratch_shapes=[pltpu.VMEM((CONV_HALO, ncv), F32), pltpu.VMEM((tm + CONV_HALO, ncv), F32)],
        compiler_params=_params("arbitrary"), name="matmul_conv")(x, w, conv_w)


def _mm_slab_body(x_ref, w_ref, o_ref):
    res = jnp.dot(x_ref[...].astype(BF16), w_ref[...], preferred_element_type=F32)
    for j in range(o_ref.shape[0]):
        o_ref[j] = res[:, j * LANES:(j + 1) * LANES]


def _matmul_slabs(x, w, tm):
    m, k = x.shape
    n = w.shape[1]
    return pl.pallas_call(
        _mm_slab_body, grid=(m // tm,),
        in_specs=[pl.BlockSpec((tm, k), lambda i: (i, 0)), pl.BlockSpec((k, n), lambda i: (0, 0))],
        out_specs=pl.BlockSpec((n // LANES, tm, LANES), lambda i: (0, i, 0)),
        out_shape=jax.ShapeDtypeStruct((n // LANES, m, LANES), F32),
        compiler_params=_params("parallel"), name="matmul_slabs")(x, w)


def _att_body(q_ref, kc_ref, vc_ref, pq_ref, pkc_ref, o_ref, lse_ref,
              kbuf, vbuf, pkbuf, kprev, vprev, pkprev, *, dil, rows, slopes):
    n = pl.program_id(1)
    nsub = rows // QB
    npair = ATT_HEADS // 2

    @pl.when(n == 0)
    def _():
        kprev[...] = jnp.zeros_like(kprev)
        vprev[...] = jnp.zeros_like(vprev)
        pkprev[...] = jnp.zeros_like(pkprev)

    a = lax.broadcasted_iota(I32, (QB, 2 * QB), 0)
    c = lax.broadcasted_iota(I32, (QB, 2 * QB), 1)
    tri = jnp.where(c < QB, c - a, a - (c - QB)) >= 0
    lane = lax.broadcasted_iota(I32, (QB, LANES), 1)
    low = lane < HEAD_DIM

    def residue(r, carry):
        for p in range(npair):
            kbuf[p, 0:QB, :] = kprev[r, p]
            kbuf[p, QB:, :] = kc_ref[p, _rows(r, rows, dil), :].astype(BF16)
            vbuf[p, 0:QB, :] = vprev[r, p]
            vbuf[p, QB:, :] = vc_ref[p, _rows(r, rows, dil), :].astype(BF16)
            kprev[r, p] = kbuf[p, rows:, :]
            vprev[r, p] = vbuf[p, rows:, :]
        pkbuf[0] = pkprev[r]
        for j in range(nsub):
            pkbuf[j + 1] = pkc_ref[r, :, j * QB:(j + 1) * QB]
        pkprev[r] = pkbuf[nsub]

        def sub(sb, carry2):
            r0 = pl.multiple_of(sb * QB, QB)
            t0 = r + r0 * dil
            pq = pq_ref[r, pl.ds(r0, QB), :]
            pk = jnp.concatenate([pkbuf[sb], pkbuf[sb + 1]], axis=1)
            dist = (pq - pk).astype(F32)
            mask = tri & (c >= jnp.where((sb > 0) | (n > 0), 0, QB))
            heads = range(ATT_HEADS)
            ks = [kbuf[p, pl.ds(r0, 2 * QB), :] for p in range(npair)]
            vs = [vbuf[p, pl.ds(r0, 2 * QB), :] for p in range(npair)]
            qs = [q_ref[p, _rows(t0, QB, dil), :] * (HEAD_DIM ** -0.5) for p in range(npair)]
            qms = [jnp.where(low if h % 2 == 0 else ~low, qs[h // 2], 0.0).astype(BF16) for h in heads]
            ss = [lax.dot_general(qms[h], ks[h // 2], (((1,), (1,)), ((), ())), preferred_element_type=F32)
                  for h in heads]
            ss = [jnp.where(mask, ss[h] - slopes[h] * dist, NEG) for h in heads]
            ms = [jnp.max(ss[h], axis=-1, keepdims=True) for h in heads]
            es = [jnp.exp(ss[h] - ms[h]) for h in heads]
            ls = [jnp.sum(es[h], axis=-1, keepdims=True) for h in heads]
            outs = [jnp.dot(es[h].astype(BF16), vs[h // 2], preferred_element_type=F32) / ls[h] for h in heads]
            lse = jnp.zeros((QB, LANES), F32)
            for h in heads:
                lse = jnp.where(lane == h, ms[h] + jnp.log(ls[h]), lse)
            for p in range(npair):
                o_ref[p, _rows(t0, QB, dil), :] = jnp.where(low, outs[2 * p], outs[2 * p + 1])
            lse_ref[_rows(t0, QB, dil), :] = lse
            return carry2

        lax.fori_loop(0, nsub, sub, 0)
        return carry

    lax.fori_loop(0, dil, residue, 0)


def _att_branch(slabs, posc, posr, dil, batch, seq):
    sub_len = seq // dil
    rows = min(512 if dil == 1 else 2048 // dil, sub_len)
    tok = rows * dil
    nsteps = seq // tok
    npair = ATT_HEADS // 2
    slopes = tuple(2.0 ** (-8.0 * (i + 1) / ATT_HEADS) for i in range(ATT_HEADS))

    def cur(which):
        return pl.BlockSpec((npair, tok, LANES), lambda b, n: (which, b * nsteps + n, 0))

    t = batch * seq
    return pl.pallas_call(
        functools.partial(_att_body, dil=dil, rows=rows, slopes=slopes),
        grid=(batch, nsteps),
        in_specs=[cur(0), cur(1), cur(2),
                  pl.BlockSpec((None, dil, rows, 1), lambda b, n: (b, 0, n, 0)),
                  pl.BlockSpec((None, dil, 1, rows), lambda b, n: (b, 0, 0, n))],
        out_specs=[pl.BlockSpec((npair, tok, LANES), lambda b, n: (0, b * nsteps + n, 0)),
                   pl.BlockSpec((tok, LANES), lambda b, n: (b * nsteps + n, 0))],
        out_shape=[jax.ShapeDtypeStruct((npair, t, LANES), F32),
                   jax.ShapeDtypeStruct((t, LANES), F32)],
        scratch_shapes=[pltpu.VMEM((npair, rows + QB, LANES), BF16),
                        pltpu.VMEM((npair, rows + QB, LANES), BF16),
                        pltpu.VMEM((rows // QB + 1, 1, QB), I32),
                        pltpu.VMEM((dil, npair, QB, LANES), BF16),
                        pltpu.VMEM((dil, npair, QB, LANES), BF16),
                        pltpu.VMEM((dil, 1, QB), I32)],
        compiler_params=_params("arbitrary", "arbitrary"),
        name=f"dilated_att_d{dil}")(slabs, slabs, slabs, posc, posr)


def _merge_branches(os_, ls_):
    ls = [l[...] for l in ls_]
    m = functools.reduce(jnp.maximum, ls)
    ws = [jnp.exp(l - m) for l in ls]
    inv = 1.0 / functools.reduce(lambda a, b: a + b, ws)
    ws = [w * inv for w in ws]
    lane = lax.broadcasted_iota(I32, ls[0].shape, 1)
    low = lane < HEAD_DIM
    pairs = []
    for p in range(ATT_HEADS // 2):
        acc = None
        for w, o in zip(ws, os_):
            wp = jnp.where(low, w[:, 2 * p:2 * p + 1], w[:, 2 * p + 1:2 * p + 2])
            t = wp * o[p]
            acc = t if acc is None else acc + t
        pairs.append(acc)
    return jnp.concatenate(pairs, axis=1)


def _split2(a):
    hi = a.astype(BF16)
    lo = (a - hi.astype(F32)).astype(BF16)
    return hi, lo


def _dot(a, b):
    return jnp.dot(a, b, preferred_element_type=F32)


def _dot_bf16(a, b):
    return _dot(a.astype(BF16), b.astype(BF16))


def _dot_hp(a, b):
    ah, al = _split2(a)
    bh, bl = _split2(b)
    return _dot(jnp.concatenate([ah, al, ah], axis=1), jnp.concatenate([bh, bh, bl], axis=0))


def _dot_t(a, b):
    return lax.dot_general(a, b, (((1,), (1,)), ((), ())), preferred_element_type=F32)


def _dot_t_hp(a, b):
    ah, al = _split2(a)
    bh, bl = _split2(b)
    return _dot_t(jnp.concatenate([ah, al, ah], axis=1), jnp.concatenate([bh, bh, bl], axis=1))


def _dn_body(hq, hk, hv, small, gate, pvec, normw, out, s_ref, *, nch):
    n = pl.program_id(1)
    c_ = QB
    npair = DN_HEADS // 2
    qkv = (hq, hk, hv)

    @pl.when(n == 0)
    def _():
        s_ref[...] = jnp.zeros_like(s_ref)

    lane = lax.broadcasted_iota(I32, (c_, LANES), 1)
    low = lane < HEAD_DIM
    row = lax.broadcasted_iota(I32, (c_, c_), 0)
    col = lax.broadcasted_iota(I32, (c_, c_), 1)
    lower = row >= col
    strict = row > col
    eye = jnp.where(row == col, 1.0, 0.0)
    blockdiag = (row < HEAD_DIM) == (col < HEAD_DIM)

    def headsum(x):
        sa = jnp.sum(jnp.where(low, x, 0.0), axis=-1, keepdims=True)
        sb = jnp.sum(jnp.where(low, 0.0, x), axis=-1, keepdims=True)
        return jnp.where(low, sa, sb)

    def col_of(arr, j):
        return arr[:, j:j + 1]

    ltri = jnp.where(lower, 1.0, 0.0).astype(BF16)
    ltri3 = jnp.concatenate([ltri, ltri, ltri], axis=1)
    pair, xs, pws, qks = [], [], [], []
    for ch in range(nch):
        rs = slice(ch * c_, (ch + 1) * c_)
        sm = small[rs, :]
        beta_all = _sigmoid(sm)
        z = sm + pvec[1:2, :]
        softplus = jnp.maximum(z, 0.0) + jnp.log(1.0 + jnp.exp(-jnp.abs(z)))
        g_all = -jnp.exp(pvec[0:1, :]) * softplus
        g1 = g_all.astype(BF16)
        rem = g_all - g1.astype(F32)
        g2 = rem.astype(BF16)
        g3 = (rem - g2.astype(F32)).astype(BF16)
        gc_all = _dot(ltri3, jnp.concatenate([g1, g2, g3], axis=0))
        gc_t = gc_all.T

        for p in range(npair):
            cs = slice(p * LANES, (p + 1) * LANES)
            q, k, v = qkv[0][rs, cs], qkv[1][rs, cs], qkv[2][rs, cs]
            q = q * lax.rsqrt(headsum(q * q) + NORM_EPS) * (HEAD_DIM ** -0.5)
            k = k * lax.rsqrt(headsum(k * k) + NORM_EPS)
            beta = jnp.where(low, col_of(beta_all, 2 * p), col_of(beta_all, 2 * p + 1))
            gc = jnp.where(low, col_of(gc_all, DN_HEADS + 2 * p), col_of(gc_all, DN_HEADS + 2 * p + 1))
            eg = jnp.exp(gc)
            rhs = jnp.concatenate([v * beta, k * beta * eg], axis=1)
            pair.append(dict(q=q, k=k, kb=k.astype(BF16), gc=gc, eg=eg, rhs=rhs))

        for h in range(DN_HEADS):
            pr = pair[ch * npair + h // 2]
            sel = low if h % 2 == 0 else ~low
            gcol = col_of(gc_all, DN_HEADS + h)
            grow = gc_t[DN_HEADS + h:DN_HEADS + h + 1, :]
            dec = jnp.where(lower, jnp.exp(jnp.minimum(gcol - grow, 0.0)), 0.0)
            kk = _dot_t(jnp.where(sel, pr["k"], 0.0).astype(BF16), pr["kb"])
            qk = _dot_t_hp(jnp.where(sel, pr["q"], 0.0), pr["k"])
            nm = jnp.where(strict, col_of(beta_all, h) * kk * dec, 0.0)
            xs.append(eye - nm)
            pws.append(nm)
            qks.append((qk * dec).astype(BF16))

    for level in range(int(math.log2(c_)) - 1):
        mm = _dot_hp if level < 3 else _dot_bf16
        for i in range(len(xs)):
            pws[i] = mm(pws[i], pws[i])
        for i in range(len(xs)):
            xs[i] = xs[i] + mm(xs[i], pws[i])
    uws = [_dot_hp(xs[i], pair[i // 2]["rhs"]) for i in range(len(xs))]

    for ch in range(nch):
        rs = slice(ch * c_, (ch + 1) * c_)
        for p in range(npair):
            pr = pair[ch * npair + p]
            i0 = ch * DN_HEADS + 2 * p
            cs = slice(p * LANES, (p + 1) * LANES)
            u = jnp.where(low, uws[i0][:, :LANES], uws[i0 + 1][:, :LANES])
            w = jnp.where(low, uws[i0][:, LANES:], uws[i0 + 1][:, LANES:])
            state = s_ref[p]
            sb16 = state.astype(BF16)
            vnew = u - _dot(w.astype(BF16), sb16)
            vn16 = vnew.astype(BF16)
            o = _dot((pr["q"] * pr["eg"]).astype(BF16), sb16)
            o = o + jnp.where(low, _dot(qks[i0], vn16), _dot(qks[i0 + 1], vn16))
            glast = pr["gc"][c_ - 1:c_, :]
            kd = pr["k"] * jnp.exp(glast - pr["gc"])
            upd = _dot(kd.T.astype(BF16), vn16)
            s_ref[p] = state * jnp.exp(glast) + jnp.where(blockdiag, upd, 0.0)
            ms = headsum(o * o) * (1.0 / HEAD_DIM)
            gt = gate[rs, cs]
            out[rs, cs] = (o * lax.rsqrt(ms + NORM_EPS) * normw[:, cs] * (gt * _sigmoid(gt))).astype(out.dtype)


def _delta_net(hr, pvec, normw, batch, seq, nch=2):
    h3 = hr.reshape(batch, seq, R_COLS)
    rows = nch * QB
    qcol = R_DN // DN_WIDTH

    def cur(off):
        return pl.BlockSpec((None, rows, DN_WIDTH), lambda b, n: (b, n, qcol + off))

    out = pl.pallas_call(
        functools.partial(_dn_body, nch=nch), grid=(batch, seq // rows),
        in_specs=[cur(0), cur(1), cur(2),
                  pl.BlockSpec((None, rows, LANES), lambda b, n: (b, n, R_SMALL // LANES)),
                  pl.BlockSpec((None, rows, DN_WIDTH), lambda b, n: (b, n, R_GATE // DN_WIDTH)),
                  pl.BlockSpec((2, LANES), lambda b, n: (0, 0)),
                  pl.BlockSpec((1, DN_WIDTH), lambda b, n: (0, 0))],
        out_specs=pl.BlockSpec((None, rows, DN_WIDTH), lambda b, n: (b, n, 0)),
        out_shape=jax.ShapeDtypeStruct((batch, seq, DN_WIDTH), BF16),
        scratch_shapes=[pltpu.VMEM((DN_HEADS // 2, LANES, LANES), F32)],
        compiler_params=_params("parallel", "arbitrary"),
        name="gated_delta_rule")(h3, h3, h3, h3, h3, pvec, normw)
    return out.reshape(batch * seq, DN_WIDTH)


def _pool_body(cur, prev, wbd, scale, out, xbuf, *, rows):
    n = pl.program_id(1)
    halo = 16
    xbuf[0:halo, :] = prev[...] * jnp.where(n > 0, 1.0, 0.0)
    xbuf[halo:, :] = cur[...]
    u = cur[...]
    lane = lax.broadcasted_iota(I32, (rows, POOL_WIDTH), 1)
    tpos = n * rows + lax.broadcasted_iota(I32, (rows, POOL_WIDTH), 0)
    run = u
    pooled = jnp.zeros_like(u)
    for j in range(1, max(POOL_WINDOWS)):
        run = run + xbuf[halo - j:halo - j + rows, :]
        if j + 1 in POOL_WINDOWS:
            gi = POOL_WINDOWS.index(j + 1)
            cnt = jnp.minimum(tpos + 1, j + 1).astype(F32)
            sel = (lane >= gi * POOL_GDIM) & (lane < (gi + 1) * POOL_GDIM)
            pooled = jnp.where(sel, run / cnt - u, pooled)
    mixed = jnp.dot(pooled.astype(BF16), wbd[...], preferred_element_type=F32)
    out[...] = (mixed * scale[...]).astype(out.dtype)


def _pool(hr, wbd, scale, batch, seq, rows=512):
    h3 = hr.reshape(batch, seq, R_COLS)
    out = pl.pallas_call(
        functools.partial(_pool_body, rows=rows), grid=(batch, seq // rows),
        in_specs=[pl.BlockSpec((None, rows, POOL_WIDTH), lambda b, n: (b, n, R_POOL // POOL_WIDTH)),
                  pl.BlockSpec((None, 16, POOL_WIDTH),
                               lambda b, n: (b, jnp.maximum(n * (rows // 16) - 1, 0), R_POOL // POOL_WIDTH)),
                  pl.BlockSpec((POOL_WIDTH, POOL_WIDTH), lambda b, n: (0, 0)),
                  pl.BlockSpec((1, POOL_WIDTH), lambda b, n: (0, 0))],
        out_specs=pl.BlockSpec((None, rows, POOL_WIDTH), lambda b, n: (b, n, 0)),
        out_shape=jax.ShapeDtypeStruct((batch, seq, POOL_WIDTH), BF16),
        scratch_shapes=[pltpu.VMEM((rows + 16, POOL_WIDTH), F32)],
        compiler_params=_params("parallel", "arbitrary"), name="multiscale_pool")(h3, h3, wbd, scale)
    return out.reshape(batch * seq, POOL_WIDTH)


def _proj_ln_body(o1, o2, o3, l1, l2, l3, d, p, x, wa, wd, wp, g, b, o):
    att = _merge_branches((o1, o2, o3), (l1, l2, l3)).astype(BF16)
    y = jnp.dot(att, wa[...], preferred_element_type=F32)
    y = y + jnp.dot(d[...], wd[...], preferred_element_type=F32)
    y = y + jnp.dot(p[...], wp[...], preferred_element_type=F32)
    o[...] = _layer_norm_rows(ALPHA * x[...] + y, g[...], b[...])


def _proj_ln(branch_o, branch_lse, dn, pool, x, wa, wd, wp, g, b, tm=512):
    t = x.shape[0]
    npair = ATT_HEADS // 2

    def rows(wd_):
        return pl.BlockSpec((tm, wd_), lambda i: (i, 0))

    def full(a_):
        return pl.BlockSpec(a_.shape, lambda i: (0, 0))

    slab = pl.BlockSpec((npair, tm, LANES), lambda i: (0, i, 0))
    return pl.pallas_call(
        _proj_ln_body, grid=(t // tm,),
        in_specs=[slab] * 3 + [rows(LANES)] * 3 + [rows(DN_WIDTH), rows(POOL_WIDTH), rows(D_MODEL),
                                                    full(wa), full(wd), full(wp), full(g), full(b)],
        out_specs=rows(D_MODEL), out_shape=jax.ShapeDtypeStruct((t, D_MODEL), F32),
        compiler_params=_params("parallel"),
        name="out_proj_ln")(*branch_o, *branch_lse, dn, pool, x, wa, wd, wp, g, b)


def _xattn_body(x, wq, k, v, wo, g, b, o):
    xv = x[...]
    q = jnp.dot(xv.astype(BF16), wq[...], preferred_element_type=F32).astype(BF16)
    cols = [slice(h * X_HEAD_DIM, (h + 1) * X_HEAD_DIM) for h in range(X_HEADS)]
    ss = [_dot_t(q[:, cs], k[:, cs]) for cs in cols]
    ms = [jnp.max(s, axis=-1, keepdims=True) for s in ss]
    es = [jnp.exp(s - m) for s, m in zip(ss, ms)]
    ls = [jnp.sum(e, axis=-1, keepdims=True) for e in es]
    heads = [(_dot(e.astype(BF16), v[:, cs]) / l).astype(BF16) for e, l, cs in zip(es, ls, cols)]
    oc = jnp.concatenate(heads, axis=1)
    y = jnp.dot(oc, wo[...], preferred_element_type=F32)
    o[...] = _layer_norm_rows(ALPHA * xv + y, g[...], b[...])


def _cross_attention(x, wq, k, v, wo, g, b, seq, tm=512):
    t = x.shape[0]
    mem_len = k.shape[1]
    per_batch = seq // tm

    def full(a_):
        return pl.BlockSpec(a_.shape, lambda i: (0, 0))

    kv = pl.BlockSpec((None, mem_len, D_MODEL), lambda i: (i // per_batch, 0, 0))
    rows = pl.BlockSpec((tm, D_MODEL), lambda i: (i, 0))
    return pl.pallas_call(
        _xattn_body, grid=(t // tm,),
        in_specs=[rows, full(wq), kv, kv, full(wo), full(g), full(b)],
        out_specs=rows, out_shape=jax.ShapeDtypeStruct((t, D_MODEL), F32),
        compiler_params=_params("parallel"), name="cross_attention_ln")(x, wq, k, v, wo, g, b)


def _ffn_body(x, w1, w3, w2, g, b, o, xb, acc):
    f = pl.program_id(1)

    @pl.when(f == 0)
    def _():
        xb[...] = x[...].astype(BF16)
        acc[...] = jnp.zeros_like(acc)

    xv = xb[...]
    h1 = jnp.dot(xv, w1[...], preferred_element_type=F32)
    h3 = jnp.dot(xv, w3[...], preferred_element_type=F32)
    hh = (h1 * _sigmoid(h1) * h3).astype(BF16)
    acc[...] += jnp.dot(hh, w2[...], preferred_element_type=F32)

    @pl.when(f == pl.num_programs(1) - 1)
    def _():
        o[...] = _layer_norm_rows(ALPHA * x[...] + acc[...], g[...], b[...])


def _ffn_dense(x, w1, w3, w2, g, b, tm=1024, tf=256):
    t = x.shape[0]
    dff = w1.shape[1]
    rows = pl.BlockSpec((tm, D_MODEL), lambda i, f: (i, 0))
    vec = pl.BlockSpec((1, D_MODEL), lambda i, f: (0, 0))
    return pl.pallas_call(
        _ffn_body, grid=(t // tm, dff // tf),
        in_specs=[rows, pl.BlockSpec((D_MODEL, tf), lambda i, f: (0, f)),
                  pl.BlockSpec((D_MODEL, tf), lambda i, f: (0, f)),
                  pl.BlockSpec((tf, D_MODEL), lambda i, f: (f, 0)), vec, vec],
        out_specs=rows, out_shape=jax.ShapeDtypeStruct((t, D_MODEL), F32),
        scratch_shapes=[pltpu.VMEM((tm, D_MODEL), BF16), pltpu.VMEM((tm, D_MODEL), F32)],
        compiler_params=_params("parallel", "arbitrary"), name="ffn_swiglu_ln")(x, w1, w3, w2, g, b)


def _router_body(x, rw, eidx, gates, ranks, counts, carry, *, tm):
    i = pl.program_id(0)

    @pl.when(i == 0)
    def _():
        carry[...] = jnp.zeros_like(carry)

    logits = jnp.dot(x[...], rw[...], preferred_element_type=F32, precision=lax.Precision.HIGHEST)
    lane = lax.broadcasted_iota(I32, (tm, LANES), 1)
    lanef = lane.astype(F32)
    lg = jnp.where(lane < N_EXPERTS, logits, NEG)
    m1 = jnp.max(lg, axis=-1, keepdims=True)
    i1 = jnp.min(jnp.where(lg == m1, lanef, float(LANES)), axis=-1, keepdims=True)
    lg2 = jnp.where(lanef == i1, NEG, lg)
    m2 = jnp.max(lg2, axis=-1, keepdims=True)
    i2 = jnp.min(jnp.where(lg2 == m2, lanef, float(LANES)), axis=-1, keepdims=True)
    e21 = jnp.exp(m2 - m1)
    g1 = 1.0 / (1.0 + e21)
    g2 = e21 * g1
    hot1 = lanef == i1
    hot2 = lanef == i2
    onehot = jnp.where(hot1 | hot2, 1.0, 0.0)
    r = lax.broadcasted_iota(I32, (tm, tm), 0)
    c = lax.broadcasted_iota(I32, (tm, tm), 1)
    before = jnp.dot(jnp.where(r > c, 1.0, 0.0).astype(BF16), onehot.astype(BF16),
                     preferred_element_type=F32) + carry[...]
    rank1 = jnp.sum(jnp.where(hot1, before, 0.0), axis=-1, keepdims=True)
    rank2 = jnp.sum(jnp.where(hot2, before, 0.0), axis=-1, keepdims=True)
    carry[...] = carry[...] + jnp.sum(onehot, axis=0, keepdims=True)
    eidx[...] = jnp.where(lane == 0, i1, jnp.where(lane == 1, i2, 0.0)).astype(I32)
    gates[...] = jnp.where(lane == 0, g1, jnp.where(lane == 1, g2, 0.0))
    ranks[...] = jnp.where(lane == 0, rank1, jnp.where(lane == 1, rank2, 0.0)).astype(I32)
    counts[...] = jnp.broadcast_to(carry[...], counts.shape).astype(I32)


def _router(x, rw, tm=512):
    t = x.shape[0]
    rows = pl.BlockSpec((tm, LANES), lambda i: (i, 0))
    return pl.pallas_call(
        functools.partial(_router_body, tm=tm), grid=(t // tm,),
        in_specs=[pl.BlockSpec((tm, D_MODEL), lambda i: (i, 0)),
                  pl.BlockSpec((D_MODEL, LANES), lambda i: (0, 0))],
        out_specs=[rows, rows, rows, pl.BlockSpec((8, LANES), lambda i: (0, 0))],
        out_shape=[jax.ShapeDtypeStruct((t, LANES), I32), jax.ShapeDtypeStruct((t, LANES), F32),
                   jax.ShapeDtypeStruct((t, LANES), I32), jax.ShapeDtypeStruct((8, LANES), I32)],
        scratch_shapes=[pltpu.VMEM((1, LANES), F32)],
        compiler_params=_params("arbitrary"), name="router_top2")(x, rw)


def _moe_body(be, nused, xs, w1, w3, w2, ys, xb, acc):
    i = pl.program_id(0)
    f = pl.program_id(1)
    last = f == pl.num_programs(1) - 1
    used = i < nused[0]

    @pl.when(used & (f == 0))
    def _():
        xb[...] = xs[...].astype(BF16)
        acc[...] = jnp.zeros_like(acc)

    @pl.when(used)
    def _():
        xv = xb[...]
        h1 = jnp.dot(xv, w1[...], preferred_element_type=F32)
        h3 = jnp.dot(xv, w3[...], preferred_element_type=F32)
        hh = (h1 * _sigmoid(h1) * h3).astype(BF16)
        acc[...] += jnp.dot(hh, w2[...], preferred_element_type=F32)

    @pl.when(used & last)
    def _():
        ys[...] = acc[...]

    @pl.when(jnp.logical_not(used) & last)
    def _():
        ys[...] = jnp.zeros_like(ys)


def _moe_experts(block_e, n_used, xs, w1, w3, w2, tm, tf=896):
    n_slots = xs.shape[0]
    dex = w1.shape[2]
    nblk = n_slots // tm

    def live(i, nu):
        return jnp.minimum(i, nu[0] - 1)

    grid_spec = pltpu.PrefetchScalarGridSpec(
        num_scalar_prefetch=2, grid=(nblk, dex // tf),
        in_specs=[pl.BlockSpec((tm, D_MODEL), lambda i, f, be, nu: (live(i, nu), 0)),
                  pl.BlockSpec((None, D_MODEL, tf), lambda i, f, be, nu: (be[i], 0, jnp.where(i < nu[0], f, dex // tf - 1))),
                  pl.BlockSpec((None, D_MODEL, tf), lambda i, f, be, nu: (be[i], 0, jnp.where(i < nu[0], f, dex // tf - 1))),
                  pl.BlockSpec((None, tf, D_MODEL), lambda i, f, be, nu: (be[i], jnp.where(i < nu[0], f, dex // tf - 1), 0))],
        out_specs=pl.BlockSpec((tm, D_MODEL), lambda i, f, be, nu: (i, 0)),
        scratch_shapes=[pltpu.VMEM((tm, D_MODEL), BF16), pltpu.VMEM((tm, D_MODEL), F32)])
    return pl.pallas_call(
        _moe_body, grid_spec=grid_spec, out_shape=jax.ShapeDtypeStruct((n_slots, D_MODEL), F32),
        compiler_params=_params("arbitrary", "arbitrary"), name="moe_experts")(block_e, n_used, xs, w1, w3, w2)


SUBLANES = 8


def _tile_row(ref3, idx):
    return ref3.at[lax.shift_right_logical(idx, 3), pl.ds(idx & (SUBLANES - 1), 1), :]


def _start_row_gather(src, ids, buf, sem, slot, nrows):
    def issue(j, carry):
        for u in range(SUBLANES):
            idx = ids[0, j * SUBLANES + u]
            pltpu.make_async_copy(_tile_row(src, idx), buf.at[slot, j, pl.ds(u, 1), :], sem.at[slot]).start()
        return carry

    lax.fori_loop(0, nrows // SUBLANES, issue, 0)


def _wait_row_gather(src, buf, sem, slot, nrows):
    pltpu.make_async_copy(src.at[pl.ds(0, nrows // SUBLANES)], buf.at[slot], sem.at[slot]).wait()


def _pipelined_row_gather(src, ids_cur, ids_next, buf, sem, nrows):
    i = pl.program_id(0)
    slot = lax.rem(i, 2)

    @pl.when(i == 0)
    def _():
        _start_row_gather(src, ids_cur, buf, sem, 0, nrows)

    @pl.when(i + 1 < pl.num_programs(0))
    def _():
        _start_row_gather(src, ids_next, buf, sem, 1 - slot, nrows)

    _wait_row_gather(src, buf, sem, slot, nrows)
    return slot


def _dispatch_body(ids, x, init, xs, sem, *, tm):
    del init

    def issue(j, carry):
        for u in range(SUBLANES):
            for k in range(2):
                dst = _tile_row(xs, ids[0, k * tm + j * SUBLANES + u])
                pltpu.make_async_copy(x.at[j, pl.ds(u, 1), :], dst, sem).start()
        return carry

    lax.fori_loop(0, tm // SUBLANES, issue, 0)
    for _ in range(2):
        pltpu.make_async_copy(x, xs.at[pl.ds(0, tm // SUBLANES)], sem).wait()


def _moe_dispatch(x, ids, n_slots, tm):
    t = x.shape[0]
    xs = pl.pallas_call(
        functools.partial(_dispatch_body, tm=tm), grid=(t // tm,),
        in_specs=[pl.BlockSpec((None, 1, 2 * tm), lambda i: (i, 0, 0), memory_space=pltpu.SMEM),
                  pl.BlockSpec((tm // SUBLANES, SUBLANES, D_MODEL), lambda i: (i, 0, 0)),
                  pl.BlockSpec(memory_space=pl.ANY)],
        out_specs=pl.BlockSpec(memory_space=pl.ANY),
        out_shape=jax.ShapeDtypeStruct((n_slots // SUBLANES, SUBLANES, D_MODEL), F32),
        scratch_shapes=[pltpu.SemaphoreType.DMA(())],
        input_output_aliases={2: 0},
        compiler_params=_params("arbitrary"),
        name="moe_dispatch_scatter")(ids, x.reshape(t // SUBLANES, SUBLANES, D_MODEL),
                                     jnp.zeros((n_slots // SUBLANES, SUBLANES, D_MODEL), F32))
    return xs.reshape(n_slots, D_MODEL)


def _combine_ln_body(ids_cur, ids_next, x, ys, gates, g, b, o, buf, sem, *, tm):
    slot = _pipelined_row_gather(ys, ids_cur, ids_next, buf, sem, 2 * tm)
    gt = gates[...]
    nt = tm // SUBLANES
    ya = buf[slot, 0:nt].reshape(tm, D_MODEL)
    yb = buf[slot, nt:].reshape(tm, D_MODEL)
    o[...] = _layer_norm_rows(ALPHA * x[...] + (ya * gt[:, 0:1] + yb * gt[:, 1:2]), g[...], b[...])


def _combine_ln(x, ys, ids, gates, g, b, tm):
    t = x.shape[0]
    nt = t // tm
    smem = functools.partial(pl.BlockSpec, (None, 1, 2 * tm), memory_space=pltpu.SMEM)
    rows = pl.BlockSpec((tm, D_MODEL), lambda i: (i, 0))
    vec = pl.BlockSpec((1, D_MODEL), lambda i: (0, 0))
    return pl.pallas_call(
        functools.partial(_combine_ln_body, tm=tm), grid=(nt,),
        in_specs=[smem(lambda i: (i, 0, 0)), smem(lambda i: (jnp.minimum(i + 1, nt - 1), 0, 0)),
                  rows, pl.BlockSpec(memory_space=pl.ANY),
                  pl.BlockSpec((tm, LANES), lambda i: (i, 0)), vec, vec],
        out_specs=rows, out_shape=jax.ShapeDtypeStruct((t, D_MODEL), F32),
        scratch_shapes=[pltpu.VMEM((2, 2 * tm // SUBLANES, SUBLANES, D_MODEL), F32),
                        pltpu.SemaphoreType.DMA((2,))],
        compiler_params=_params("arbitrary"),
        name="moe_combine_ln")(ids, ids, x, ys.reshape(-1, SUBLANES, D_MODEL), gates, g, b)


def _moe_layer(x, rw_pad, w1, w3, w2, g, b, tm=512):
    t = x.shape[0]
    eidx, gates, ranks, counts = _router(x, rw_pad)
    cnt = counts[0, :N_EXPERTS]
    padded = (cnt + tm - 1) // tm * tm
    pad_end = jnp.cumsum(padded)
    pad_start = pad_end - padded
    e2 = eidx[:, :2]
    dest = pad_start[e2] + ranks[:, :2]
    nblk = (2 * t) // tm + N_EXPERTS
    n_slots = nblk * tm
    tt = 512
    ids = dest.reshape(t // tt, tt, 2).transpose(0, 2, 1).reshape(t // tt, 1, 2 * tt)
    n_used = (pad_end[-1] // tm).astype(I32).reshape(1)
    blk = jnp.arange(nblk, dtype=I32) * tm
    block_e = jnp.minimum(jnp.sum(pad_end[None, :] <= blk[:, None], axis=1), N_EXPERTS - 1).astype(I32)
    block_e = jnp.where(jnp.arange(nblk) < n_used[0], block_e, block_e[jnp.maximum(n_used[0] - 1, 0)])
    xs = _moe_dispatch(x, ids, n_slots, tt)
    ys = _moe_experts(block_e, n_used, xs, w1, w3, w2, tm)
    return _combine_ln(x, ys, ids, gates, g, b, tt)


def _block_diag(w):
    g, c, _ = w.shape
    eye = jnp.eye(g, dtype=w.dtype)
    return (eye[:, None, :, None] * w[:, :, None, :]).reshape(g * c, g * c)


def kernel(x, mem, positions, w_in, conv_w, a_log, dt_bias, dn_norm_w, pool_w, pool_scale, w_out, ln_mix_g, ln_mix_b, xq_w, xk_w, xv_w, xo_w, ln_x_g, ln_x_b, ffn_w1, ffn_w3, ffn_w2, router_w, moe_w1, moe_w3, moe_w2, ln_ffn_g, ln_ffn_b):
    batch, seq, _ = x.shape
    t = batch * seq
    depth = w_in.shape[0]
    xf = x.reshape(t, D_MODEL)
    memf = mem.reshape(batch * mem.shape[1], D_MODEL)

    posc, posr = {}, {}
    for dil in DILATIONS:
        pt = positions.reshape(batch, seq // dil, dil).transpose(0, 2, 1)
        posc[dil] = pt[..., None]
        posr[dil] = pt[:, :, None, :]

    for l in range(depth):
        wl = w_in[l]
        small = jnp.pad(wl[:, OFF_BETA:OFF_GATE], ((0, 0), (0, LANES - 2 * DN_HEADS)))
        w_rest = jnp.concatenate([wl[:, OFF_DN:OFF_BETA], wl[:, OFF_GATE:OFF_POOL], wl[:, OFF_POOL:], small],
                                 axis=1).astype(BF16)
        slabs = _matmul_slabs(xf, wl[:, :OFF_DN].astype(BF16), 512)
        hr = _matmul_conv(xf, w_rest, conv_w[l], seq, 512)

        branches = [_att_branch(slabs, posc[d], posr[d], d, batch, seq) for d in DILATIONS]
        pvec = jnp.zeros((2, LANES), F32)
        pvec = pvec.at[0, DN_HEADS:2 * DN_HEADS].set(a_log[l]).at[1, DN_HEADS:2 * DN_HEADS].set(dt_bias[l])
        normw = jnp.tile(dn_norm_w[l], DN_HEADS)[None, :]
        dn = _delta_net(hr, pvec, normw, batch, seq)

        pool = _pool(hr, _block_diag(pool_w[l]).astype(BF16), pool_scale[l][None, :], batch, seq)

        wo = w_out[l].astype(BF16)
        xf = _proj_ln([o for o, _ in branches], [s for _, s in branches], dn, pool, xf, wo[:ATT_WIDTH],
                      wo[ATT_WIDTH:ATT_WIDTH + DN_WIDTH], wo[ATT_WIDTH + DN_WIDTH:],
                      ln_mix_g[l][None, :], ln_mix_b[l][None, :])

        wkv = jnp.concatenate([xk_w[l], xv_w[l]], axis=1).astype(BF16)
        kv = _matmul(memf, wkv, BF16, 512, 1024).reshape(batch, mem.shape[1], 2 * D_MODEL)
        xf = _cross_attention(xf, (xq_w[l] * (X_HEAD_DIM ** -0.5)).astype(BF16), kv[..., :D_MODEL],
                              kv[..., D_MODEL:], xo_w[l].astype(BF16), ln_x_g[l][None, :],
                              ln_x_b[l][None, :], seq)

        j = l // 2
        gl, bl = ln_ffn_g[l][None, :], ln_ffn_b[l][None, :]
        if l % 2 == 0:
            xf = _ffn_dense(xf, ffn_w1[j].astype(BF16), ffn_w3[j].astype(BF16), ffn_w2[j].astype(BF16), gl, bl)
        else:
            rw_pad = jnp.pad(router_w[j], ((0, 0), (0, LANES - N_EXPERTS)))
            xf = _moe_layer(xf, rw_pad, moe_w1[j].astype(BF16), moe_w3[j].astype(BF16),
                            moe_w2[j].astype(BF16), gl, bl)
    return xf.reshape(batch, seq, D_MODEL)
```

```python
import functools
import math

import jax
import jax.numpy as jnp
from jax import lax
from jax.experimental import pallas as pl
from jax.experimental.pallas import tpu as pltpu

F32, BF16, I32 = jnp.float32, jnp.bfloat16, jnp.int32

D_MODEL = 1024
HEAD_DIM = 64
ATT_HEADS = 6
ATT_WIDTH = ATT_HEADS * HEAD_DIM
DN_HEADS = 6
DN_WIDTH = DN_HEADS * HEAD_DIM
POOL_WIDTH = 256
POOL_WINDOWS = (2, 4, 8, 16)
POOL_GDIM = POOL_WIDTH // len(POOL_WINDOWS)
DILATIONS = (1, 4, 16)
CONV_K = 4
X_HEADS = 4
X_HEAD_DIM = D_MODEL // X_HEADS
N_EXPERTS = 8
DEPTH = 4
ALPHA = (2 * DEPTH) ** 0.25
LN_EPS = 1e-5
NORM_EPS = 1e-6

OFF_DN = 3 * ATT_WIDTH
OFF_BETA = OFF_DN + 3 * DN_WIDTH
OFF_DECAY = OFF_BETA + DN_HEADS
OFF_GATE = OFF_DECAY + DN_HEADS
OFF_POOL = OFF_GATE + DN_WIDTH

LANES = 128
MXU_WIDTH = 256
R_DN = 0
R_GATE = R_DN + 3 * DN_WIDTH
R_POOL = R_GATE + DN_WIDTH
R_SMALL = R_POOL + POOL_WIDTH
R_COLS = R_SMALL + LANES

QB = 128
NEG = -1e30
VMEM_LIMIT = 48 * 1024 * 1024


def _params(*sem):
    return pltpu.CompilerParams(dimension_semantics=sem, vmem_limit_bytes=VMEM_LIMIT)


def _sigmoid(x):
    return 1.0 / (1.0 + jnp.exp(-x))


def _layer_norm_rows(z, g, b):
    mu = jnp.mean(z, axis=-1, keepdims=True)
    zc = z - mu
    var = jnp.mean(zc * zc, axis=-1, keepdims=True)
    return zc * lax.rsqrt(var + LN_EPS) * g + b


def _rows(start, size, stride):
    return pl.ds(start, size) if stride == 1 else pl.ds(start, size, stride=stride)


def _mm_body(x_ref, w_ref, o_ref):
    o_ref[...] = jnp.dot(x_ref[...].astype(BF16), w_ref[...],
                         preferred_element_type=F32).astype(o_ref.dtype)


def _matmul(x, w, out_dtype, tm, tn):
    m, k = x.shape
    n = w.shape[1]
    return pl.pallas_call(
        _mm_body, grid=(m // tm, n // tn),
        in_specs=[pl.BlockSpec((tm, k), lambda i, j: (i, 0)),
                  pl.BlockSpec((k, tn), lambda i, j: (0, j))],
        out_specs=pl.BlockSpec((tm, tn), lambda i, j: (i, j)),
        out_shape=jax.ShapeDtypeStruct((m, n), out_dtype),
        compiler_params=_params("parallel", "arbitrary"), name="matmul")(x, w)


CONV_HALO = 8


def _mm_conv_body(x_ref, w_ref, cw_ref, o_ref, hist, cbuf, *, tm, ncv, tiles_per_seq):
    i = pl.program_id(0)

    @pl.when(lax.rem(i, tiles_per_seq) == 0)
    def _():
        hist[...] = jnp.zeros_like(hist)

    xb = x_ref[...].astype(BF16)
    n = w_ref.shape[1]
    starts = list(range(0, ncv, MXU_WIDTH)) + list(range(ncv, n, MXU_WIDTH))
    for c0, c1 in zip(starts, starts[1:] + [n]):
        cs = slice(c0, c1)
        res = jnp.dot(xb, w_ref[:, cs], preferred_element_type=F32)
        if c0 >= ncv:
            o_ref[:, cs] = res
            continue
        cbuf[0:CONV_HALO, cs] = hist[:, cs]
        cbuf[CONV_HALO:, cs] = res
        hist[:, cs] = cbuf[tm:, cs]
        acc = None
        for j in range(CONV_K):
            start = CONV_HALO - (CONV_K - 1) + j
            t = cbuf[start:start + tm, cs] * cw_ref[j:j + 1, cs]
            acc = t if acc is None else acc + t
        o_ref[:, cs] = acc * _sigmoid(acc)


def _matmul_conv(x, w, conv_w, seq, tm):
    m, k = x.shape
    n = w.shape[1]
    ncv = conv_w.shape[1]
    return pl.pallas_call(
        functools.partial(_mm_conv_body, tm=tm, ncv=ncv, tiles_per_seq=seq // tm), grid=(m // tm,),
        in_specs=[pl.BlockSpec((tm, k), lambda i: (i, 0)), pl.BlockSpec((k, n), lambda i: (0, 0)),
                  pl.BlockSpec((CONV_K, ncv), lambda i: (0, 0))],
        out_specs=pl.BlockSpec((tm, n), lambda i: (i, 0)),
        out_shape=jax.ShapeDtypeStruct((m, n), F32),
        scratch_shapes=[pltpu.VMEM((CONV_HALO, ncv), F32), pltpu.VMEM((tm + CONV_HALO, ncv), F32)],
        compiler_params=_params("arbitrary"), name="matmul_conv")(x, w, conv_w)


def _mm_slab_body(x_ref, w_ref, o_ref):
    res = jnp.dot(x_ref[...].astype(BF16), w_ref[...], preferred_element_type=F32)
    for j in range(o_ref.shape[0]):
        o_ref[j] = res[:, j * LANES:(j + 1) * LANES]


def _matmul_slabs(x, w, tm):
    m, k = x.shape
    n = w.shape[1]
    return pl.pallas_call(
        _mm_slab_body, grid=(m // tm,),
        in_specs=[pl.BlockSpec((tm, k), lambda i: (i, 0)), pl.BlockSpec((k, n), lambda i: (0, 0))],
        out_specs=pl.BlockSpec((n // LANES, tm, LANES), lambda i: (0, i, 0)),
        out_shape=jax.ShapeDtypeStruct((n // LANES, m, LANES), F32),
        compiler_params=_params("parallel"), name="matmul_slabs")(x, w)


def _att_body(q_ref, kc_ref, vc_ref, pq_ref, pkc_ref, o_ref, lse_ref,
              kbuf, vbuf, pkbuf, kprev, vprev, pkprev, *, dil, rows, slopes):
    n = pl.program_id(1)
    nsub = rows // QB
    npair = ATT_HEADS // 2

    @pl.when(n == 0)
    def _():
        kprev[...] = jnp.zeros_like(kprev)
        vprev[...] = jnp.zeros_like(vprev)
        pkprev[...] = jnp.zeros_like(pkprev)

    a = lax.broadcasted_iota(I32, (QB, 2 * QB), 0)
    c = lax.broadcasted_iota(I32, (QB, 2 * QB), 1)
    tri = jnp.where(c < QB, c - a, a - (c - QB)) >= 0
    lane = lax.broadcasted_iota(I32, (QB, LANES), 1)
    low = lane < HEAD_DIM

    def residue(r, carry):
        for p in range(npair):
            kbuf[p, 0:QB, :] = kprev[r, p]
            kbuf[p, QB:, :] = kc_ref[p, _rows(r, rows, dil), :].astype(BF16)
            vbuf[p, 0:QB, :] = vprev[r, p]
            vbuf[p, QB:, :] = vc_ref[p, _rows(r, rows, dil), :].astype(BF16)
            kprev[r, p] = kbuf[p, rows:, :]
            vprev[r, p] = vbuf[p, rows:, :]
        pkbuf[0] = pkprev[r]
        for j in range(nsub):
            pkbuf[j + 1] = pkc_ref[r, :, j * QB:(j + 1) * QB]
        pkprev[r] = pkbuf[nsub]

        def sub(sb, carry2):
            r0 = pl.multiple_of(sb * QB, QB)
            t0 = r + r0 * dil
            pq = pq_ref[r, pl.ds(r0, QB), :]
            pk = jnp.concatenate([pkbuf[sb], pkbuf[sb + 1]], axis=1)
            dist = (pq - pk).astype(F32)
            mask = tri & (c >= jnp.where((sb > 0) | (n > 0), 0, QB))
            heads = range(ATT_HEADS)
            ks = [kbuf[p, pl.ds(r0, 2 * QB), :] for p in range(npair)]
            vs = [vbuf[p, pl.ds(r0, 2 * QB), :] for p in range(npair)]
            qs = [q_ref[p, _rows(t0, QB, dil), :] * (HEAD_DIM ** -0.5) for p in range(npair)]
            qms = [jnp.where(low if h % 2 == 0 else ~low, qs[h // 2], 0.0).astype(BF16) for h in heads]
            ss = [lax.dot_general(qms[h], ks[h // 2], (((1,), (1,)), ((), ())), preferred_element_type=F32)
                  for h in heads]
            ss = [jnp.where(mask, ss[h] - slopes[h] * dist, NEG) for h in heads]
            ms = [jnp.max(ss[h], axis=-1, keepdims=True) for h in heads]
            es = [jnp.exp(ss[h] - ms[h]) for h in heads]
            ls = [jnp.sum(es[h], axis=-1, keepdims=True) for h in heads]
            outs = [jnp.dot(es[h].astype(BF16), vs[h // 2], preferred_element_type=F32) / ls[h] for h in heads]
            lse = jnp.zeros((QB, LANES), F32)
            for h in heads:
                lse = jnp.where(lane == h, ms[h] + jnp.log(ls[h]), lse)
            for p in range(npair):
                o_ref[p, _rows(t0, QB, dil), :] = jnp.where(low, outs[2 * p], outs[2 * p + 1])
            lse_ref[_rows(t0, QB, dil), :] = lse
            return carry2

        lax.fori_loop(0, nsub, sub, 0, unroll=min(nsub, 2))
        return carry

    lax.fori_loop(0, dil, residue, 0)


def _att_branch(slabs, posc, posr, dil, batch, seq):
    sub_len = seq // dil
    rows = min(512 if dil == 1 else 2048 // dil, sub_len)
    tok = rows * dil
    nsteps = seq // tok
    npair = ATT_HEADS // 2
    slopes = tuple(2.0 ** (-8.0 * (i + 1) / ATT_HEADS) for i in range(ATT_HEADS))

    def cur(which):
        return pl.BlockSpec((npair, tok, LANES), lambda b, n: (which, b * nsteps + n, 0))

    t = batch * seq
    return pl.pallas_call(
        functools.partial(_att_body, dil=dil, rows=rows, slopes=slopes),
        grid=(batch, nsteps),
        in_specs=[cur(0), cur(1), cur(2),
                  pl.BlockSpec((None, dil, rows, 1), lambda b, n: (b, 0, n, 0)),
                  pl.BlockSpec((None, dil, 1, rows), lambda b, n: (b, 0, 0, n))],
        out_specs=[pl.BlockSpec((npair, tok, LANES), lambda b, n: (0, b * nsteps + n, 0)),
                   pl.BlockSpec((tok, LANES), lambda b, n: (b * nsteps + n, 0))],
        out_shape=[jax.ShapeDtypeStruct((npair, t, LANES), F32),
                   jax.ShapeDtypeStruct((t, LANES), F32)],
        scratch_shapes=[pltpu.VMEM((npair, rows + QB, LANES), BF16),
                        pltpu.VMEM((npair, rows + QB, LANES), BF16),
                        pltpu.VMEM((rows // QB + 1, 1, QB), I32),
                        pltpu.VMEM((dil, npair, QB, LANES), BF16),
                        pltpu.VMEM((dil, npair, QB, LANES), BF16),
                        pltpu.VMEM((dil, 1, QB), I32)],
        compiler_params=_params("arbitrary", "arbitrary"),
        name=f"dilated_att_d{dil}")(slabs, slabs, slabs, posc, posr)


def _merge_branches(os_, ls_):
    ls = [l[...] for l in ls_]
    m = functools.reduce(jnp.maximum, ls)
    ws = [jnp.exp(l - m) for l in ls]
    inv = 1.0 / functools.reduce(lambda a, b: a + b, ws)
    ws = [w * inv for w in ws]
    lane = lax.broadcasted_iota(I32, ls[0].shape, 1)
    low = lane < HEAD_DIM
    pairs = []
    for p in range(ATT_HEADS // 2):
        acc = None
        for w, o in zip(ws, os_):
            wp = jnp.where(low, w[:, 2 * p:2 * p + 1], w[:, 2 * p + 1:2 * p + 2])
            t = wp * o[p]
            acc = t if acc is None else acc + t
        pairs.append(acc)
    return jnp.concatenate(pairs, axis=1)


def _split2(a):
    hi = a.astype(BF16)
    lo = (a - hi.astype(F32)).astype(BF16)
    return hi, lo


def _dot(a, b):
    return jnp.dot(a, b, preferred_element_type=F32)


def _dot_bf16(a, b):
    return _dot(a.astype(BF16), b.astype(BF16))


def _dot_hp(a, b):
    ah, al = _split2(a)
    bh, bl = _split2(b)
    return _dot(jnp.concatenate([ah, al, ah], axis=1), jnp.concatenate([bh, bh, bl], axis=0))


def _dot_t(a, b):
    return lax.dot_general(a, b, (((1,), (1,)), ((), ())), preferred_element_type=F32)


def _dot_t_hp(a, b):
    ah, al = _split2(a)
    bh, bl = _split2(b)
    return _dot_t(jnp.concatenate([ah, al, ah], axis=1), jnp.concatenate([bh, bh, bl], axis=1))


def _dn_body(hq, hk, hv, small, gate, pvec, normw, out, s_ref, *, nch):
    n = pl.program_id(1)
    c_ = QB
    npair = DN_HEADS // 2
    qkv = (hq, hk, hv)

    @pl.when(n == 0)
    def _():
        s_ref[...] = jnp.zeros_like(s_ref)

    lane = lax.broadcasted_iota(I32, (c_, LANES), 1)
    low = lane < HEAD_DIM
    row = lax.broadcasted_iota(I32, (c_, c_), 0)
    col = lax.broadcasted_iota(I32, (c_, c_), 1)
    lower = row >= col
    strict = row > col
    eye = jnp.where(row == col, 1.0, 0.0)
    blockdiag = (row < HEAD_DIM) == (col < HEAD_DIM)

    def headsum(x):
        sa = jnp.sum(jnp.where(low, x, 0.0), axis=-1, keepdims=True)
        sb = jnp.sum(jnp.where(low, 0.0, x), axis=-1, keepdims=True)
        return jnp.where(low, sa, sb)

    def col_of(arr, j):
        return arr[:, j:j + 1]

    ltri = jnp.where(lower, 1.0, 0.0).astype(BF16)
    ltri3 = jnp.concatenate([ltri, ltri, ltri], axis=1)
    pair, xs, pws, qks = [], [], [], []
    for ch in range(nch):
        rs = slice(ch * c_, (ch + 1) * c_)
        sm = small[rs, :]
        beta_all = _sigmoid(sm)
        z = sm + pvec[1:2, :]
        softplus = jnp.maximum(z, 0.0) + jnp.log(1.0 + jnp.exp(-jnp.abs(z)))
        g_all = -jnp.exp(pvec[0:1, :]) * softplus
        g1 = g_all.astype(BF16)
        rem = g_all - g1.astype(F32)
        g2 = rem.astype(BF16)
        g3 = (rem - g2.astype(F32)).astype(BF16)
        gc_all = _dot(ltri3, jnp.concatenate([g1, g2, g3], axis=0))
        gc_t = gc_all.T

        for p in range(npair):
            cs = slice(p * LANES, (p + 1) * LANES)
            q, k, v = qkv[0][rs, cs], qkv[1][rs, cs], qkv[2][rs, cs]
            q = q * lax.rsqrt(headsum(q * q) + NORM_EPS) * (HEAD_DIM ** -0.5)
            k = k * lax.rsqrt(headsum(k * k) + NORM_EPS)
            beta = jnp.where(low, col_of(beta_all, 2 * p), col_of(beta_all, 2 * p + 1))
            gc = jnp.where(low, col_of(gc_all, DN_HEADS + 2 * p), col_of(gc_all, DN_HEADS + 2 * p + 1))
            eg = jnp.exp(gc)
            rhs = jnp.concatenate([v * beta, k * beta * eg], axis=1)
            pair.append(dict(q=q, k=k, kb=k.astype(BF16), gc=gc, eg=eg, rhs=rhs))

        for h in range(DN_HEADS):
            pr = pair[ch * npair + h // 2]
            sel = low if h % 2 == 0 else ~low
            gcol = col_of(gc_all, DN_HEADS + h)
            grow = gc_t[DN_HEADS + h:DN_HEADS + h + 1, :]
            dec = jnp.where(lower, jnp.exp(jnp.minimum(gcol - grow, 0.0)), 0.0)
            kk = _dot_t(jnp.where(sel, pr["k"], 0.0).astype(BF16), pr["kb"])
            qk = _dot_t_hp(jnp.where(sel, pr["q"], 0.0), pr["k"])
            nm = jnp.where(strict, col_of(beta_all, h) * kk * dec, 0.0)
            xs.append(eye - nm)
            pws.append(nm)
            qks.append((qk * dec).astype(BF16))

    for level in range(int(math.log2(c_)) - 1):
        mm = _dot_hp if level < 3 else _dot_bf16
        for i in range(len(xs)):
            pws[i] = mm(pws[i], pws[i])
        for i in range(len(xs)):
            xs[i] = xs[i] + mm(xs[i], pws[i])
    uws = [_dot_hp(xs[i], pair[i // 2]["rhs"]) for i in range(len(xs))]

    for ch in range(nch):
        rs = slice(ch * c_, (ch + 1) * c_)
        for p in range(npair):
            pr = pair[ch * npair + p]
            i0 = ch * DN_HEADS + 2 * p
            cs = slice(p * LANES, (p + 1) * LANES)
            u = jnp.where(low, uws[i0][:, :LANES], uws[i0 + 1][:, :LANES])
            w = jnp.where(low, uws[i0][:, LANES:], uws[i0 + 1][:, LANES:])
            state = s_ref[p]
            sb16 = state.astype(BF16)
            vnew = u - _dot(w.astype(BF16), sb16)
            vn16 = vnew.astype(BF16)
            o = _dot((pr["q"] * pr["eg"]).astype(BF16), sb16)
            o = o + jnp.where(low, _dot(qks[i0], vn16), _dot(qks[i0 + 1], vn16))
            glast = pr["gc"][c_ - 1:c_, :]
            kd = pr["k"] * jnp.exp(glast - pr["gc"])
            upd = _dot(kd.T.astype(BF16), vn16)
            s_ref[p] = state * jnp.exp(glast) + jnp.where(blockdiag, upd, 0.0)
            ms = headsum(o * o) * (1.0 / HEAD_DIM)
            gt = gate[rs, cs]
            out[rs, cs] = (o * lax.rsqrt(ms + NORM_EPS) * normw[:, cs] * (gt * _sigmoid(gt))).astype(out.dtype)


def _delta_net(hr, pvec, normw, batch, seq, nch=2):
    h3 = hr.reshape(batch, seq, R_COLS)
    rows = nch * QB
    qcol = R_DN // DN_WIDTH

    def cur(off):
        return pl.BlockSpec((None, rows, DN_WIDTH), lambda b, n: (b, n, qcol + off))

    out = pl.pallas_call(
        functools.partial(_dn_body, nch=nch), grid=(batch, seq // rows),
        in_specs=[cur(0), cur(1), cur(2),
                  pl.BlockSpec((None, rows, LANES), lambda b, n: (b, n, R_SMALL // LANES)),
                  pl.BlockSpec((None, rows, DN_WIDTH), lambda b, n: (b, n, R_GATE // DN_WIDTH)),
                  pl.BlockSpec((2, LANES), lambda b, n: (0, 0)),
                  pl.BlockSpec((1, DN_WIDTH), lambda b, n: (0, 0))],
        out_specs=pl.BlockSpec((None, rows, DN_WIDTH), lambda b, n: (b, n, 0)),
        out_shape=jax.ShapeDtypeStruct((batch, seq, DN_WIDTH), BF16),
        scratch_shapes=[pltpu.VMEM((DN_HEADS // 2, LANES, LANES), F32)],
        compiler_params=_params("parallel", "arbitrary"),
        name="gated_delta_rule")(h3, h3, h3, h3, h3, pvec, normw)
    return out.reshape(batch * seq, DN_WIDTH)


def _pool_body(cur, prev, wbd, scale, out, xbuf, *, rows):
    n = pl.program_id(1)
    halo = 16
    xbuf[0:halo, :] = prev[...] * jnp.where(n > 0, 1.0, 0.0)
    xbuf[halo:, :] = cur[...]
    u = cur[...]
    lane = lax.broadcasted_iota(I32, (rows, POOL_WIDTH), 1)
    tpos = n * rows + lax.broadcasted_iota(I32, (rows, POOL_WIDTH), 0)
    run = u
    pooled = jnp.zeros_like(u)
    for j in range(1, max(POOL_WINDOWS)):
        run = run + xbuf[halo - j:halo - j + rows, :]
        if j + 1 in POOL_WINDOWS:
            gi = POOL_WINDOWS.index(j + 1)
            cnt = jnp.minimum(tpos + 1, j + 1).astype(F32)
            sel = (lane >= gi * POOL_GDIM) & (lane < (gi + 1) * POOL_GDIM)
            pooled = jnp.where(sel, run / cnt - u, pooled)
    mixed = jnp.dot(pooled.astype(BF16), wbd[...], preferred_element_type=F32)
    out[...] = (mixed * scale[...]).astype(out.dtype)


def _pool(hr, wbd, scale, batch, seq, rows=512):
    h3 = hr.reshape(batch, seq, R_COLS)
    out = pl.pallas_call(
        functools.partial(_pool_body, rows=rows), grid=(batch, seq // rows),
        in_specs=[pl.BlockSpec((None, rows, POOL_WIDTH), lambda b, n: (b, n, R_POOL // POOL_WIDTH)),
                  pl.BlockSpec((None, 16, POOL_WIDTH),
                               lambda b, n: (b, jnp.maximum(n * (rows // 16) - 1, 0), R_POOL // POOL_WIDTH)),
                  pl.BlockSpec((POOL_WIDTH, POOL_WIDTH), lambda b, n: (0, 0)),
                  pl.BlockSpec((1, POOL_WIDTH), lambda b, n: (0, 0))],
        out_specs=pl.BlockSpec((None, rows, POOL_WIDTH), lambda b, n: (b, n, 0)),
        out_shape=jax.ShapeDtypeStruct((batch, seq, POOL_WIDTH), BF16),
        scratch_shapes=[pltpu.VMEM((rows + 16, POOL_WIDTH), F32)],
        compiler_params=_params("parallel", "arbitrary"), name="multiscale_pool")(h3, h3, wbd, scale)
    return out.reshape(batch * seq, POOL_WIDTH)


def _proj_ln_body(o1, o2, o3, l1, l2, l3, d, p, x, wa, wd, wp, g, b, o):
    att = _merge_branches((o1, o2, o3), (l1, l2, l3)).astype(BF16)
    y = jnp.dot(att, wa[...], preferred_element_type=F32)
    y = y + jnp.dot(d[...], wd[...], preferred_element_type=F32)
    y = y + jnp.dot(p[...], wp[...], preferred_element_type=F32)
    o[...] = _layer_norm_rows(ALPHA * x[...] + y, g[...], b[...])


def _proj_ln(branch_o, branch_lse, dn, pool, x, wa, wd, wp, g, b, tm=512):
    t = x.shape[0]
    npair = ATT_HEADS // 2

    def rows(wd_):
        return pl.BlockSpec((tm, wd_), lambda i: (i, 0))

    def full(a_):
        return pl.BlockSpec(a_.shape, lambda i: (0, 0))

    slab = pl.BlockSpec((npair, tm, LANES), lambda i: (0, i, 0))
    return pl.pallas_call(
        _proj_ln_body, grid=(t // tm,),
        in_specs=[slab] * 3 + [rows(LANES)] * 3 + [rows(DN_WIDTH), rows(POOL_WIDTH), rows(D_MODEL),
                                                    full(wa), full(wd), full(wp), full(g), full(b)],
        out_specs=rows(D_MODEL), out_shape=jax.ShapeDtypeStruct((t, D_MODEL), F32),
        compiler_params=_params("parallel"),
        name="out_proj_ln")(*branch_o, *branch_lse, dn, pool, x, wa, wd, wp, g, b)


def _xattn_body(x, wq, k, v, wo, g, b, o):
    xv = x[...]
    q = jnp.dot(xv.astype(BF16), wq[...], preferred_element_type=F32).astype(BF16)
    cols = [slice(h * X_HEAD_DIM, (h + 1) * X_HEAD_DIM) for h in range(X_HEADS)]
    ss = [_dot_t(q[:, cs], k[:, cs]) for cs in cols]
    ms = [jnp.max(s, axis=-1, keepdims=True) for s in ss]
    es = [jnp.exp(s - m) for s, m in zip(ss, ms)]
    ls = [jnp.sum(e, axis=-1, keepdims=True) for e in es]
    heads = [(_dot(e.astype(BF16), v[:, cs]) / l).astype(BF16) for e, l, cs in zip(es, ls, cols)]
    oc = jnp.concatenate(heads, axis=1)
    y = jnp.dot(oc, wo[...], preferred_element_type=F32)
    o[...] = _layer_norm_rows(ALPHA * xv + y, g[...], b[...])


def _cross_attention(x, wq, k, v, wo, g, b, seq, tm=512):
    t = x.shape[0]
    mem_len = k.shape[1]
    per_batch = seq // tm

    def full(a_):
        return pl.BlockSpec(a_.shape, lambda i: (0, 0))

    kv = pl.BlockSpec((None, mem_len, D_MODEL), lambda i: (i // per_batch, 0, 0))
    rows = pl.BlockSpec((tm, D_MODEL), lambda i: (i, 0))
    return pl.pallas_call(
        _xattn_body, grid=(t // tm,),
        in_specs=[rows, full(wq), kv, kv, full(wo), full(g), full(b)],
        out_specs=rows, out_shape=jax.ShapeDtypeStruct((t, D_MODEL), F32),
        compiler_params=_params("parallel"), name="cross_attention_ln")(x, wq, k, v, wo, g, b)


def _ffn_body(x, w1, w3, w2, g, b, o, xb, acc):
    f = pl.program_id(1)

    @pl.when(f == 0)
    def _():
        xb[...] = x[...].astype(BF16)
        acc[...] = jnp.zeros_like(acc)

    xv = xb[...]
    h1 = jnp.dot(xv, w1[...], preferred_element_type=F32)
    h3 = jnp.dot(xv, w3[...], preferred_element_type=F32)
    hh = (h1 * _sigmoid(h1) * h3).astype(BF16)
    acc[...] += jnp.dot(hh, w2[...], preferred_element_type=F32)

    @pl.when(f == pl.num_programs(1) - 1)
    def _():
        o[...] = _layer_norm_rows(ALPHA * x[...] + acc[...], g[...], b[...])


def _ffn_dense(x, w1, w3, w2, g, b, tm=512, tf=1408):
    t = x.shape[0]
    dff = w1.shape[1]
    rows = pl.BlockSpec((tm, D_MODEL), lambda i, f: (i, 0))
    vec = pl.BlockSpec((1, D_MODEL), lambda i, f: (0, 0))
    return pl.pallas_call(
        _ffn_body, grid=(t // tm, dff // tf),
        in_specs=[rows, pl.BlockSpec((D_MODEL, tf), lambda i, f: (0, f)),
                  pl.BlockSpec((D_MODEL, tf), lambda i, f: (0, f)),
                  pl.BlockSpec((tf, D_MODEL), lambda i, f: (f, 0)), vec, vec],
        out_specs=rows, out_shape=jax.ShapeDtypeStruct((t, D_MODEL), F32),
        scratch_shapes=[pltpu.VMEM((tm, D_MODEL), BF16), pltpu.VMEM((tm, D_MODEL), F32)],
        compiler_params=_params("parallel", "arbitrary"), name="ffn_swiglu_ln")(x, w1, w3, w2, g, b)


def _router_body(x, rw, eidx, gates, ranks, counts, carry, *, tm):
    i = pl.program_id(0)

    @pl.when(i == 0)
    def _():
        carry[...] = jnp.zeros_like(carry)

    logits = jnp.dot(x[...], rw[...], preferred_element_type=F32, precision=lax.Precision.HIGHEST)
    lane = lax.broadcasted_iota(I32, (tm, LANES), 1)
    lanef = lane.astype(F32)
    lg = jnp.where(lane < N_EXPERTS, logits, NEG)
    m1 = jnp.max(lg, axis=-1, keepdims=True)
    i1 = jnp.min(jnp.where(lg == m1, lanef, float(LANES)), axis=-1, keepdims=True)
    lg2 = jnp.where(lanef == i1, NEG, lg)
    m2 = jnp.max(lg2, axis=-1, keepdims=True)
    i2 = jnp.min(jnp.where(lg2 == m2, lanef, float(LANES)), axis=-1, keepdims=True)
    e21 = jnp.exp(m2 - m1)
    g1 = 1.0 / (1.0 + e21)
    g2 = e21 * g1
    hot1 = lanef == i1
    hot2 = lanef == i2
    onehot = jnp.where(hot1 | hot2, 1.0, 0.0)
    r = lax.broadcasted_iota(I32, (tm, tm), 0)
    c = lax.broadcasted_iota(I32, (tm, tm), 1)
    before = jnp.dot(jnp.where(r > c, 1.0, 0.0).astype(BF16), onehot.astype(BF16),
                     preferred_element_type=F32) + carry[...]
    rank1 = jnp.sum(jnp.where(hot1, before, 0.0), axis=-1, keepdims=True)
    rank2 = jnp.sum(jnp.where(hot2, before, 0.0), axis=-1, keepdims=True)
    carry[...] = carry[...] + jnp.sum(onehot, axis=0, keepdims=True)
    eidx[...] = jnp.where(lane == 0, i1, jnp.where(lane == 1, i2, 0.0)).astype(I32)
    gates[...] = jnp.where(lane == 0, g1, jnp.where(lane == 1, g2, 0.0))
    ranks[...] = jnp.where(lane == 0, rank1, jnp.where(lane == 1, rank2, 0.0)).astype(I32)
    counts[...] = jnp.broadcast_to(carry[...], counts.shape).astype(I32)


def _router(x, rw, tm=512):
    t = x.shape[0]
    rows = pl.BlockSpec((tm, LANES), lambda i: (i, 0))
    return pl.pallas_call(
        functools.partial(_router_body, tm=tm), grid=(t // tm,),
        in_specs=[pl.BlockSpec((tm, D_MODEL), lambda i: (i, 0)),
                  pl.BlockSpec((D_MODEL, LANES), lambda i: (0, 0))],
        out_specs=[rows, rows, rows, pl.BlockSpec((8, LANES), lambda i: (0, 0))],
        out_shape=[jax.ShapeDtypeStruct((t, LANES), I32), jax.ShapeDtypeStruct((t, LANES), F32),
                   jax.ShapeDtypeStruct((t, LANES), I32), jax.ShapeDtypeStruct((8, LANES), I32)],
        scratch_shapes=[pltpu.VMEM((1, LANES), F32)],
        compiler_params=_params("arbitrary"), name="router_top2")(x, rw)


def _moe_body(be, nused, xs, w1, w3, w2, ys, xb, acc):
    i = pl.program_id(0)
    f = pl.program_id(1)
    last = f == pl.num_programs(1) - 1
    used = i < nused[0]

    @pl.when(used & (f == 0))
    def _():
        xb[...] = xs[...].astype(BF16)
        acc[...] = jnp.zeros_like(acc)

    @pl.when(used)
    def _():
        xv = xb[...]
        h1 = jnp.dot(xv, w1[...], preferred_element_type=F32)
        h3 = jnp.dot(xv, w3[...], preferred_element_type=F32)
        hh = (h1 * _sigmoid(h1) * h3).astype(BF16)
        acc[...] += jnp.dot(hh, w2[...], preferred_element_type=F32)

    @pl.when(used & last)
    def _():
        ys[...] = acc[...]

    @pl.when(jnp.logical_not(used) & last)
    def _():
        ys[...] = jnp.zeros_like(ys)


def _moe_experts(block_e, n_used, xs, w1, w3, w2, tm, tf=1792):
    n_slots = xs.shape[0]
    dex = w1.shape[2]
    nblk = n_slots // tm

    def live(i, nu):
        return jnp.minimum(i, nu[0] - 1)

    grid_spec = pltpu.PrefetchScalarGridSpec(
        num_scalar_prefetch=2, grid=(nblk, dex // tf),
        in_specs=[pl.BlockSpec((tm, D_MODEL), lambda i, f, be, nu: (live(i, nu), 0)),
                  pl.BlockSpec((None, D_MODEL, tf), lambda i, f, be, nu: (be[i], 0, jnp.where(i < nu[0], f, dex // tf - 1))),
                  pl.BlockSpec((None, D_MODEL, tf), lambda i, f, be, nu: (be[i], 0, jnp.where(i < nu[0], f, dex // tf - 1))),
                  pl.BlockSpec((None, tf, D_MODEL), lambda i, f, be, nu: (be[i], jnp.where(i < nu[0], f, dex // tf - 1), 0))],
        out_specs=pl.BlockSpec((tm, D_MODEL), lambda i, f, be, nu: (i, 0)),
        scratch_shapes=[pltpu.VMEM((tm, D_MODEL), BF16), pltpu.VMEM((tm, D_MODEL), F32)])
    return pl.pallas_call(
        _moe_body, grid_spec=grid_spec, out_shape=jax.ShapeDtypeStruct((n_slots, D_MODEL), F32),
        compiler_params=_params("arbitrary", "arbitrary"), name="moe_experts")(block_e, n_used, xs, w1, w3, w2)


SUBLANES = 8


def _tile_row(ref3, idx):
    return ref3.at[lax.shift_right_logical(idx, 3), pl.ds(idx & (SUBLANES - 1), 1), :]


def _start_row_gather(src, ids, buf, sem, slot, nrows):
    def issue(j, carry):
        for u in range(SUBLANES):
            idx = ids[0, j * SUBLANES + u]
            pltpu.make_async_copy(_tile_row(src, idx), buf.at[slot, j, pl.ds(u, 1), :], sem.at[slot]).start()
        return carry

    lax.fori_loop(0, nrows // SUBLANES, issue, 0)


def _wait_row_gather(src, buf, sem, slot, nrows):
    pltpu.make_async_copy(src.at[pl.ds(0, nrows // SUBLANES)], buf.at[slot], sem.at[slot]).wait()


def _pipelined_row_gather(src, ids_cur, ids_next, buf, sem, nrows):
    i = pl.program_id(0)
    slot = lax.rem(i, 2)

    @pl.when(i == 0)
    def _():
        _start_row_gather(src, ids_cur, buf, sem, 0, nrows)

    @pl.when(i + 1 < pl.num_programs(0))
    def _():
        _start_row_gather(src, ids_next, buf, sem, 1 - slot, nrows)

    _wait_row_gather(src, buf, sem, slot, nrows)
    return slot


def _dispatch_body(ids, x, init, xs, sem, *, tm):
    del init

    def issue(j, carry):
        for u in range(SUBLANES):
            for k in range(2):
                dst = _tile_row(xs, ids[0, k * tm + j * SUBLANES + u])
                pltpu.make_async_copy(x.at[j, pl.ds(u, 1), :], dst, sem).start()
        return carry

    lax.fori_loop(0, tm // SUBLANES, issue, 0)
    for _ in range(2):
        pltpu.make_async_copy(x, xs.at[pl.ds(0, tm // SUBLANES)], sem).wait()


def _moe_dispatch(x, ids, n_slots, tm):
    t = x.shape[0]
    xs = pl.pallas_call(
        functools.partial(_dispatch_body, tm=tm), grid=(t // tm,),
        in_specs=[pl.BlockSpec((None, 1, 2 * tm), lambda i: (i, 0, 0), memory_space=pltpu.SMEM),
                  pl.BlockSpec((tm // SUBLANES, SUBLANES, D_MODEL), lambda i: (i, 0, 0)),
                  pl.BlockSpec(memory_space=pl.ANY)],
        out_specs=pl.BlockSpec(memory_space=pl.ANY),
        out_shape=jax.ShapeDtypeStruct((n_slots // SUBLANES, SUBLANES, D_MODEL), F32),
        scratch_shapes=[pltpu.SemaphoreType.DMA(())],
        input_output_aliases={2: 0},
        compiler_params=_params("arbitrary"),
        name="moe_dispatch_scatter")(ids, x.reshape(t // SUBLANES, SUBLANES, D_MODEL),
                                     jnp.zeros((n_slots // SUBLANES, SUBLANES, D_MODEL), F32))
    return xs.reshape(n_slots, D_MODEL)


def _combine_ln_body(ids_cur, ids_next, x, ys, gates, g, b, o, buf, sem, *, tm):
    slot = _pipelined_row_gather(ys, ids_cur, ids_next, buf, sem, 2 * tm)
    gt = gates[...]
    nt = tm // SUBLANES
    ya = buf[slot, 0:nt].reshape(tm, D_MODEL)
    yb = buf[slot, nt:].reshape(tm, D_MODEL)
    o[...] = _layer_norm_rows(ALPHA * x[...] + (ya * gt[:, 0:1] + yb * gt[:, 1:2]), g[...], b[...])


def _combine_ln(x, ys, ids, gates, g, b, tm):
    t = x.shape[0]
    nt = t // tm
    smem = functools.partial(pl.BlockSpec, (None, 1, 2 * tm), memory_space=pltpu.SMEM)
    rows = pl.BlockSpec((tm, D_MODEL), lambda i: (i, 0))
    vec = pl.BlockSpec((1, D_MODEL), lambda i: (0, 0))
    return pl.pallas_call(
        functools.partial(_combine_ln_body, tm=tm), grid=(nt,),
        in_specs=[smem(lambda i: (i, 0, 0)), smem(lambda i: (jnp.minimum(i + 1, nt - 1), 0, 0)),
                  rows, pl.BlockSpec(memory_space=pl.ANY),
                  pl.BlockSpec((tm, LANES), lambda i: (i, 0)), vec, vec],
        out_specs=rows, out_shape=jax.ShapeDtypeStruct((t, D_MODEL), F32),
        scratch_shapes=[pltpu.VMEM((2, 2 * tm // SUBLANES, SUBLANES, D_MODEL), F32),
                        pltpu.SemaphoreType.DMA((2,))],
        compiler_params=_params("arbitrary"),
        name="moe_combine_ln")(ids, ids, x, ys.reshape(-1, SUBLANES, D_MODEL), gates, g, b)


def _moe_layer(x, rw_pad, w1, w3, w2, g, b, tm=512):
    t = x.shape[0]
    eidx, gates, ranks, counts = _router(x, rw_pad)
    cnt = counts[0, :N_EXPERTS]
    padded = (cnt + tm - 1) // tm * tm
    pad_end = jnp.cumsum(padded)
    pad_start = pad_end - padded
    e2 = eidx[:, :2]
    dest = pad_start[e2] + ranks[:, :2]
    nblk = (2 * t) // tm + N_EXPERTS
    n_slots = nblk * tm
    tt = 512
    ids = dest.reshape(t // tt, tt, 2).transpose(0, 2, 1).reshape(t // tt, 1, 2 * tt)
    n_used = (pad_end[-1] // tm).astype(I32).reshape(1)
    blk = jnp.arange(nblk, dtype=I32) * tm
    block_e = jnp.minimum(jnp.sum(pad_end[None, :] <= blk[:, None], axis=1), N_EXPERTS - 1).astype(I32)
    block_e = jnp.where(jnp.arange(nblk) < n_used[0], block_e, block_e[jnp.maximum(n_used[0] - 1, 0)])
    xs = _moe_dispatch(x, ids, n_slots, tt)
    ys = _moe_experts(block_e, n_used, xs, w1, w3, w2, tm)
    return _combine_ln(x, ys, ids, gates, g, b, tt)


def _block_diag(w):
    g, c, _ = w.shape
    eye = jnp.eye(g, dtype=w.dtype)
    return (eye[:, None, :, None] * w[:, :, None, :]).reshape(g * c, g * c)


def kernel(x, mem, positions, w_in, conv_w, a_log, dt_bias, dn_norm_w, pool_w, pool_scale, w_out, ln_mix_g, ln_mix_b, xq_w, xk_w, xv_w, xo_w, ln_x_g, ln_x_b, ffn_w1, ffn_w3, ffn_w2, router_w, moe_w1, moe_w3, moe_w2, ln_ffn_g, ln_ffn_b):
    batch, seq, _ = x.shape
    t = batch * seq
    depth = w_in.shape[0]
    xf = x.reshape(t, D_MODEL)
    memf = mem.reshape(batch * mem.shape[1], D_MODEL)

    posc, posr = {}, {}
    for dil in DILATIONS:
        pt = positions.reshape(batch, seq // dil, dil).transpose(0, 2, 1)
        posc[dil] = pt[..., None]
        posr[dil] = pt[:, :, None, :]

    for l in range(depth):
        wl = w_in[l]
        small = jnp.pad(wl[:, OFF_BETA:OFF_GATE], ((0, 0), (0, LANES - 2 * DN_HEADS)))
        w_rest = jnp.concatenate([wl[:, OFF_DN:OFF_BETA], wl[:, OFF_GATE:OFF_POOL], wl[:, OFF_POOL:], small],
                                 axis=1).astype(BF16)
        slabs = _matmul_slabs(xf, wl[:, :OFF_DN].astype(BF16), 512)
        hr = _matmul_conv(xf, w_rest, conv_w[l], seq, 512)

        branches = [_att_branch(slabs, posc[d], posr[d], d, batch, seq) for d in DILATIONS]
        pvec = jnp.zeros((2, LANES), F32)
        pvec = pvec.at[0, DN_HEADS:2 * DN_HEADS].set(a_log[l]).at[1, DN_HEADS:2 * DN_HEADS].set(dt_bias[l])
        normw = jnp.tile(dn_norm_w[l], DN_HEADS)[None, :]
        dn = _delta_net(hr, pvec, normw, batch, seq)

        pool = _pool(hr, _block_diag(pool_w[l]).astype(BF16), pool_scale[l][None, :], batch, seq)

        wo = w_out[l].astype(BF16)
        xf = _proj_ln([o for o, _ in branches], [s for _, s in branches], dn, pool, xf, wo[:ATT_WIDTH],
                      wo[ATT_WIDTH:ATT_WIDTH + DN_WIDTH], wo[ATT_WIDTH + DN_WIDTH:],
                      ln_mix_g[l][None, :], ln_mix_b[l][None, :])

        wkv = jnp.concatenate([xk_w[l], xv_w[l]], axis=1).astype(BF16)
        kv = _matmul(memf, wkv, BF16, 512, 1024).reshape(batch, mem.shape[1], 2 * D_MODEL)
        xf = _cross_attention(xf, (xq_w[l] * (X_HEAD_DIM ** -0.5)).astype(BF16), kv[..., :D_MODEL],
                              kv[..., D_MODEL:], xo_w[l].astype(BF16), ln_x_g[l][None, :],
                              ln_x_b[l][None, :], seq)

        j = l // 2
        gl, bl = ln_ffn_g[l][None, :], ln_ffn_b[l][None, :]
        if l % 2 == 0:
            xf = _ffn_dense(xf, ffn_w1[j].astype(BF16), ffn_w3[j].astype(BF16), ffn_w2[j].astype(BF16), gl, bl)
        else:
            rw_pad = jnp.pad(router_w[j], ((0, 0), (0, LANES - N_EXPERTS)))
            xf = _moe_layer(xf, rw_pad, moe_w1[j].astype(BF16), moe_w3[j].astype(BF16),
                            moe_w2[j].astype(BF16), gl, bl)
    return xf.reshape(batch, seq, D_MODEL)
```

```python
import functools
import math

import jax
import jax.numpy as jnp
from jax import lax
from jax.experimental import pallas as pl
from jax.experimental.pallas import tpu as pltpu

F32, BF16, I32 = jnp.float32, jnp.bfloat16, jnp.int32

D_MODEL = 1024
HEAD_DIM = 64
ATT_HEADS = 6
ATT_WIDTH = ATT_HEADS * HEAD_DIM
DN_HEADS = 6
DN_WIDTH = DN_HEADS * HEAD_DIM
POOL_WIDTH = 256
POOL_WINDOWS = (2, 4, 8, 16)
POOL_GDIM = POOL_WIDTH // len(POOL_WINDOWS)
DILATIONS = (1, 4, 16)
CONV_K = 4
X_HEADS = 4
X_HEAD_DIM = D_MODEL // X_HEADS
N_EXPERTS = 8
DEPTH = 4
ALPHA = (2 * DEPTH) ** 0.25
LN_EPS = 1e-5
NORM_EPS = 1e-6

OFF_DN = 3 * ATT_WIDTH
OFF_BETA = OFF_DN + 3 * DN_WIDTH
OFF_DECAY = OFF_BETA + DN_HEADS
OFF_GATE = OFF_DECAY + DN_HEADS
OFF_POOL = OFF_GATE + DN_WIDTH

LANES = 128
MXU_WIDTH = 256
R_DN = 0
R_GATE = R_DN + 3 * DN_WIDTH
R_POOL = R_GATE + DN_WIDTH
R_SMALL = R_POOL + POOL_WIDTH
R_COLS = R_SMALL + LANES

QB = 128
NEG = -1e30
VMEM_LIMIT = 48 * 1024 * 1024


def _params(*sem):
    return pltpu.CompilerParams(dimension_semantics=sem, vmem_limit_bytes=VMEM_LIMIT)


def _sigmoid(x):
    return 1.0 / (1.0 + jnp.exp(-x))


def _layer_norm_rows(z, g, b):
    mu = jnp.mean(z, axis=-1, keepdims=True)
    zc = z - mu
    var = jnp.mean(zc * zc, axis=-1, keepdims=True)
    return zc * lax.rsqrt(var + LN_EPS) * g + b


def _rows(start, size, stride):
    return pl.ds(start, size) if stride == 1 else pl.ds(start, size, stride=stride)


def _mm_body(x_ref, w_ref, o_ref):
    o_ref[...] = jnp.dot(x_ref[...].astype(BF16), w_ref[...],
                         preferred_element_type=F32).astype(o_ref.dtype)


def _matmul(x, w, out_dtype, tm, tn):
    m, k = x.shape
    n = w.shape[1]
    return pl.pallas_call(
        _mm_body, grid=(m // tm, n // tn),
        in_specs=[pl.BlockSpec((tm, k), lambda i, j: (i, 0)),
                  pl.BlockSpec((k, tn), lambda i, j: (0, j))],
        out_specs=pl.BlockSpec((tm, tn), lambda i, j: (i, j)),
        out_shape=jax.ShapeDtypeStruct((m, n), out_dtype),
        compiler_params=_params("parallel", "arbitrary"), name="matmul")(x, w)


CONV_HALO = 8


def _mm_conv_body(x_ref, w_ref, cw_ref, o_ref, hist, cbuf, *, tm, ncv, tiles_per_seq):
    i = pl.program_id(0)

    @pl.when(lax.rem(i, tiles_per_seq) == 0)
    def _():
        hist[...] = jnp.zeros_like(hist)

    xb = x_ref[...].astype(BF16)
    n = w_ref.shape[1]
    starts = list(range(0, ncv, MXU_WIDTH)) + list(range(ncv, n, MXU_WIDTH))
    for c0, c1 in zip(starts, starts[1:] + [n]):
        cs = slice(c0, c1)
        res = jnp.dot(xb, w_ref[:, cs], preferred_element_type=F32)
        if c0 >= ncv:
            o_ref[:, cs] = res
            continue
        cbuf[0:CONV_HALO, cs] = hist[:, cs]
        cbuf[CONV_HALO:, cs] = res
        hist[:, cs] = cbuf[tm:, cs]
        acc = None
        for j in range(CONV_K):
            start = CONV_HALO - (CONV_K - 1) + j
            t = cbuf[start:start + tm, cs] * cw_ref[j:j + 1, cs]
            acc = t if acc is None else acc + t
        o_ref[:, cs] = acc * _sigmoid(acc)


def _matmul_conv(x, w, conv_w, seq, tm):
    m, k = x.shape
    n = w.shape[1]
    ncv = conv_w.shape[1]
    return pl.pallas_call(
        functools.partial(_mm_conv_body, tm=tm, ncv=ncv, tiles_per_seq=seq // tm), grid=(m // tm,),
        in_specs=[pl.BlockSpec((tm, k), lambda i: (i, 0)), pl.BlockSpec((k, n), lambda i: (0, 0)),
                  pl.BlockSpec((CONV_K, ncv), lambda i: (0, 0))],
        out_specs=pl.BlockSpec((tm, n), lambda i: (i, 0)),
        out_shape=jax.ShapeDtypeStruct((m, n), F32),
        scratch_shapes=[pltpu.VMEM((CONV_HALO, ncv), F32), pltpu.VMEM((tm + CONV_HALO, ncv), F32)],
        compiler_params=_params("arbitrary"), name="matmul_conv")(x, w, conv_w)


def _mm_slab_body(x_ref, w_ref, o_ref):
    res = jnp.dot(x_ref[...].astype(BF16), w_ref[...], preferred_element_type=F32)
    for j in range(o_ref.shape[0]):
        o_ref[j] = res[:, j * LANES:(j + 1) * LANES]


def _matmul_slabs(x, w, tm):
    m, k = x.shape
    n = w.shape[1]
    return pl.pallas_call(
        _mm_slab_body, grid=(m // tm,),
        in_specs=[pl.BlockSpec((tm, k), lambda i: (i, 0)), pl.BlockSpec((k, n), lambda i: (0, 0))],
        out_specs=pl.BlockSpec((n // LANES, tm, LANES), lambda i: (0, i, 0)),
        out_shape=jax.ShapeDtypeStruct((n // LANES, m, LANES), F32),
        compiler_params=_params("parallel"), name="matmul_slabs")(x, w)


def _att_body(q_ref, kc_ref, vc_ref, pq_ref, pkc_ref, o_ref, lse_ref,
              kbuf, vbuf, pkbuf, kprev, vprev, pkprev, *, dil, rows, slopes):
    n = pl.program_id(1)
    nsub = rows // QB
    npair = ATT_HEADS // 2

    @pl.when(n == 0)
    def _():
        kprev[...] = jnp.zeros_like(kprev)
        vprev[...] = jnp.zeros_like(vprev)
        pkprev[...] = jnp.zeros_like(pkprev)

    a = lax.broadcasted_iota(I32, (QB, 2 * QB), 0)
    c = lax.broadcasted_iota(I32, (QB, 2 * QB), 1)
    tri = jnp.where(c < QB, c - a, a - (c - QB)) >= 0
    lane = lax.broadcasted_iota(I32, (QB, LANES), 1)
    low = lane < HEAD_DIM

    def residue(r, carry):
        for p in range(npair):
            kbuf[p, 0:QB, :] = kprev[r, p]
            kbuf[p, QB:, :] = kc_ref[p, _rows(r, rows, dil), :].astype(BF16)
            vbuf[p, 0:QB, :] = vprev[r, p]
            vbuf[p, QB:, :] = vc_ref[p, _rows(r, rows, dil), :].astype(BF16)
            kprev[r, p] = kbuf[p, rows:, :]
            vprev[r, p] = vbuf[p, rows:, :]
        pkbuf[0] = pkprev[r]
        for j in range(nsub):
            pkbuf[j + 1] = pkc_ref[r, :, j * QB:(j + 1) * QB]
        pkprev[r] = pkbuf[nsub]

        def sub(sb, carry2):
            r0 = pl.multiple_of(sb * QB, QB)
            t0 = r + r0 * dil
            pq = pq_ref[r, pl.ds(r0, QB), :]
            pk = jnp.concatenate([pkbuf[sb], pkbuf[sb + 1]], axis=1)
            dist = (pq - pk).astype(F32)
            mask = tri & (c >= jnp.where((sb > 0) | (n > 0), 0, QB))
            heads = range(ATT_HEADS)
            ks = [kbuf[p, pl.ds(r0, 2 * QB), :] for p in range(npair)]
            vs = [vbuf[p, pl.ds(r0, 2 * QB), :] for p in range(npair)]
            qs = [q_ref[p, _rows(t0, QB, dil), :] * (HEAD_DIM ** -0.5) for p in range(npair)]
            qms = [jnp.where(low if h % 2 == 0 else ~low, qs[h // 2], 0.0).astype(BF16) for h in heads]
            ss = [lax.dot_general(qms[h], ks[h // 2], (((1,), (1,)), ((), ())), preferred_element_type=F32)
                  for h in heads]
            ss = [jnp.where(mask, ss[h] - slopes[h] * dist, NEG) for h in heads]
            ms = [jnp.max(ss[h], axis=-1, keepdims=True) for h in heads]
            es = [jnp.exp(ss[h] - ms[h]) for h in heads]
            ls = [jnp.sum(es[h], axis=-1, keepdims=True) for h in heads]
            outs = [jnp.dot(es[h].astype(BF16), vs[h // 2], preferred_element_type=F32) / ls[h] for h in heads]
            lse = jnp.zeros((QB, LANES), F32)
            for h in heads:
                lse = jnp.where(lane == h, ms[h] + jnp.log(ls[h]), lse)
            for p in range(npair):
                o_ref[p, _rows(t0, QB, dil), :] = jnp.where(low, outs[2 * p], outs[2 * p + 1])
            lse_ref[_rows(t0, QB, dil), :] = lse
            return carry2

        lax.fori_loop(0, nsub, sub, 0, unroll=min(nsub, 2))
        return carry

    lax.fori_loop(0, dil, residue, 0)


def _att_branch(slabs, posc, posr, dil, batch, seq):
    sub_len = seq // dil
    rows = min(512 if dil == 1 else 2048 // dil, sub_len)
    tok = rows * dil
    nsteps = seq // tok
    npair = ATT_HEADS // 2
    slopes = tuple(2.0 ** (-8.0 * (i + 1) / ATT_HEADS) for i in range(ATT_HEADS))

    def cur(which):
        return pl.BlockSpec((npair, tok, LANES), lambda b, n: (which, b * nsteps + n, 0))

    t = batch * seq
    return pl.pallas_call(
        functools.partial(_att_body, dil=dil, rows=rows, slopes=slopes),
        grid=(batch, nsteps),
        in_specs=[cur(0), cur(1), cur(2),
                  pl.BlockSpec((None, dil, rows, 1), lambda b, n: (b, 0, n, 0)),
                  pl.BlockSpec((None, dil, 1, rows), lambda b, n: (b, 0, 0, n))],
        out_specs=[pl.BlockSpec((npair, tok, LANES), lambda b, n: (0, b * nsteps + n, 0)),
                   pl.BlockSpec((tok, LANES), lambda b, n: (b * nsteps + n, 0))],
        out_shape=[jax.ShapeDtypeStruct((npair, t, LANES), F32),
                   jax.ShapeDtypeStruct((t, LANES), F32)],
        scratch_shapes=[pltpu.VMEM((npair, rows + QB, LANES), BF16),
                        pltpu.VMEM((npair, rows + QB, LANES), BF16),
                        pltpu.VMEM((rows // QB + 1, 1, QB), I32),
                        pltpu.VMEM((dil, npair, QB, LANES), BF16),
                        pltpu.VMEM((dil, npair, QB, LANES), BF16),
                        pltpu.VMEM((dil, 1, QB), I32)],
        compiler_params=_params("arbitrary", "arbitrary"),
        name=f"dilated_att_d{dil}")(slabs, slabs, slabs, posc, posr)


def _merge_branches(os_, ls_):
    ls = [l[...] for l in ls_]
    m = functools.reduce(jnp.maximum, ls)
    ws = [jnp.exp(l - m) for l in ls]
    inv = 1.0 / functools.reduce(lambda a, b: a + b, ws)
    ws = [w * inv for w in ws]
    lane = lax.broadcasted_iota(I32, ls[0].shape, 1)
    low = lane < HEAD_DIM
    pairs = []
    for p in range(ATT_HEADS // 2):
        acc = None
        for w, o in zip(ws, os_):
            wp = jnp.where(low, w[:, 2 * p:2 * p + 1], w[:, 2 * p + 1:2 * p + 2])
            t = wp * o[p]
            acc = t if acc is None else acc + t
        pairs.append(acc)
    return jnp.concatenate(pairs, axis=1)


def _split2(a):
    hi = a.astype(BF16)
    lo = (a - hi.astype(F32)).astype(BF16)
    return hi, lo


def _dot(a, b):
    return jnp.dot(a, b, preferred_element_type=F32)


def _dot_bf16(a, b):
    return _dot(a.astype(BF16), b.astype(BF16))


def _dot_hp(a, b):
    ah, al = _split2(a)
    bh, bl = _split2(b)
    return _dot(jnp.concatenate([ah, al, ah], axis=1), jnp.concatenate([bh, bh, bl], axis=0))


def _dot_t(a, b):
    return lax.dot_general(a, b, (((1,), (1,)), ((), ())), preferred_element_type=F32)


def _dot_t_hp(a, b):
    ah, al = _split2(a)
    bh, bl = _split2(b)
    return _dot_t(jnp.concatenate([ah, al, ah], axis=1), jnp.concatenate([bh, bh, bl], axis=1))


def _dn_body(hq, hk, hv, small, gate, pvec, normw, out, s_ref, *, nch):
    n = pl.program_id(1)
    c_ = QB
    npair = DN_HEADS // 2
    qkv = (hq, hk, hv)

    @pl.when(n == 0)
    def _():
        s_ref[...] = jnp.zeros_like(s_ref)

    lane = lax.broadcasted_iota(I32, (c_, LANES), 1)
    low = lane < HEAD_DIM
    row = lax.broadcasted_iota(I32, (c_, c_), 0)
    col = lax.broadcasted_iota(I32, (c_, c_), 1)
    lower = row >= col
    strict = row > col
    eye = jnp.where(row == col, 1.0, 0.0)
    blockdiag = (row < HEAD_DIM) == (col < HEAD_DIM)

    def headsum(x):
        sa = jnp.sum(jnp.where(low, x, 0.0), axis=-1, keepdims=True)
        sb = jnp.sum(jnp.where(low, 0.0, x), axis=-1, keepdims=True)
        return jnp.where(low, sa, sb)

    def col_of(arr, j):
        return arr[:, j:j + 1]

    ltri = jnp.where(lower, 1.0, 0.0).astype(BF16)
    ltri3 = jnp.concatenate([ltri, ltri, ltri], axis=1)
    pair, xs, pws, qks = [], [], [], []
    for ch in range(nch):
        rs = slice(ch * c_, (ch + 1) * c_)
        sm = small[rs, :]
        beta_all = _sigmoid(sm)
        z = sm + pvec[1:2, :]
        softplus = jnp.maximum(z, 0.0) + jnp.log(1.0 + jnp.exp(-jnp.abs(z)))
        g_all = -jnp.exp(pvec[0:1, :]) * softplus
        g1 = g_all.astype(BF16)
        rem = g_all - g1.astype(F32)
        g2 = rem.astype(BF16)
        g3 = (rem - g2.astype(F32)).astype(BF16)
        gc_all = _dot(ltri3, jnp.concatenate([g1, g2, g3], axis=0))
        gc_t = gc_all.T

        for p in range(npair):
            cs = slice(p * LANES, (p + 1) * LANES)
            q, k, v = qkv[0][rs, cs], qkv[1][rs, cs], qkv[2][rs, cs]
            q = q * lax.rsqrt(headsum(q * q) + NORM_EPS) * (HEAD_DIM ** -0.5)
            k = k * lax.rsqrt(headsum(k * k) + NORM_EPS)
            beta = jnp.where(low, col_of(beta_all, 2 * p), col_of(beta_all, 2 * p + 1))
            gc = jnp.where(low, col_of(gc_all, DN_HEADS + 2 * p), col_of(gc_all, DN_HEADS + 2 * p + 1))
            eg = jnp.exp(gc)
            rhs = jnp.concatenate([v * beta, k * beta * eg], axis=1)
            pair.append(dict(q=q, k=k, kb=k.astype(BF16), gc=gc, eg=eg, rhs=rhs))

        for h in range(DN_HEADS):
            pr = pair[ch * npair + h // 2]
            sel = low if h % 2 == 0 else ~low
            gcol = col_of(gc_all, DN_HEADS + h)
            grow = gc_t[DN_HEADS + h:DN_HEADS + h + 1, :]
            dec = jnp.where(lower, jnp.exp(jnp.minimum(gcol - grow, 0.0)), 0.0)
            kk = _dot_t(jnp.where(sel, pr["k"], 0.0).astype(BF16), pr["kb"])
            qk = _dot_t_hp(jnp.where(sel, pr["q"], 0.0), pr["k"])
            nm = jnp.where(strict, col_of(beta_all, h) * kk * dec, 0.0)
            xs.append(eye - nm)
            pws.append(nm)
            qks.append((qk * dec).astype(BF16))

    for level in range(int(math.log2(c_)) - 1):
        mm = _dot_hp if level < 3 else _dot_bf16
        for i in range(len(xs)):
            pws[i] = mm(pws[i], pws[i])
        for i in range(len(xs)):
            xs[i] = xs[i] + mm(xs[i], pws[i])
    uws = [_dot_hp(xs[i], pair[i // 2]["rhs"]) for i in range(len(xs))]

    for ch in range(nch):
        rs = slice(ch * c_, (ch + 1) * c_)
        for p in range(npair):
            pr = pair[ch * npair + p]
            i0 = ch * DN_HEADS + 2 * p
            cs = slice(p * LANES, (p + 1) * LANES)
            u = jnp.where(low, uws[i0][:, :LANES], uws[i0 + 1][:, :LANES])
            w = jnp.where(low, uws[i0][:, LANES:], uws[i0 + 1][:, LANES:])
            state = s_ref[p]
            sb16 = state.astype(BF16)
            vnew = u - _dot(w.astype(BF16), sb16)
            vn16 = vnew.astype(BF16)
            o = _dot((pr["q"] * pr["eg"]).astype(BF16), sb16)
            o = o + jnp.where(low, _dot(qks[i0], vn16), _dot(qks[i0 + 1], vn16))
            glast = pr["gc"][c_ - 1:c_, :]
            kd = pr["k"] * jnp.exp(glast - pr["gc"])
            upd = _dot(kd.T.astype(BF16), vn16)
            s_ref[p] = state * jnp.exp(glast) + jnp.where(blockdiag, upd, 0.0)
            ms = headsum(o * o) * (1.0 / HEAD_DIM)
            gt = gate[rs, cs]
            out[rs, cs] = (o * lax.rsqrt(ms + NORM_EPS) * normw[:, cs] * (gt * _sigmoid(gt))).astype(out.dtype)


def _delta_net(hr, pvec, normw, batch, seq, nch=2):
    h3 = hr.reshape(batch, seq, R_COLS)
    rows = nch * QB
    qcol = R_DN // DN_WIDTH

    def cur(off):
        return pl.BlockSpec((None, rows, DN_WIDTH), lambda b, n: (b, n, qcol + off))

    out = pl.pallas_call(
        functools.partial(_dn_body, nch=nch), grid=(batch, seq // rows),
        in_specs=[cur(0), cur(1), cur(2),
                  pl.BlockSpec((None, rows, LANES), lambda b, n: (b, n, R_SMALL // LANES)),
                  pl.BlockSpec((None, rows, DN_WIDTH), lambda b, n: (b, n, R_GATE // DN_WIDTH)),
                  pl.BlockSpec((2, LANES), lambda b, n: (0, 0)),
                  pl.BlockSpec((1, DN_WIDTH), lambda b, n: (0, 0))],
        out_specs=pl.BlockSpec((None, rows, DN_WIDTH), lambda b, n: (b, n, 0)),
        out_shape=jax.ShapeDtypeStruct((batch, seq, DN_WIDTH), BF16),
        scratch_shapes=[pltpu.VMEM((DN_HEADS // 2, LANES, LANES), F32)],
        compiler_params=_params("parallel", "arbitrary"),
        name="gated_delta_rule")(h3, h3, h3, h3, h3, pvec, normw)
    return out.reshape(batch * seq, DN_WIDTH)


def _pool_body(cur, prev, wbd, scale, out, xbuf, *, rows):
    n = pl.program_id(1)
    halo = 16
    xbuf[0:halo, :] = prev[...] * jnp.where(n > 0, 1.0, 0.0)
    xbuf[halo:, :] = cur[...]
    u = cur[...]
    lane = lax.broadcasted_iota(I32, (rows, POOL_WIDTH), 1)
    tpos = n * rows + lax.broadcasted_iota(I32, (rows, POOL_WIDTH), 0)
    run = u
    pooled = jnp.zeros_like(u)
    for j in range(1, max(POOL_WINDOWS)):
        run = run + xbuf[halo - j:halo - j + rows, :]
        if j + 1 in POOL_WINDOWS:
            gi = POOL_WINDOWS.index(j + 1)
            cnt = jnp.minimum(tpos + 1, j + 1).astype(F32)
            sel = (lane >= gi * POOL_GDIM) & (lane < (gi + 1) * POOL_GDIM)
            pooled = jnp.where(sel, run / cnt - u, pooled)
    mixed = jnp.dot(pooled.astype(BF16), wbd[...], preferred_element_type=F32)
    out[...] = (mixed * scale[...]).astype(out.dtype)


def _pool(hr, wbd, scale, batch, seq, rows=512):
    h3 = hr.reshape(batch, seq, R_COLS)
    out = pl.pallas_call(
        functools.partial(_pool_body, rows=rows), grid=(batch, seq // rows),
        in_specs=[pl.BlockSpec((None, rows, POOL_WIDTH), lambda b, n: (b, n, R_POOL // POOL_WIDTH)),
                  pl.BlockSpec((None, 16, POOL_WIDTH),
                               lambda b, n: (b, jnp.maximum(n * (rows // 16) - 1, 0), R_POOL // POOL_WIDTH)),
                  pl.BlockSpec((POOL_WIDTH, POOL_WIDTH), lambda b, n: (0, 0)),
                  pl.BlockSpec((1, POOL_WIDTH), lambda b, n: (0, 0))],
        out_specs=pl.BlockSpec((None, rows, POOL_WIDTH), lambda b, n: (b, n, 0)),
        out_shape=jax.ShapeDtypeStruct((batch, seq, POOL_WIDTH), BF16),
        scratch_shapes=[pltpu.VMEM((rows + 16, POOL_WIDTH), F32)],
        compiler_params=_params("parallel", "arbitrary"), name="multiscale_pool")(h3, h3, wbd, scale)
    return out.reshape(batch * seq, POOL_WIDTH)


def _proj_ln_body(o1, o2, o3, l1, l2, l3, d, p, x, wa, wd, wp, g, b, o):
    att = _merge_branches((o1, o2, o3), (l1, l2, l3)).astype(BF16)
    y = jnp.dot(att, wa[...], preferred_element_type=F32)
    y = y + jnp.dot(d[...], wd[...], preferred_element_type=F32)
    y = y + jnp.dot(p[...], wp[...], preferred_element_type=F32)
    o[...] = _layer_norm_rows(ALPHA * x[...] + y, g[...], b[...])


def _proj_ln(branch_o, branch_lse, dn, pool, x, wa, wd, wp, g, b, tm=512):
    t = x.shape[0]
    npair = ATT_HEADS // 2

    def rows(wd_):
        return pl.BlockSpec((tm, wd_), lambda i: (i, 0))

    def full(a_):
        return pl.BlockSpec(a_.shape, lambda i: (0, 0))

    slab = pl.BlockSpec((npair, tm, LANES), lambda i: (0, i, 0))
    return pl.pallas_call(
        _proj_ln_body, grid=(t // tm,),
        in_specs=[slab] * 3 + [rows(LANES)] * 3 + [rows(DN_WIDTH), rows(POOL_WIDTH), rows(D_MODEL),
                                                    full(wa), full(wd), full(wp), full(g), full(b)],
        out_specs=rows(D_MODEL), out_shape=jax.ShapeDtypeStruct((t, D_MODEL), F32),
        compiler_params=_params("parallel"),
        name="out_proj_ln")(*branch_o, *branch_lse, dn, pool, x, wa, wd, wp, g, b)


def _xattn_body(x, wq, k, v, wo, g, b, o):
    xv = x[...]
    q = jnp.dot(xv.astype(BF16), wq[...], preferred_element_type=F32).astype(BF16)
    cols = [slice(h * X_HEAD_DIM, (h + 1) * X_HEAD_DIM) for h in range(X_HEADS)]
    ss = [_dot_t(q[:, cs], k[:, cs]) for cs in cols]
    ms = [jnp.max(s, axis=-1, keepdims=True) for s in ss]
    es = [jnp.exp(s - m) for s, m in zip(ss, ms)]
    ls = [jnp.sum(e, axis=-1, keepdims=True) for e in es]
    heads = [(_dot(e.astype(BF16), v[:, cs]) / l).astype(BF16) for e, l, cs in zip(es, ls, cols)]
    oc = jnp.concatenate(heads, axis=1)
    y = jnp.dot(oc, wo[...], preferred_element_type=F32)
    o[...] = _layer_norm_rows(ALPHA * xv + y, g[...], b[...])


def _cross_attention(x, wq, k, v, wo, g, b, seq, tm=1024):
    t = x.shape[0]
    mem_len = k.shape[1]
    per_batch = seq // tm

    def full(a_):
        return pl.BlockSpec(a_.shape, lambda i: (0, 0))

    kv = pl.BlockSpec((None, mem_len, D_MODEL), lambda i: (i // per_batch, 0, 0))
    rows = pl.BlockSpec((tm, D_MODEL), lambda i: (i, 0))
    return pl.pallas_call(
        _xattn_body, grid=(t // tm,),
        in_specs=[rows, full(wq), kv, kv, full(wo), full(g), full(b)],
        out_specs=rows, out_shape=jax.ShapeDtypeStruct((t, D_MODEL), F32),
        compiler_params=_params("parallel"), name="cross_attention_ln")(x, wq, k, v, wo, g, b)


def _ffn_body(x, w1, w3, w2, g, b, o, *, chunks):
    xv = x[...]
    xb = xv.astype(BF16)
    y = None
    for c0, c1 in chunks:
        h1 = jnp.dot(xb, w1[:, c0:c1], preferred_element_type=F32)
        h3 = jnp.dot(xb, w3[:, c0:c1], preferred_element_type=F32)
        hh = (h1 * _sigmoid(h1) * h3).astype(BF16)
        t = jnp.dot(hh, w2[c0:c1, :], preferred_element_type=F32)
        y = t if y is None else y + t
    o[...] = _layer_norm_rows(ALPHA * xv + y, g[...], b[...])


def _ffn_dense(x, w1, w3, w2, g, b, tm=512, chunk=6 * MXU_WIDTH):
    t = x.shape[0]
    dff = w1.shape[1]
    chunks = tuple((c0, min(c0 + chunk, dff)) for c0 in range(0, dff, chunk))
    rows = pl.BlockSpec((tm, D_MODEL), lambda i: (i, 0))
    vec = pl.BlockSpec((1, D_MODEL), lambda i: (0, 0))

    def resident(a_):
        return pl.BlockSpec(a_.shape, lambda i: (0, 0), pipeline_mode=pl.Buffered(1))

    return pl.pallas_call(
        functools.partial(_ffn_body, chunks=chunks), grid=(t // tm,),
        in_specs=[rows, resident(w1), resident(w3), resident(w2), vec, vec],
        out_specs=rows, out_shape=jax.ShapeDtypeStruct((t, D_MODEL), F32),
        compiler_params=_params("parallel"), name="ffn_swiglu_ln")(x, w1, w3, w2, g, b)


def _router_body(x, rw, eidx, gates, ranks, counts, carry, *, tm):
    i = pl.program_id(0)

    @pl.when(i == 0)
    def _():
        carry[...] = jnp.zeros_like(carry)

    logits = jnp.dot(x[...], rw[...], preferred_element_type=F32, precision=lax.Precision.HIGHEST)
    lane = lax.broadcasted_iota(I32, (tm, LANES), 1)
    lanef = lane.astype(F32)
    lg = jnp.where(lane < N_EXPERTS, logits, NEG)
    m1 = jnp.max(lg, axis=-1, keepdims=True)
    i1 = jnp.min(jnp.where(lg == m1, lanef, float(LANES)), axis=-1, keepdims=True)
    lg2 = jnp.where(lanef == i1, NEG, lg)
    m2 = jnp.max(lg2, axis=-1, keepdims=True)
    i2 = jnp.min(jnp.where(lg2 == m2, lanef, float(LANES)), axis=-1, keepdims=True)
    e21 = jnp.exp(m2 - m1)
    g1 = 1.0 / (1.0 + e21)
    g2 = e21 * g1
    hot1 = lanef == i1
    hot2 = lanef == i2
    onehot = jnp.where(hot1 | hot2, 1.0, 0.0)
    r = lax.broadcasted_iota(I32, (tm, tm), 0)
    c = lax.broadcasted_iota(I32, (tm, tm), 1)
    before = jnp.dot(jnp.where(r > c, 1.0, 0.0).astype(BF16), onehot.astype(BF16),
                     preferred_element_type=F32) + carry[...]
    rank1 = jnp.sum(jnp.where(hot1, before, 0.0), axis=-1, keepdims=True)
    rank2 = jnp.sum(jnp.where(hot2, before, 0.0), axis=-1, keepdims=True)
    carry[...] = carry[...] + jnp.sum(onehot, axis=0, keepdims=True)
    eidx[...] = jnp.where(lane == 0, i1, jnp.where(lane == 1, i2, 0.0)).astype(I32)
    gates[...] = jnp.where(lane == 0, g1, jnp.where(lane == 1, g2, 0.0))
    ranks[...] = jnp.where(lane == 0, rank1, jnp.where(lane == 1, rank2, 0.0)).astype(I32)
    counts[...] = jnp.broadcast_to(carry[...], counts.shape).astype(I32)


def _router(x, rw, tm=512):
    t = x.shape[0]
    rows = pl.BlockSpec((tm, LANES), lambda i: (i, 0))
    return pl.pallas_call(
        functools.partial(_router_body, tm=tm), grid=(t // tm,),
        in_specs=[pl.BlockSpec((tm, D_MODEL), lambda i: (i, 0)),
                  pl.BlockSpec((D_MODEL, LANES), lambda i: (0, 0))],
        out_specs=[rows, rows, rows, pl.BlockSpec((8, LANES), lambda i: (0, 0))],
        out_shape=[jax.ShapeDtypeStruct((t, LANES), I32), jax.ShapeDtypeStruct((t, LANES), F32),
                   jax.ShapeDtypeStruct((t, LANES), I32), jax.ShapeDtypeStruct((8, LANES), I32)],
        scratch_shapes=[pltpu.VMEM((1, LANES), F32)],
        compiler_params=_params("arbitrary"), name="router_top2")(x, rw)


def _moe_body(be, nused, xs, w1, w3, w2, ys, xb, acc):
    i = pl.program_id(0)
    f = pl.program_id(1)
    last = f == pl.num_programs(1) - 1
    used = i < nused[0]

    @pl.when(used & (f == 0))
    def _():
        xb[...] = xs[...].astype(BF16)
        acc[...] = jnp.zeros_like(acc)

    @pl.when(used)
    def _():
        xv = xb[...]
        h1 = jnp.dot(xv, w1[...], preferred_element_type=F32)
        h3 = jnp.dot(xv, w3[...], preferred_element_type=F32)
        hh = (h1 * _sigmoid(h1) * h3).astype(BF16)
        acc[...] += jnp.dot(hh, w2[...], preferred_element_type=F32)

    @pl.when(used & last)
    def _():
        ys[...] = acc[...]

    @pl.when(jnp.logical_not(used) & last)
    def _():
        ys[...] = jnp.zeros_like(ys)


def _moe_experts(block_e, n_used, xs, w1, w3, w2, tm, tf=1792):
    n_slots = xs.shape[0]
    dex = w1.shape[2]
    nblk = n_slots // tm

    def live(i, nu):
        return jnp.minimum(i, nu[0] - 1)

    grid_spec = pltpu.PrefetchScalarGridSpec(
        num_scalar_prefetch=2, grid=(nblk, dex // tf),
        in_specs=[pl.BlockSpec((tm, D_MODEL), lambda i, f, be, nu: (live(i, nu), 0)),
                  pl.BlockSpec((None, D_MODEL, tf), lambda i, f, be, nu: (be[i], 0, jnp.where(i < nu[0], f, dex // tf - 1))),
                  pl.BlockSpec((None, D_MODEL, tf), lambda i, f, be, nu: (be[i], 0, jnp.where(i < nu[0], f, dex // tf - 1))),
                  pl.BlockSpec((None, tf, D_MODEL), lambda i, f, be, nu: (be[i], jnp.where(i < nu[0], f, dex // tf - 1), 0))],
        out_specs=pl.BlockSpec((tm, D_MODEL), lambda i, f, be, nu: (i, 0)),
        scratch_shapes=[pltpu.VMEM((tm, D_MODEL), BF16), pltpu.VMEM((tm, D_MODEL), F32)])
    return pl.pallas_call(
        _moe_body, grid_spec=grid_spec, out_shape=jax.ShapeDtypeStruct((n_slots, D_MODEL), F32),
        compiler_params=_params("arbitrary", "arbitrary"), name="moe_experts")(block_e, n_used, xs, w1, w3, w2)


SUBLANES = 8


def _tile_row(ref3, idx):
    return ref3.at[lax.shift_right_logical(idx, 3), pl.ds(idx & (SUBLANES - 1), 1), :]


def _start_row_gather(src, ids, buf, sem, slot, nrows):
    def issue(j, carry):
        for u in range(SUBLANES):
            idx = ids[0, j * SUBLANES + u]
            pltpu.make_async_copy(_tile_row(src, idx), buf.at[slot, j, pl.ds(u, 1), :], sem.at[slot]).start()
        return carry

    lax.fori_loop(0, nrows // SUBLANES, issue, 0)


def _wait_row_gather(src, buf, sem, slot, nrows):
    pltpu.make_async_copy(src.at[pl.ds(0, nrows // SUBLANES)], buf.at[slot], sem.at[slot]).wait()


def _pipelined_row_gather(src, ids_cur, ids_next, buf, sem, nrows):
    i = pl.program_id(0)
    slot = lax.rem(i, 2)

    @pl.when(i == 0)
    def _():
        _start_row_gather(src, ids_cur, buf, sem, 0, nrows)

    @pl.when(i + 1 < pl.num_programs(0))
    def _():
        _start_row_gather(src, ids_next, buf, sem, 1 - slot, nrows)

    _wait_row_gather(src, buf, sem, slot, nrows)
    return slot


def _dispatch_body(ids, x, init, xs, sem, *, tm):
    del init

    def issue(j, carry):
        for u in range(SUBLANES):
            for k in range(2):
                dst = _tile_row(xs, ids[0, k * tm + j * SUBLANES + u])
                pltpu.make_async_copy(x.at[j, pl.ds(u, 1), :], dst, sem).start()
        return carry

    lax.fori_loop(0, tm // SUBLANES, issue, 0)
    for _ in range(2):
        pltpu.make_async_copy(x, xs.at[pl.ds(0, tm // SUBLANES)], sem).wait()


def _moe_dispatch(x, ids, n_slots, tm):
    t = x.shape[0]
    xs = pl.pallas_call(
        functools.partial(_dispatch_body, tm=tm), grid=(t // tm,),
        in_specs=[pl.BlockSpec((None, 1, 2 * tm), lambda i: (i, 0, 0), memory_space=pltpu.SMEM),
                  pl.BlockSpec((tm // SUBLANES, SUBLANES, D_MODEL), lambda i: (i, 0, 0)),
                  pl.BlockSpec(memory_space=pl.ANY)],
        out_specs=pl.BlockSpec(memory_space=pl.ANY),
        out_shape=jax.ShapeDtypeStruct((n_slots // SUBLANES, SUBLANES, D_MODEL), F32),
        scratch_shapes=[pltpu.SemaphoreType.DMA(())],
        input_output_aliases={2: 0},
        compiler_params=_params("arbitrary"),
        name="moe_dispatch_scatter")(ids, x.reshape(t // SUBLANES, SUBLANES, D_MODEL),
                                     jnp.zeros((n_slots // SUBLANES, SUBLANES, D_MODEL), F32))
    return xs.reshape(n_slots, D_MODEL)


def _combine_ln_body(ids_cur, ids_next, x, ys, gates, g, b, o, buf, sem, *, tm):
    slot = _pipelined_row_gather(ys, ids_cur, ids_next, buf, sem, 2 * tm)
    gt = gates[...]
    nt = tm // SUBLANES
    ya = buf[slot, 0:nt].reshape(tm, D_MODEL)
    yb = buf[slot, nt:].reshape(tm, D_MODEL)
    o[...] = _layer_norm_rows(ALPHA * x[...] + (ya * gt[:, 0:1] + yb * gt[:, 1:2]), g[...], b[...])


def _combine_ln(x, ys, ids, gates, g, b, tm):
    t = x.shape[0]
    nt = t // tm
    smem = functools.partial(pl.BlockSpec, (None, 1, 2 * tm), memory_space=pltpu.SMEM)
    rows = pl.BlockSpec((tm, D_MODEL), lambda i: (i, 0))
    vec = pl.BlockSpec((1, D_MODEL), lambda i: (0, 0))
    return pl.pallas_call(
        functools.partial(_combine_ln_body, tm=tm), grid=(nt,),
        in_specs=[smem(lambda i: (i, 0, 0)), smem(lambda i: (jnp.minimum(i + 1, nt - 1), 0, 0)),
                  rows, pl.BlockSpec(memory_space=pl.ANY),
                  pl.BlockSpec((tm, LANES), lambda i: (i, 0)), vec, vec],
        out_specs=rows, out_shape=jax.ShapeDtypeStruct((t, D_MODEL), F32),
        scratch_shapes=[pltpu.VMEM((2, 2 * tm // SUBLANES, SUBLANES, D_MODEL), F32),
                        pltpu.SemaphoreType.DMA((2,))],
        compiler_params=_params("arbitrary"),
        name="moe_combine_ln")(ids, ids, x, ys.reshape(-1, SUBLANES, D_MODEL), gates, g, b)


def _moe_layer(x, rw_pad, w1, w3, w2, g, b, tm=512):
    t = x.shape[0]
    eidx, gates, ranks, counts = _router(x, rw_pad)
    cnt = counts[0, :N_EXPERTS]
    padded = (cnt + tm - 1) // tm * tm
    pad_end = jnp.cumsum(padded)
    pad_start = pad_end - padded
    e2 = eidx[:, :2]
    dest = pad_start[e2] + ranks[:, :2]
    nblk = (2 * t) // tm + N_EXPERTS
    n_slots = nblk * tm
    tt = 512
    ids = dest.reshape(t // tt, tt, 2).transpose(0, 2, 1).reshape(t // tt, 1, 2 * tt)
    n_used = (pad_end[-1] // tm).astype(I32).reshape(1)
    blk = jnp.arange(nblk, dtype=I32) * tm
    block_e = jnp.minimum(jnp.sum(pad_end[None, :] <= blk[:, None], axis=1), N_EXPERTS - 1).astype(I32)
    block_e = jnp.where(jnp.arange(nblk) < n_used[0], block_e, block_e[jnp.maximum(n_used[0] - 1, 0)])
    xs = _moe_dispatch(x, ids, n_slots, tt)
    ys = _moe_experts(block_e, n_used, xs, w1, w3, w2, tm)
    return _combine_ln(x, ys, ids, gates, g, b, tt)


def _block_diag(w):
    g, c, _ = w.shape
    eye = jnp.eye(g, dtype=w.dtype)
    return (eye[:, None, :, None] * w[:, :, None, :]).reshape(g * c, g * c)


def kernel(x, mem, positions, w_in, conv_w, a_log, dt_bias, dn_norm_w, pool_w, pool_scale, w_out, ln_mix_g, ln_mix_b, xq_w, xk_w, xv_w, xo_w, ln_x_g, ln_x_b, ffn_w1, ffn_w3, ffn_w2, router_w, moe_w1, moe_w3, moe_w2, ln_ffn_g, ln_ffn_b):
    batch, seq, _ = x.shape
    t = batch * seq
    depth = w_in.shape[0]
    xf = x.reshape(t, D_MODEL)
    memf = mem.reshape(batch * mem.shape[1], D_MODEL)

    posc, posr = {}, {}
    for dil in DILATIONS:
        pt = positions.reshape(batch, seq // dil, dil).transpose(0, 2, 1)
        posc[dil] = pt[..., None]
        posr[dil] = pt[:, :, None, :]

    for l in range(depth):
        wl = w_in[l]
        small = jnp.pad(wl[:, OFF_BETA:OFF_GATE], ((0, 0), (0, LANES - 2 * DN_HEADS)))
        w_rest = jnp.concatenate([wl[:, OFF_DN:OFF_BETA], wl[:, OFF_GATE:OFF_POOL], wl[:, OFF_POOL:], small],
                                 axis=1).astype(BF16)
        slabs = _matmul_slabs(xf, wl[:, :OFF_DN].astype(BF16), 512)
        hr = _matmul_conv(xf, w_rest, conv_w[l], seq, 512)

        branches = [_att_branch(slabs, posc[d], posr[d], d, batch, seq) for d in DILATIONS]
        pvec = jnp.zeros((2, LANES), F32)
        pvec = pvec.at[0, DN_HEADS:2 * DN_HEADS].set(a_log[l]).at[1, DN_HEADS:2 * DN_HEADS].set(dt_bias[l])
        normw = jnp.tile(dn_norm_w[l], DN_HEADS)[None, :]
        dn = _delta_net(hr, pvec, normw, batch, seq)

        pool = _pool(hr, _block_diag(pool_w[l]).astype(BF16), pool_scale[l][None, :], batch, seq)

        wo = w_out[l].astype(BF16)
        xf = _proj_ln([o for o, _ in branches], [s for _, s in branches], dn, pool, xf, wo[:ATT_WIDTH],
                      wo[ATT_WIDTH:ATT_WIDTH + DN_WIDTH], wo[ATT_WIDTH + DN_WIDTH:],
                      ln_mix_g[l][None, :], ln_mix_b[l][None, :])

        wkv = jnp.concatenate([xk_w[l], xv_w[l]], axis=1).astype(BF16)
        kv = _matmul(memf, wkv, BF16, 512, 1024).reshape(batch, mem.shape[1], 2 * D_MODEL)
        xf = _cross_attention(xf, (xq_w[l] * (X_HEAD_DIM ** -0.5)).astype(BF16), kv[..., :D_MODEL],
                              kv[..., D_MODEL:], xo_w[l].astype(BF16), ln_x_g[l][None, :],
                              ln_x_b[l][None, :], seq)

        j = l // 2
        gl, bl = ln_ffn_g[l][None, :], ln_ffn_b[l][None, :]
        if l % 2 == 0:
            xf = _ffn_dense(xf, ffn_w1[j].astype(BF16), ffn_w3[j].astype(BF16), ffn_w2[j].astype(BF16), gl, bl)
        else:
            rw_pad = jnp.pad(router_w[j], ((0, 0), (0, LANES - N_EXPERTS)))
            xf = _moe_layer(xf, rw_pad, moe_w1[j].astype(BF16), moe_w3[j].astype(BF16),
                            moe_w2[j].astype(BF16), gl, bl)
    return xf.reshape(batch, seq, D_MODEL)
```

```python
import functools
import math

import jax
import jax.numpy as jnp
from jax import lax
from jax.experimental import pallas as pl
from jax.experimental.pallas import tpu as pltpu

F32, BF16, I32 = jnp.float32, jnp.bfloat16, jnp.int32

D_MODEL = 1024
HEAD_DIM = 64
ATT_HEADS = 6
ATT_WIDTH = ATT_HEADS * HEAD_DIM
DN_HEADS = 6
DN_WIDTH = DN_HEADS * HEAD_DIM
POOL_WIDTH = 256
POOL_WINDOWS = (2, 4, 8, 16)
POOL_GDIM = POOL_WIDTH // len(POOL_WINDOWS)
DILATIONS = (1, 4, 16)
CONV_K = 4
X_HEADS = 4
X_HEAD_DIM = D_MODEL // X_HEADS
N_EXPERTS = 8
DEPTH = 4
ALPHA = (2 * DEPTH) ** 0.25
LN_EPS = 1e-5
NORM_EPS = 1e-6

OFF_DN = 3 * ATT_WIDTH
OFF_BETA = OFF_DN + 3 * DN_WIDTH
OFF_DECAY = OFF_BETA + DN_HEADS
OFF_GATE = OFF_DECAY + DN_HEADS
OFF_POOL = OFF_GATE + DN_WIDTH

LANES = 128
MXU_WIDTH = 256
R_DN = 0
R_GATE = R_DN + 3 * DN_WIDTH
R_POOL = R_GATE + DN_WIDTH
R_SMALL = R_POOL + POOL_WIDTH
R_COLS = R_SMALL + LANES

QB = 128
NEG = -1e30
VMEM_LIMIT = 48 * 1024 * 1024


def _params(*sem):
    return pltpu.CompilerParams(dimension_semantics=sem, vmem_limit_bytes=VMEM_LIMIT)


def _sigmoid(x):
    return 1.0 / (1.0 + jnp.exp(-x))


def _layer_norm_rows(z, g, b):
    mu = jnp.mean(z, axis=-1, keepdims=True)
    zc = z - mu
    var = jnp.mean(zc * zc, axis=-1, keepdims=True)
    return zc * lax.rsqrt(var + LN_EPS) * g + b


def _rows(start, size, stride):
    return pl.ds(start, size) if stride == 1 else pl.ds(start, size, stride=stride)


def _mm_body(x_ref, w_ref, o_ref):
    o_ref[...] = jnp.dot(x_ref[...].astype(BF16), w_ref[...],
                         preferred_element_type=F32).astype(o_ref.dtype)


def _matmul(x, w, out_dtype, tm, tn):
    m, k = x.shape
    n = w.shape[1]
    return pl.pallas_call(
        _mm_body, grid=(m // tm, n // tn),
        in_specs=[pl.BlockSpec((tm, k), lambda i, j: (i, 0)),
                  pl.BlockSpec((k, tn), lambda i, j: (0, j))],
        out_specs=pl.BlockSpec((tm, tn), lambda i, j: (i, j)),
        out_shape=jax.ShapeDtypeStruct((m, n), out_dtype),
        compiler_params=_params("parallel", "arbitrary"), name="matmul")(x, w)


CONV_HALO = 8


def _in_proj_body(x_ref, w_ref, cw_ref, slab_ref, o_ref, wb, hist, cbuf, *, tm, tiles_per_seq):
    i = pl.program_id(0)
    ncv = 3 * DN_WIDTH

    @pl.when(i == 0)
    def _():
        wb[:, 0:OFF_BETA] = w_ref[:, 0:OFF_BETA].astype(BF16)
        c0 = OFF_DN + R_GATE
        wb[:, c0:c0 + DN_WIDTH] = w_ref[:, OFF_GATE:OFF_POOL].astype(BF16)
        c0 = OFF_DN + R_POOL
        wb[:, c0:c0 + POOL_WIDTH] = w_ref[:, OFF_POOL:].astype(BF16)
        c0 = OFF_DN + R_SMALL
        wb[:, c0:c0 + LANES] = jnp.zeros((wb.shape[0], LANES), BF16)
        wb[:, c0:c0 + 2 * DN_HEADS] = w_ref[:, OFF_BETA:OFF_GATE].astype(BF16)

    @pl.when(lax.rem(i, tiles_per_seq) == 0)
    def _():
        hist[...] = jnp.zeros_like(hist)

    xb = x_ref[...].astype(BF16)
    for j in range(slab_ref.shape[0]):
        if j % 2 == 0:
            res = jnp.dot(xb, wb[:, j * LANES:min(j + 2, slab_ref.shape[0]) * LANES],
                          preferred_element_type=F32)
        slab_ref[j] = res[:, (j % 2) * LANES:(j % 2 + 1) * LANES]

    n = o_ref.shape[1]
    starts = list(range(0, ncv, MXU_WIDTH)) + list(range(ncv, n, MXU_WIDTH))
    for c0, c1 in zip(starts, starts[1:] + [n]):
        cs = slice(c0, c1)
        res = jnp.dot(xb, wb[:, OFF_DN + c0:OFF_DN + c1], preferred_element_type=F32)
        if c0 >= ncv:
            o_ref[:, cs] = res
            continue
        cbuf[0:CONV_HALO, cs] = hist[:, cs]
        cbuf[CONV_HALO:, cs] = res
        hist[:, cs] = cbuf[tm:, cs]
        acc = None
        for j in range(CONV_K):
            start = CONV_HALO - (CONV_K - 1) + j
            t = cbuf[start:start + tm, cs] * cw_ref[j:j + 1, cs]
            acc = t if acc is None else acc + t
        o_ref[:, cs] = acc * _sigmoid(acc)


def _in_proj(x, w_in, layer, conv_w, seq, tm=512):
    m, k = x.shape
    nslab = OFF_DN // LANES
    return pl.pallas_call(
        functools.partial(_in_proj_body, tm=tm, tiles_per_seq=seq // tm), grid=(m // tm,),
        in_specs=[pl.BlockSpec((tm, k), lambda i: (i, 0)),
                  pl.BlockSpec((None,) + w_in.shape[1:], lambda i: (layer, 0, 0),
                               pipeline_mode=pl.Buffered(1)),
                  pl.BlockSpec(conv_w.shape, lambda i: (0, 0))],
        out_specs=[pl.BlockSpec((nslab, tm, LANES), lambda i: (0, i, 0)),
                   pl.BlockSpec((tm, R_COLS), lambda i: (i, 0))],
        out_shape=[jax.ShapeDtypeStruct((nslab, m, LANES), F32),
                   jax.ShapeDtypeStruct((m, R_COLS), F32)],
        scratch_shapes=[pltpu.VMEM((k, OFF_DN + R_COLS), BF16),
                        pltpu.VMEM((CONV_HALO, 3 * DN_WIDTH), F32),
                        pltpu.VMEM((tm + CONV_HALO, 3 * DN_WIDTH), F32)],
        compiler_params=_params("arbitrary"), name="in_proj_conv")(x, w_in, conv_w)


def _att_body(q_ref, kc_ref, vc_ref, pq_ref, pkc_ref, o_ref, lse_ref,
              kbuf, vbuf, pkbuf, kprev, vprev, pkprev, *, dil, rows, slopes):
    n = pl.program_id(1)
    nsub = rows // QB
    npair = ATT_HEADS // 2

    @pl.when(n == 0)
    def _():
        kprev[...] = jnp.zeros_like(kprev)
        vprev[...] = jnp.zeros_like(vprev)
        pkprev[...] = jnp.zeros_like(pkprev)

    a = lax.broadcasted_iota(I32, (QB, 2 * QB), 0)
    c = lax.broadcasted_iota(I32, (QB, 2 * QB), 1)
    tri = jnp.where(c < QB, c - a, a - (c - QB)) >= 0
    lane = lax.broadcasted_iota(I32, (QB, LANES), 1)
    low = lane < HEAD_DIM

    def residue(r, carry):
        for p in range(npair):
            kbuf[p, 0:QB, :] = kprev[r, p]
            kbuf[p, QB:, :] = kc_ref[p, _rows(r, rows, dil), :].astype(BF16)
            vbuf[p, 0:QB, :] = vprev[r, p]
            vbuf[p, QB:, :] = vc_ref[p, _rows(r, rows, dil), :].astype(BF16)
            kprev[r, p] = kbuf[p, rows:, :]
            vprev[r, p] = vbuf[p, rows:, :]
        pkbuf[0] = pkprev[r]
        for j in range(nsub):
            pkbuf[j + 1] = pkc_ref[r, :, j * QB:(j + 1) * QB]
        pkprev[r] = pkbuf[nsub]

        def sub(sb, carry2):
            r0 = pl.multiple_of(sb * QB, QB)
            t0 = r + r0 * dil
            pq = pq_ref[r, pl.ds(r0, QB), :]
            pk = jnp.concatenate([pkbuf[sb], pkbuf[sb + 1]], axis=1)
            dist = (pq - pk).astype(F32)
            mask = tri & (c >= jnp.where((sb > 0) | (n > 0), 0, QB))
            heads = range(ATT_HEADS)
            ks = [kbuf[p, pl.ds(r0, 2 * QB), :] for p in range(npair)]
            vs = [vbuf[p, pl.ds(r0, 2 * QB), :] for p in range(npair)]
            qs = [q_ref[p, _rows(t0, QB, dil), :] * (HEAD_DIM ** -0.5) for p in range(npair)]
            qms = [jnp.where(low if h % 2 == 0 else ~low, qs[h // 2], 0.0).astype(BF16) for h in heads]
            ss = [lax.dot_general(qms[h], ks[h // 2], (((1,), (1,)), ((), ())), preferred_element_type=F32)
                  for h in heads]
            ss = [jnp.where(mask, ss[h] - slopes[h] * dist, NEG) for h in heads]
            ms = [jnp.max(ss[h], axis=-1, keepdims=True) for h in heads]
            es = [jnp.exp(ss[h] - ms[h]) for h in heads]
            ls = [jnp.sum(es[h], axis=-1, keepdims=True) for h in heads]
            outs = [jnp.dot(es[h].astype(BF16), vs[h // 2], preferred_element_type=F32) / ls[h] for h in heads]
            lse = jnp.zeros((QB, LANES), F32)
            for h in heads:
                lse = jnp.where(lane == h, ms[h] + jnp.log(ls[h]), lse)
            for p in range(npair):
                o_ref[p, _rows(t0, QB, dil), :] = jnp.where(low, outs[2 * p], outs[2 * p + 1])
            lse_ref[_rows(t0, QB, dil), :] = lse
            return carry2

        lax.fori_loop(0, nsub, sub, 0, unroll=min(nsub, 2))
        return carry

    lax.fori_loop(0, dil, residue, 0)


def _att_branch(slabs, posc, posr, dil, batch, seq):
    sub_len = seq // dil
    rows = min(512 if dil == 1 else 2048 // dil, sub_len)
    tok = rows * dil
    nsteps = seq // tok
    npair = ATT_HEADS // 2
    slopes = tuple(2.0 ** (-8.0 * (i + 1) / ATT_HEADS) for i in range(ATT_HEADS))

    def cur(which):
        return pl.BlockSpec((npair, tok, LANES), lambda b, n: (which, b * nsteps + n, 0))

    t = batch * seq
    return pl.pallas_call(
        functools.partial(_att_body, dil=dil, rows=rows, slopes=slopes),
        grid=(batch, nsteps),
        in_specs=[cur(0), cur(1), cur(2),
                  pl.BlockSpec((None, dil, rows, 1), lambda b, n: (b, 0, n, 0)),
                  pl.BlockSpec((None, dil, 1, rows), lambda b, n: (b, 0, 0, n))],
        out_specs=[pl.BlockSpec((npair, tok, LANES), lambda b, n: (0, b * nsteps + n, 0)),
                   pl.BlockSpec((tok, LANES), lambda b, n: (b * nsteps + n, 0))],
        out_shape=[jax.ShapeDtypeStruct((npair, t, LANES), F32),
                   jax.ShapeDtypeStruct((t, LANES), F32)],
        scratch_shapes=[pltpu.VMEM((npair, rows + QB, LANES), BF16),
                        pltpu.VMEM((npair, rows + QB, LANES), BF16),
                        pltpu.VMEM((rows // QB + 1, 1, QB), I32),
                        pltpu.VMEM((dil, npair, QB, LANES), BF16),
                        pltpu.VMEM((dil, npair, QB, LANES), BF16),
                        pltpu.VMEM((dil, 1, QB), I32)],
        compiler_params=_params("arbitrary", "arbitrary"),
        name=f"dilated_att_d{dil}")(slabs, slabs, slabs, posc, posr)


def _merge_branches(os_, ls_):
    ls = [l[...] for l in ls_]
    m = functools.reduce(jnp.maximum, ls)
    ws = [jnp.exp(l - m) for l in ls]
    inv = 1.0 / functools.reduce(lambda a, b: a + b, ws)
    ws = [w * inv for w in ws]
    lane = lax.broadcasted_iota(I32, ls[0].shape, 1)
    low = lane < HEAD_DIM
    pairs = []
    for p in range(ATT_HEADS // 2):
        acc = None
        for w, o in zip(ws, os_):
            wp = jnp.where(low, w[:, 2 * p:2 * p + 1], w[:, 2 * p + 1:2 * p + 2])
            t = wp * o[p]
            acc = t if acc is None else acc + t
        pairs.append(acc)
    return jnp.concatenate(pairs, axis=1)


def _split2(a):
    hi = a.astype(BF16)
    lo = (a - hi.astype(F32)).astype(BF16)
    return hi, lo


def _dot(a, b):
    return jnp.dot(a, b, preferred_element_type=F32)


def _dot_bf16(a, b):
    return _dot(a.astype(BF16), b.astype(BF16))


def _dot_hp(a, b):
    ah, al = _split2(a)
    bh, bl = _split2(b)
    return _dot(jnp.concatenate([ah, al, ah], axis=1), jnp.concatenate([bh, bh, bl], axis=0))


def _dot_t(a, b):
    return lax.dot_general(a, b, (((1,), (1,)), ((), ())), preferred_element_type=F32)


def _dot_t_hp(a, b):
    ah, al = _split2(a)
    bh, bl = _split2(b)
    return _dot_t(jnp.concatenate([ah, al, ah], axis=1), jnp.concatenate([bh, bh, bl], axis=1))


def _dn_body(hq, hk, hv, small, gate, pvec, normw, out, s_ref, *, nch):
    n = pl.program_id(1)
    c_ = QB
    npair = DN_HEADS // 2
    qkv = (hq, hk, hv)

    @pl.when(n == 0)
    def _():
        s_ref[...] = jnp.zeros_like(s_ref)

    lane = lax.broadcasted_iota(I32, (c_, LANES), 1)
    low = lane < HEAD_DIM
    row = lax.broadcasted_iota(I32, (c_, c_), 0)
    col = lax.broadcasted_iota(I32, (c_, c_), 1)
    lower = row >= col
    strict = row > col
    eye = jnp.where(row == col, 1.0, 0.0)
    blockdiag = (row < HEAD_DIM) == (col < HEAD_DIM)

    def headsum(x):
        sa = jnp.sum(jnp.where(low, x, 0.0), axis=-1, keepdims=True)
        sb = jnp.sum(jnp.where(low, 0.0, x), axis=-1, keepdims=True)
        return jnp.where(low, sa, sb)

    def col_of(arr, j):
        return arr[:, j:j + 1]

    ltri = jnp.where(lower, 1.0, 0.0).astype(BF16)
    ltri3 = jnp.concatenate([ltri, ltri, ltri], axis=1)
    pair, xs, pws, qks = [], [], [], []
    for ch in range(nch):
        rs = slice(ch * c_, (ch + 1) * c_)
        sm = small[rs, :]
        beta_all = _sigmoid(sm)
        z = sm + pvec[1:2, :]
        softplus = jnp.maximum(z, 0.0) + jnp.log(1.0 + jnp.exp(-jnp.abs(z)))
        g_all = -jnp.exp(pvec[0:1, :]) * softplus
        g1 = g_all.astype(BF16)
        rem = g_all - g1.astype(F32)
        g2 = rem.astype(BF16)
        g3 = (rem - g2.astype(F32)).astype(BF16)
        gc_all = _dot(ltri3, jnp.concatenate([g1, g2, g3], axis=0))
        gc_t = gc_all.T

        for p in range(npair):
            cs = slice(p * LANES, (p + 1) * LANES)
            q, k, v = qkv[0][rs, cs], qkv[1][rs, cs], qkv[2][rs, cs]
            q = q * lax.rsqrt(headsum(q * q) + NORM_EPS) * (HEAD_DIM ** -0.5)
            k = k * lax.rsqrt(headsum(k * k) + NORM_EPS)
            beta = jnp.where(low, col_of(beta_all, 2 * p), col_of(beta_all, 2 * p + 1))
            gc = jnp.where(low, col_of(gc_all, DN_HEADS + 2 * p), col_of(gc_all, DN_HEADS + 2 * p + 1))
            eg = jnp.exp(gc)
            rhs = jnp.concatenate([v * beta, k * beta * eg], axis=1)
            pair.append(dict(q=q, k=k, kb=k.astype(BF16), gc=gc, eg=eg, rhs=rhs))

        for h in range(DN_HEADS):
            pr = pair[ch * npair + h // 2]
            sel = low if h % 2 == 0 else ~low
            gcol = col_of(gc_all, DN_HEADS + h)
            grow = gc_t[DN_HEADS + h:DN_HEADS + h + 1, :]
            dec = jnp.where(lower, jnp.exp(jnp.minimum(gcol - grow, 0.0)), 0.0)
            kk = _dot_t(jnp.where(sel, pr["k"], 0.0).astype(BF16), pr["kb"])
            qk = _dot_t_hp(jnp.where(sel, pr["q"], 0.0), pr["k"])
            nm = jnp.where(strict, col_of(beta_all, h) * kk * dec, 0.0)
            xs.append(eye - nm)
            pws.append(nm)
            qks.append((qk * dec).astype(BF16))

    for level in range(int(math.log2(c_)) - 1):
        mm = _dot_hp if level < 3 else _dot_bf16
        for i in range(len(xs)):
            pws[i] = mm(pws[i], pws[i])
        for i in range(len(xs)):
            xs[i] = xs[i] + mm(xs[i], pws[i])
    uws = [_dot_hp(xs[i], pair[i // 2]["rhs"]) for i in range(len(xs))]

    for ch in range(nch):
        rs = slice(ch * c_, (ch + 1) * c_)
        for p in range(npair):
            pr = pair[ch * npair + p]
            i0 = ch * DN_HEADS + 2 * p
            cs = slice(p * LANES, (p + 1) * LANES)
            u = jnp.where(low, uws[i0][:, :LANES], uws[i0 + 1][:, :LANES])
            w = jnp.where(low, uws[i0][:, LANES:], uws[i0 + 1][:, LANES:])
            state = s_ref[p]
            sb16 = state.astype(BF16)
            vnew = u - _dot(w.astype(BF16), sb16)
            vn16 = vnew.astype(BF16)
            o = _dot((pr["q"] * pr["eg"]).astype(BF16), sb16)
            o = o + jnp.where(low, _dot(qks[i0], vn16), _dot(qks[i0 + 1], vn16))
            glast = pr["gc"][c_ - 1:c_, :]
            kd = pr["k"] * jnp.exp(glast - pr["gc"])
            upd = _dot(kd.T.astype(BF16), vn16)
            s_ref[p] = state * jnp.exp(glast) + jnp.where(blockdiag, upd, 0.0)
            ms = headsum(o * o) * (1.0 / HEAD_DIM)
            gt = gate[rs, cs]
            out[rs, cs] = (o * lax.rsqrt(ms + NORM_EPS) * normw[:, cs] * (gt * _sigmoid(gt))).astype(out.dtype)


def _delta_net(hr, pvec, normw, batch, seq, nch=2):
    h3 = hr.reshape(batch, seq, R_COLS)
    rows = nch * QB
    qcol = R_DN // DN_WIDTH

    def cur(off):
        return pl.BlockSpec((None, rows, DN_WIDTH), lambda b, n: (b, n, qcol + off))

    out = pl.pallas_call(
        functools.partial(_dn_body, nch=nch), grid=(batch, seq // rows),
        in_specs=[cur(0), cur(1), cur(2),
                  pl.BlockSpec((None, rows, LANES), lambda b, n: (b, n, R_SMALL // LANES)),
                  pl.BlockSpec((None, rows, DN_WIDTH), lambda b, n: (b, n, R_GATE // DN_WIDTH)),
                  pl.BlockSpec((2, LANES), lambda b, n: (0, 0)),
                  pl.BlockSpec((1, DN_WIDTH), lambda b, n: (0, 0))],
        out_specs=pl.BlockSpec((None, rows, DN_WIDTH), lambda b, n: (b, n, 0)),
        out_shape=jax.ShapeDtypeStruct((batch, seq, DN_WIDTH), BF16),
        scratch_shapes=[pltpu.VMEM((DN_HEADS // 2, LANES, LANES), F32)],
        compiler_params=_params("parallel", "arbitrary"),
        name="gated_delta_rule")(h3, h3, h3, h3, h3, pvec, normw)
    return out.reshape(batch * seq, DN_WIDTH)


def _pool_body(cur, prev, wbd, scale, out, xbuf, *, rows):
    n = pl.program_id(1)
    halo = 16
    xbuf[0:halo, :] = prev[...] * jnp.where(n > 0, 1.0, 0.0)
    xbuf[halo:, :] = cur[...]
    u = cur[...]
    lane = lax.broadcasted_iota(I32, (rows, POOL_WIDTH), 1)
    tpos = n * rows + lax.broadcasted_iota(I32, (rows, POOL_WIDTH), 0)
    run = u
    pooled = jnp.zeros_like(u)
    for j in range(1, max(POOL_WINDOWS)):
        run = run + xbuf[halo - j:halo - j + rows, :]
        if j + 1 in POOL_WINDOWS:
            gi = POOL_WINDOWS.index(j + 1)
            cnt = jnp.minimum(tpos + 1, j + 1).astype(F32)
            sel = (lane >= gi * POOL_GDIM) & (lane < (gi + 1) * POOL_GDIM)
            pooled = jnp.where(sel, run / cnt - u, pooled)
    mixed = jnp.dot(pooled.astype(BF16), wbd[...], preferred_element_type=F32)
    out[...] = (mixed * scale[...]).astype(out.dtype)


def _pool(hr, wbd, scale, batch, seq, rows=512):
    h3 = hr.reshape(batch, seq, R_COLS)
    out = pl.pallas_call(
        functools.partial(_pool_body, rows=rows), grid=(batch, seq // rows),
        in_specs=[pl.BlockSpec((None, rows, POOL_WIDTH), lambda b, n: (b, n, R_POOL // POOL_WIDTH)),
                  pl.BlockSpec((None, 16, POOL_WIDTH),
                               lambda b, n: (b, jnp.maximum(n * (rows // 16) - 1, 0), R_POOL // POOL_WIDTH)),
                  pl.BlockSpec((POOL_WIDTH, POOL_WIDTH), lambda b, n: (0, 0)),
                  pl.BlockSpec((1, POOL_WIDTH), lambda b, n: (0, 0))],
        out_specs=pl.BlockSpec((None, rows, POOL_WIDTH), lambda b, n: (b, n, 0)),
        out_shape=jax.ShapeDtypeStruct((batch, seq, POOL_WIDTH), BF16),
        scratch_shapes=[pltpu.VMEM((rows + 16, POOL_WIDTH), F32)],
        compiler_params=_params("parallel", "arbitrary"), name="multiscale_pool")(h3, h3, wbd, scale)
    return out.reshape(batch * seq, POOL_WIDTH)


def _proj_ln_body(o1, o2, o3, l1, l2, l3, d, p, x, wa, wd, wp, g, b, o):
    att = _merge_branches((o1, o2, o3), (l1, l2, l3)).astype(BF16)
    y = jnp.dot(att, wa[...], preferred_element_type=F32)
    y = y + jnp.dot(d[...], wd[...], preferred_element_type=F32)
    y = y + jnp.dot(p[...], wp[...], preferred_element_type=F32)
    o[...] = _layer_norm_rows(ALPHA * x[...] + y, g[...], b[...])


def _proj_ln(branch_o, branch_lse, dn, pool, x, wa, wd, wp, g, b, tm=512):
    t = x.shape[0]
    npair = ATT_HEADS // 2

    def rows(wd_):
        return pl.BlockSpec((tm, wd_), lambda i: (i, 0))

    def full(a_):
        return pl.BlockSpec(a_.shape, lambda i: (0, 0))

    slab = pl.BlockSpec((npair, tm, LANES), lambda i: (0, i, 0))
    return pl.pallas_call(
        _proj_ln_body, grid=(t // tm,),
        in_specs=[slab] * 3 + [rows(LANES)] * 3 + [rows(DN_WIDTH), rows(POOL_WIDTH), rows(D_MODEL),
                                                    full(wa), full(wd), full(wp), full(g), full(b)],
        out_specs=rows(D_MODEL), out_shape=jax.ShapeDtypeStruct((t, D_MODEL), F32),
        compiler_params=_params("parallel"),
        name="out_proj_ln")(*branch_o, *branch_lse, dn, pool, x, wa, wd, wp, g, b)


def _xattn_body(x, wq, k, v, wo, g, b, o):
    xv = x[...]
    q = jnp.dot(xv.astype(BF16), wq[...], preferred_element_type=F32).astype(BF16)
    cols = [slice(h * X_HEAD_DIM, (h + 1) * X_HEAD_DIM) for h in range(X_HEADS)]
    ss = [_dot_t(q[:, cs], k[:, cs]) for cs in cols]
    ms = [jnp.max(s, axis=-1, keepdims=True) for s in ss]
    es = [jnp.exp(s - m) for s, m in zip(ss, ms)]
    ls = [jnp.sum(e, axis=-1, keepdims=True) for e in es]
    heads = [(_dot(e.astype(BF16), v[:, cs]) / l).astype(BF16) for e, l, cs in zip(es, ls, cols)]
    oc = jnp.concatenate(heads, axis=1)
    y = jnp.dot(oc, wo[...], preferred_element_type=F32)
    o[...] = _layer_norm_rows(ALPHA * xv + y, g[...], b[...])


def _cross_attention(x, wq, k, v, wo, g, b, seq, tm=1024):
    t = x.shape[0]
    mem_len = k.shape[1]
    per_batch = seq // tm

    def full(a_):
        return pl.BlockSpec(a_.shape, lambda i: (0, 0))

    kv = pl.BlockSpec((None, mem_len, D_MODEL), lambda i: (i // per_batch, 0, 0))
    rows = pl.BlockSpec((tm, D_MODEL), lambda i: (i, 0))
    return pl.pallas_call(
        _xattn_body, grid=(t // tm,),
        in_specs=[rows, full(wq), kv, kv, full(wo), full(g), full(b)],
        out_specs=rows, out_shape=jax.ShapeDtypeStruct((t, D_MODEL), F32),
        compiler_params=_params("parallel"), name="cross_attention_ln")(x, wq, k, v, wo, g, b)


def _ffn_body(x, w1, w3, w2, g, b, o, *, chunks):
    xv = x[...]
    xb = xv.astype(BF16)
    y = None
    for c0, c1 in chunks:
        h1 = jnp.dot(xb, w1[:, c0:c1], preferred_element_type=F32)
        h3 = jnp.dot(xb, w3[:, c0:c1], preferred_element_type=F32)
        hh = (h1 * _sigmoid(h1) * h3).astype(BF16)
        t = jnp.dot(hh, w2[c0:c1, :], preferred_element_type=F32)
        y = t if y is None else y + t
    o[...] = _layer_norm_rows(ALPHA * xv + y, g[...], b[...])


def _ffn_dense(x, w1, w3, w2, g, b, tm=512, chunk=6 * MXU_WIDTH):
    t = x.shape[0]
    dff = w1.shape[1]
    chunks = tuple((c0, min(c0 + chunk, dff)) for c0 in range(0, dff, chunk))
    rows = pl.BlockSpec((tm, D_MODEL), lambda i: (i, 0))
    vec = pl.BlockSpec((1, D_MODEL), lambda i: (0, 0))

    def resident(a_):
        return pl.BlockSpec(a_.shape, lambda i: (0, 0), pipeline_mode=pl.Buffered(1))

    return pl.pallas_call(
        functools.partial(_ffn_body, chunks=chunks), grid=(t // tm,),
        in_specs=[rows, resident(w1), resident(w3), resident(w2), vec, vec],
        out_specs=rows, out_shape=jax.ShapeDtypeStruct((t, D_MODEL), F32),
        compiler_params=_params("parallel"), name="ffn_swiglu_ln")(x, w1, w3, w2, g, b)


def _router_body(x, rw, eidx, gates, ranks, counts, carry, *, tm):
    i = pl.program_id(0)

    @pl.when(i == 0)
    def _():
        carry[...] = jnp.zeros_like(carry)

    logits = _dot_hp(x[...], rw[...])
    lane = lax.broadcasted_iota(I32, (tm, LANES), 1)
    lanef = lane.astype(F32)
    lg = jnp.where(lane < N_EXPERTS, logits, NEG)
    m1 = jnp.max(lg, axis=-1, keepdims=True)
    i1 = jnp.min(jnp.where(lg == m1, lanef, float(LANES)), axis=-1, keepdims=True)
    lg2 = jnp.where(lanef == i1, NEG, lg)
    m2 = jnp.max(lg2, axis=-1, keepdims=True)
    i2 = jnp.min(jnp.where(lg2 == m2, lanef, float(LANES)), axis=-1, keepdims=True)
    e21 = jnp.exp(m2 - m1)
    g1 = 1.0 / (1.0 + e21)
    g2 = e21 * g1
    hot1 = lanef == i1
    hot2 = lanef == i2
    onehot = jnp.where(hot1 | hot2, 1.0, 0.0)
    r = lax.broadcasted_iota(I32, (tm, tm), 0)
    c = lax.broadcasted_iota(I32, (tm, tm), 1)
    before = jnp.dot(jnp.where(r > c, 1.0, 0.0).astype(BF16), onehot.astype(BF16),
                     preferred_element_type=F32) + carry[...]
    rank1 = jnp.sum(jnp.where(hot1, before, 0.0), axis=-1, keepdims=True)
    rank2 = jnp.sum(jnp.where(hot2, before, 0.0), axis=-1, keepdims=True)
    carry[...] = carry[...] + jnp.sum(onehot, axis=0, keepdims=True)
    eidx[...] = jnp.where(lane == 0, i1, jnp.where(lane == 1, i2, 0.0)).astype(I32)
    gates[...] = jnp.where(lane == 0, g1, jnp.where(lane == 1, g2, 0.0))
    ranks[...] = jnp.where(lane == 0, rank1, jnp.where(lane == 1, rank2, 0.0)).astype(I32)
    counts[...] = jnp.broadcast_to(carry[...], counts.shape).astype(I32)


def _router(x, rw, tm=512):
    t = x.shape[0]
    rows = pl.BlockSpec((tm, LANES), lambda i: (i, 0))
    return pl.pallas_call(
        functools.partial(_router_body, tm=tm), grid=(t // tm,),
        in_specs=[pl.BlockSpec((tm, D_MODEL), lambda i: (i, 0)),
                  pl.BlockSpec((D_MODEL, LANES), lambda i: (0, 0))],
        out_specs=[rows, rows, rows, pl.BlockSpec((8, LANES), lambda i: (0, 0))],
        out_shape=[jax.ShapeDtypeStruct((t, LANES), I32), jax.ShapeDtypeStruct((t, LANES), F32),
                   jax.ShapeDtypeStruct((t, LANES), I32), jax.ShapeDtypeStruct((8, LANES), I32)],
        scratch_shapes=[pltpu.VMEM((1, LANES), F32)],
        compiler_params=_params("arbitrary"), name="router_top2")(x, rw)


def _moe_body(be, nused, xs, w1, w3, w2, ys, xb, acc):
    i = pl.program_id(0)
    f = pl.program_id(1)
    last = f == pl.num_programs(1) - 1
    used = i < nused[0]

    @pl.when(used & (f == 0))
    def _():
        xb[...] = xs[...].astype(BF16)
        acc[...] = jnp.zeros_like(acc)

    @pl.when(used)
    def _():
        xv = xb[...]
        h1 = jnp.dot(xv, w1[...], preferred_element_type=F32)
        h3 = jnp.dot(xv, w3[...], preferred_element_type=F32)
        hh = (h1 * _sigmoid(h1) * h3).astype(BF16)
        acc[...] += jnp.dot(hh, w2[...], preferred_element_type=F32)

    @pl.when(used & last)
    def _():
        ys[...] = acc[...]

    @pl.when(jnp.logical_not(used) & last)
    def _():
        ys[...] = jnp.zeros_like(ys)


def _moe_experts(block_e, n_used, xs, w1, w3, w2, tm, tf=1792):
    n_slots = xs.shape[0]
    dex = w1.shape[2]
    nblk = n_slots // tm

    def live(i, nu):
        return jnp.minimum(i, nu[0] - 1)

    grid_spec = pltpu.PrefetchScalarGridSpec(
        num_scalar_prefetch=2, grid=(nblk, dex // tf),
        in_specs=[pl.BlockSpec((tm, D_MODEL), lambda i, f, be, nu: (live(i, nu), 0)),
                  pl.BlockSpec((None, D_MODEL, tf), lambda i, f, be, nu: (be[i], 0, jnp.where(i < nu[0], f, dex // tf - 1))),
                  pl.BlockSpec((None, D_MODEL, tf), lambda i, f, be, nu: (be[i], 0, jnp.where(i < nu[0], f, dex // tf - 1))),
                  pl.BlockSpec((None, tf, D_MODEL), lambda i, f, be, nu: (be[i], jnp.where(i < nu[0], f, dex // tf - 1), 0))],
        out_specs=pl.BlockSpec((tm, D_MODEL), lambda i, f, be, nu: (i, 0)),
        scratch_shapes=[pltpu.VMEM((tm, D_MODEL), BF16), pltpu.VMEM((tm, D_MODEL), F32)])
    return pl.pallas_call(
        _moe_body, grid_spec=grid_spec, out_shape=jax.ShapeDtypeStruct((n_slots, D_MODEL), F32),
        compiler_params=_params("arbitrary", "arbitrary"), name="moe_experts")(block_e, n_used, xs, w1, w3, w2)


SUBLANES = 8


def _tile_row(ref3, idx):
    return ref3.at[lax.shift_right_logical(idx, 3), pl.ds(idx & (SUBLANES - 1), 1), :]


def _start_row_gather(src, ids, buf, sem, slot, nrows):
    def issue(j, carry):
        for u in range(SUBLANES):
            idx = ids[0, j * SUBLANES + u]
            pltpu.make_async_copy(_tile_row(src, idx), buf.at[slot, j, pl.ds(u, 1), :], sem.at[slot]).start()
        return carry

    lax.fori_loop(0, nrows // SUBLANES, issue, 0)


def _wait_row_gather(src, buf, sem, slot, nrows):
    pltpu.make_async_copy(src.at[pl.ds(0, nrows // SUBLANES)], buf.at[slot], sem.at[slot]).wait()


def _pipelined_row_gather(src, ids_cur, ids_next, buf, sem, nrows):
    i = pl.program_id(0)
    slot = lax.rem(i, 2)

    @pl.when(i == 0)
    def _():
        _start_row_gather(src, ids_cur, buf, sem, 0, nrows)

    @pl.when(i + 1 < pl.num_programs(0))
    def _():
        _start_row_gather(src, ids_next, buf, sem, 1 - slot, nrows)

    _wait_row_gather(src, buf, sem, slot, nrows)
    return slot


def _dispatch_body(ids, x, init, xs, sem, *, tm):
    del init

    def issue(j, carry):
        for u in range(SUBLANES):
            for k in range(2):
                dst = _tile_row(xs, ids[0, k * tm + j * SUBLANES + u])
                pltpu.make_async_copy(x.at[j, pl.ds(u, 1), :], dst, sem).start()
        return carry

    lax.fori_loop(0, tm // SUBLANES, issue, 0)
    for _ in range(2):
        pltpu.make_async_copy(x, xs.at[pl.ds(0, tm // SUBLANES)], sem).wait()


def _moe_dispatch(x, ids, n_slots, tm):
    t = x.shape[0]
    xs = pl.pallas_call(
        functools.partial(_dispatch_body, tm=tm), grid=(t // tm,),
        in_specs=[pl.BlockSpec((None, 1, 2 * tm), lambda i: (i, 0, 0), memory_space=pltpu.SMEM),
                  pl.BlockSpec((tm // SUBLANES, SUBLANES, D_MODEL), lambda i: (i, 0, 0)),
                  pl.BlockSpec(memory_space=pl.ANY)],
        out_specs=pl.BlockSpec(memory_space=pl.ANY),
        out_shape=jax.ShapeDtypeStruct((n_slots // SUBLANES, SUBLANES, D_MODEL), F32),
        scratch_shapes=[pltpu.SemaphoreType.DMA(())],
        input_output_aliases={2: 0},
        compiler_params=_params("arbitrary"),
        name="moe_dispatch_scatter")(ids, x.reshape(t // SUBLANES, SUBLANES, D_MODEL),
                                     jnp.zeros((n_slots // SUBLANES, SUBLANES, D_MODEL), F32))
    return xs.reshape(n_slots, D_MODEL)


def _combine_ln_body(ids_cur, ids_next, x, ys, gates, g, b, o, buf, sem, *, tm):
    slot = _pipelined_row_gather(ys, ids_cur, ids_next, buf, sem, 2 * tm)
    gt = gates[...]
    nt = tm // SUBLANES
    ya = buf[slot, 0:nt].reshape(tm, D_MODEL)
    yb = buf[slot, nt:].reshape(tm, D_MODEL)
    o[...] = _layer_norm_rows(ALPHA * x[...] + (ya * gt[:, 0:1] + yb * gt[:, 1:2]), g[...], b[...])


def _combine_ln(x, ys, ids, gates, g, b, tm):
    t = x.shape[0]
    nt = t // tm
    smem = functools.partial(pl.BlockSpec, (None, 1, 2 * tm), memory_space=pltpu.SMEM)
    rows = pl.BlockSpec((tm, D_MODEL), lambda i: (i, 0))
    vec = pl.BlockSpec((1, D_MODEL), lambda i: (0, 0))
    return pl.pallas_call(
        functools.partial(_combine_ln_body, tm=tm), grid=(nt,),
        in_specs=[smem(lambda i: (i, 0, 0)), smem(lambda i: (jnp.minimum(i + 1, nt - 1), 0, 0)),
                  rows, pl.BlockSpec(memory_space=pl.ANY),
                  pl.BlockSpec((tm, LANES), lambda i: (i, 0)), vec, vec],
        out_specs=rows, out_shape=jax.ShapeDtypeStruct((t, D_MODEL), F32),
        scratch_shapes=[pltpu.VMEM((2, 2 * tm // SUBLANES, SUBLANES, D_MODEL), F32),
                        pltpu.SemaphoreType.DMA((2,))],
        compiler_params=_params("arbitrary"),
        name="moe_combine_ln")(ids, ids, x, ys.reshape(-1, SUBLANES, D_MODEL), gates, g, b)


def _moe_layer(x, rw_pad, w1, w3, w2, g, b, tm=512):
    t = x.shape[0]
    eidx, gates, ranks, counts = _router(x, rw_pad)
    cnt = counts[0, :N_EXPERTS]
    padded = (cnt + tm - 1) // tm * tm
    pad_end = jnp.cumsum(padded)
    pad_start = pad_end - padded
    e2 = eidx[:, :2]
    dest = pad_start[e2] + ranks[:, :2]
    nblk = (2 * t) // tm + N_EXPERTS
    n_slots = nblk * tm
    tt = 512
    ids = dest.reshape(t // tt, tt, 2).transpose(0, 2, 1).reshape(t // tt, 1, 2 * tt)
    n_used = (pad_end[-1] // tm).astype(I32).reshape(1)
    blk = jnp.arange(nblk, dtype=I32) * tm
    block_e = jnp.minimum(jnp.sum(pad_end[None, :] <= blk[:, None], axis=1), N_EXPERTS - 1).astype(I32)
    block_e = jnp.where(jnp.arange(nblk) < n_used[0], block_e, block_e[jnp.maximum(n_used[0] - 1, 0)])
    xs = _moe_dispatch(x, ids, n_slots, tt)
    ys = _moe_experts(block_e, n_used, xs, w1, w3, w2, tm)
    return _combine_ln(x, ys, ids, gates, g, b, tt)


def _block_diag(w):
    g, c, _ = w.shape
    eye = jnp.eye(g, dtype=w.dtype)
    return (eye[:, None, :, None] * w[:, :, None, :]).reshape(g * c, g * c)


def kernel(x, mem, positions, w_in, conv_w, a_log, dt_bias, dn_norm_w, pool_w, pool_scale, w_out, ln_mix_g, ln_mix_b, xq_w, xk_w, xv_w, xo_w, ln_x_g, ln_x_b, ffn_w1, ffn_w3, ffn_w2, router_w, moe_w1, moe_w3, moe_w2, ln_ffn_g, ln_ffn_b):
    batch, seq, _ = x.shape
    t = batch * seq
    depth = w_in.shape[0]
    xf = x.reshape(t, D_MODEL)
    memf = mem.reshape(batch * mem.shape[1], D_MODEL)

    posc, posr = {}, {}
    for dil in DILATIONS:
        pt = positions.reshape(batch, seq // dil, dil).transpose(0, 2, 1)
        posc[dil] = pt[..., None]
        posr[dil] = pt[:, :, None, :]

    for l in range(depth):
        slabs, hr = _in_proj(xf, w_in, l, conv_w[l], seq)

        branches = [_att_branch(slabs, posc[d], posr[d], d, batch, seq) for d in DILATIONS]
        pvec = jnp.zeros((2, LANES), F32)
        pvec = pvec.at[0, DN_HEADS:2 * DN_HEADS].set(a_log[l]).at[1, DN_HEADS:2 * DN_HEADS].set(dt_bias[l])
        normw = jnp.tile(dn_norm_w[l], DN_HEADS)[None, :]
        dn = _delta_net(hr, pvec, normw, batch, seq)

        pool = _pool(hr, _block_diag(pool_w[l]).astype(BF16), pool_scale[l][None, :], batch, seq)

        wo = w_out[l].astype(BF16)
        xf = _proj_ln([o for o, _ in branches], [s for _, s in branches], dn, pool, xf, wo[:ATT_WIDTH],
                      wo[ATT_WIDTH:ATT_WIDTH + DN_WIDTH], wo[ATT_WIDTH + DN_WIDTH:],
                      ln_mix_g[l][None, :], ln_mix_b[l][None, :])

        wkv = jnp.concatenate([xk_w[l], xv_w[l]], axis=1).astype(BF16)
        kv = _matmul(memf, wkv, BF16, 512, 1024).reshape(batch, mem.shape[1], 2 * D_MODEL)
        xf = _cross_attention(xf, (xq_w[l] * (X_HEAD_DIM ** -0.5)).astype(BF16), kv[..., :D_MODEL],
                              kv[..., D_MODEL:], xo_w[l].astype(BF16), ln_x_g[l][None, :],
                              ln_x_b[l][None, :], seq)

        j = l // 2
        gl, bl = ln_ffn_g[l][None, :], ln_ffn_b[l][None, :]
        if l % 2 == 0:
            xf = _ffn_dense(xf, ffn_w1[j].astype(BF16), ffn_w3[j].astype(BF16), ffn_w2[j].astype(BF16), gl, bl)
        else:
            rw_pad = jnp.pad(router_w[j], ((0, 0), (0, LANES - N_EXPERTS)))
            xf = _moe_layer(xf, rw_pad, moe_w1[j].astype(BF16), moe_w3[j].astype(BF16),
                            moe_w2[j].astype(BF16), gl, bl)
    return xf.reshape(batch, seq, D_MODEL)
```

```python
import functools
import math

import jax
import jax.numpy as jnp
from jax import lax
from jax.experimental import pallas as pl
from jax.experimental.pallas import tpu as pltpu

F32, BF16, I32 = jnp.float32, jnp.bfloat16, jnp.int32

D_MODEL = 1024
HEAD_DIM = 64
ATT_HEADS = 6
ATT_WIDTH = ATT_HEADS * HEAD_DIM
DN_HEADS = 6
DN_WIDTH = DN_HEADS * HEAD_DIM
POOL_WIDTH = 256
POOL_WINDOWS = (2, 4, 8, 16)
POOL_GDIM = POOL_WIDTH // len(POOL_WINDOWS)
DILATIONS = (1, 4, 16)
CONV_K = 4
X_HEADS = 4
X_HEAD_DIM = D_MODEL // X_HEADS
N_EXPERTS = 8
DEPTH = 4
ALPHA = (2 * DEPTH) ** 0.25
LN_EPS = 1e-5
NORM_EPS = 1e-6

OFF_DN = 3 * ATT_WIDTH
OFF_BETA = OFF_DN + 3 * DN_WIDTH
OFF_DECAY = OFF_BETA + DN_HEADS
OFF_GATE = OFF_DECAY + DN_HEADS
OFF_POOL = OFF_GATE + DN_WIDTH

LANES = 128
MXU_WIDTH = 256
R_DN = 0
R_GATE = R_DN + 3 * DN_WIDTH
R_POOL = R_GATE + DN_WIDTH
R_SMALL = R_POOL + POOL_WIDTH
R_COLS = R_SMALL + LANES

QB = 128
NEG = -1e30
VMEM_LIMIT = 48 * 1024 * 1024


def _params(*sem):
    return pltpu.CompilerParams(dimension_semantics=sem, vmem_limit_bytes=VMEM_LIMIT)


def _sigmoid(x):
    return 1.0 / (1.0 + jnp.exp(-x))


def _layer_norm_rows(z, g, b):
    mu = jnp.mean(z, axis=-1, keepdims=True)
    zc = z - mu
    var = jnp.mean(zc * zc, axis=-1, keepdims=True)
    return zc * lax.rsqrt(var + LN_EPS) * g + b


def _rows(start, size, stride):
    return pl.ds(start, size) if stride == 1 else pl.ds(start, size, stride=stride)


def _mm_body(x_ref, w_ref, o_ref):
    o_ref[...] = jnp.dot(x_ref[...].astype(BF16), w_ref[...],
                         preferred_element_type=F32).astype(o_ref.dtype)


def _matmul(x, w, out_dtype, tm, tn):
    m, k = x.shape
    n = w.shape[1]
    return pl.pallas_call(
        _mm_body, grid=(m // tm, n // tn),
        in_specs=[pl.BlockSpec((tm, k), lambda i, j: (i, 0)),
                  pl.BlockSpec((k, tn), lambda i, j: (0, j))],
        out_specs=pl.BlockSpec((tm, tn), lambda i, j: (i, j)),
        out_shape=jax.ShapeDtypeStruct((m, n), out_dtype),
        compiler_params=_params("parallel", "arbitrary"), name="matmul")(x, w)


CONV_HALO = 8


def _in_proj_body(x_ref, w_ref, cw_ref, slab_ref, o_ref, wb, hist, cbuf, *, tm, tiles_per_seq):
    i = pl.program_id(0)
    ncv = 3 * DN_WIDTH

    @pl.when(i == 0)
    def _():
        wb[:, 0:OFF_BETA] = w_ref[:, 0:OFF_BETA].astype(BF16)
        c0 = OFF_DN + R_GATE
        wb[:, c0:c0 + DN_WIDTH] = w_ref[:, OFF_GATE:OFF_POOL].astype(BF16)
        c0 = OFF_DN + R_POOL
        wb[:, c0:c0 + POOL_WIDTH] = w_ref[:, OFF_POOL:].astype(BF16)
        c0 = OFF_DN + R_SMALL
        wb[:, c0:c0 + LANES] = jnp.zeros((wb.shape[0], LANES), BF16)
        wb[:, c0:c0 + 2 * DN_HEADS] = w_ref[:, OFF_BETA:OFF_GATE].astype(BF16)

    @pl.when(lax.rem(i, tiles_per_seq) == 0)
    def _():
        hist[...] = jnp.zeros_like(hist)

    xb = x_ref[...].astype(BF16)
    for j in range(slab_ref.shape[0]):
        if j % 2 == 0:
            res = jnp.dot(xb, wb[:, j * LANES:min(j + 2, slab_ref.shape[0]) * LANES],
                          preferred_element_type=F32)
        slab_ref[j] = res[:, (j % 2) * LANES:(j % 2 + 1) * LANES]

    n = o_ref.shape[1]
    starts = list(range(0, ncv, MXU_WIDTH)) + list(range(ncv, n, MXU_WIDTH))
    for c0, c1 in zip(starts, starts[1:] + [n]):
        cs = slice(c0, c1)
        res = jnp.dot(xb, wb[:, OFF_DN + c0:OFF_DN + c1], preferred_element_type=F32)
        if c0 >= ncv:
            o_ref[:, cs] = res
            continue
        cbuf[0:CONV_HALO, cs] = hist[:, cs]
        cbuf[CONV_HALO:, cs] = res
        hist[:, cs] = cbuf[tm:, cs]
        acc = None
        for j in range(CONV_K):
            start = CONV_HALO - (CONV_K - 1) + j
            t = cbuf[start:start + tm, cs] * cw_ref[j:j + 1, cs]
            acc = t if acc is None else acc + t
        o_ref[:, cs] = acc * _sigmoid(acc)


def _in_proj(x, w_in, layer, conv_w, seq, tm=512):
    m, k = x.shape
    nslab = OFF_DN // LANES
    return pl.pallas_call(
        functools.partial(_in_proj_body, tm=tm, tiles_per_seq=seq // tm), grid=(m // tm,),
        in_specs=[pl.BlockSpec((tm, k), lambda i: (i, 0)),
                  pl.BlockSpec((None,) + w_in.shape[1:], lambda i: (layer, 0, 0),
                               pipeline_mode=pl.Buffered(1)),
                  pl.BlockSpec(conv_w.shape, lambda i: (0, 0))],
        out_specs=[pl.BlockSpec((nslab, tm, LANES), lambda i: (0, i, 0)),
                   pl.BlockSpec((tm, R_COLS), lambda i: (i, 0))],
        out_shape=[jax.ShapeDtypeStruct((nslab, m, LANES), F32),
                   jax.ShapeDtypeStruct((m, R_COLS), F32)],
        scratch_shapes=[pltpu.VMEM((k, OFF_DN + R_COLS), BF16),
                        pltpu.VMEM((CONV_HALO, 3 * DN_WIDTH), F32),
                        pltpu.VMEM((tm + CONV_HALO, 3 * DN_WIDTH), F32)],
        compiler_params=_params("arbitrary"), name="in_proj_conv")(x, w_in, conv_w)


def _att_body(q_ref, kc_ref, vc_ref, pq_ref, pkc_ref, o_ref, lse_ref,
              kbuf, vbuf, pkbuf, kprev, vprev, pkprev, *, dil, rows, slopes):
    n = pl.program_id(1)
    nsub = rows // QB
    npair = ATT_HEADS // 2

    @pl.when(n == 0)
    def _():
        kprev[...] = jnp.zeros_like(kprev)
        vprev[...] = jnp.zeros_like(vprev)
        pkprev[...] = jnp.zeros_like(pkprev)

    a = lax.broadcasted_iota(I32, (QB, 2 * QB), 0)
    c = lax.broadcasted_iota(I32, (QB, 2 * QB), 1)
    tri = jnp.where(c < QB, c - a, a - (c - QB)) >= 0
    lane = lax.broadcasted_iota(I32, (QB, LANES), 1)
    low = lane < HEAD_DIM

    def residue(r, carry):
        for p in range(npair):
            kbuf[p, 0:QB, :] = kprev[r, p]
            kbuf[p, QB:, :] = kc_ref[p, _rows(r, rows, dil), :].astype(BF16)
            vbuf[p, 0:QB, :] = vprev[r, p]
            vbuf[p, QB:, :] = vc_ref[p, _rows(r, rows, dil), :].astype(BF16)
            kprev[r, p] = kbuf[p, rows:, :]
            vprev[r, p] = vbuf[p, rows:, :]
        pkbuf[0] = pkprev[r]
        for j in range(nsub):
            pkbuf[j + 1] = pkc_ref[r, :, j * QB:(j + 1) * QB]
        pkprev[r] = pkbuf[nsub]

        def sub(sb, carry2):
            r0 = pl.multiple_of(sb * QB, QB)
            t0 = r + r0 * dil
            pq = pq_ref[r, pl.ds(r0, QB), :]
            pk = jnp.concatenate([pkbuf[sb], pkbuf[sb + 1]], axis=1)
            dist = (pq - pk).astype(F32)
            mask = tri & (c >= jnp.where((sb > 0) | (n > 0), 0, QB))
            heads = range(ATT_HEADS)
            ks = [kbuf[p, pl.ds(r0, 2 * QB), :] for p in range(npair)]
            vs = [vbuf[p, pl.ds(r0, 2 * QB), :] for p in range(npair)]
            qs = [q_ref[p, _rows(t0, QB, dil), :] * (HEAD_DIM ** -0.5) for p in range(npair)]
            qms = [jnp.where(low if h % 2 == 0 else ~low, qs[h // 2], 0.0).astype(BF16) for h in heads]
            ss = [lax.dot_general(qms[h], ks[h // 2], (((1,), (1,)), ((), ())), preferred_element_type=F32)
                  for h in heads]
            ss = [jnp.where(mask, ss[h] - slopes[h] * dist, NEG) for h in heads]
            ms = [jnp.max(ss[h], axis=-1, keepdims=True) for h in heads]
            es = [jnp.exp(ss[h] - ms[h]) for h in heads]
            ls = [jnp.sum(es[h], axis=-1, keepdims=True) for h in heads]
            outs = [jnp.dot(es[h].astype(BF16), vs[h // 2], preferred_element_type=F32) / ls[h] for h in heads]
            lse = jnp.zeros((QB, LANES), F32)
            for h in heads:
                lse = jnp.where(lane == h, ms[h] + jnp.log(ls[h]), lse)
            for p in range(npair):
                o_ref[p, _rows(t0, QB, dil), :] = jnp.where(low, outs[2 * p], outs[2 * p + 1])
            lse_ref[_rows(t0, QB, dil), :] = lse
            return carry2

        lax.fori_loop(0, nsub, sub, 0, unroll=min(nsub, 2))
        return carry

    lax.fori_loop(0, dil, residue, 0)


def _att_branch(slabs, posc, posr, dil, batch, seq):
    sub_len = seq // dil
    rows = min(512 if dil == 1 else 2048 // dil, sub_len)
    tok = rows * dil
    nsteps = seq // tok
    npair = ATT_HEADS // 2
    slopes = tuple(2.0 ** (-8.0 * (i + 1) / ATT_HEADS) for i in range(ATT_HEADS))

    def cur(which):
        return pl.BlockSpec((npair, tok, LANES), lambda b, n: (which, b * nsteps + n, 0))

    t = batch * seq
    return pl.pallas_call(
        functools.partial(_att_body, dil=dil, rows=rows, slopes=slopes),
        grid=(batch, nsteps),
        in_specs=[cur(0), cur(1), cur(2),
                  pl.BlockSpec((None, dil, rows, 1), lambda b, n: (b, 0, n, 0)),
                  pl.BlockSpec((None, dil, 1, rows), lambda b, n: (b, 0, 0, n))],
        out_specs=[pl.BlockSpec((npair, tok, LANES), lambda b, n: (0, b * nsteps + n, 0)),
                   pl.BlockSpec((tok, LANES), lambda b, n: (b * nsteps + n, 0))],
        out_shape=[jax.ShapeDtypeStruct((npair, t, LANES), F32),
                   jax.ShapeDtypeStruct((t, LANES), F32)],
        scratch_shapes=[pltpu.VMEM((npair, rows + QB, LANES), BF16),
                        pltpu.VMEM((npair, rows + QB, LANES), BF16),
                        pltpu.VMEM((rows // QB + 1, 1, QB), I32),
                        pltpu.VMEM((dil, npair, QB, LANES), BF16),
                        pltpu.VMEM((dil, npair, QB, LANES), BF16),
                        pltpu.VMEM((dil, 1, QB), I32)],
        compiler_params=_params("arbitrary", "arbitrary"),
        name=f"dilated_att_d{dil}")(slabs, slabs, slabs, posc, posr)


def _merge_branches(os_, ls_):
    ls = [l[...] for l in ls_]
    m = functools.reduce(jnp.maximum, ls)
    ws = [jnp.exp(l - m) for l in ls]
    inv = 1.0 / functools.reduce(lambda a, b: a + b, ws)
    ws = [w * inv for w in ws]
    lane = lax.broadcasted_iota(I32, ls[0].shape, 1)
    low = lane < HEAD_DIM
    pairs = []
    for p in range(ATT_HEADS // 2):
        acc = None
        for w, o in zip(ws, os_):
            wp = jnp.where(low, w[:, 2 * p:2 * p + 1], w[:, 2 * p + 1:2 * p + 2])
            t = wp * o[p]
            acc = t if acc is None else acc + t
        pairs.append(acc)
    return jnp.concatenate(pairs, axis=1)


def _split2(a):
    hi = a.astype(BF16)
    lo = (a - hi.astype(F32)).astype(BF16)
    return hi, lo


def _dot(a, b):
    return jnp.dot(a, b, preferred_element_type=F32)


def _dot_bf16(a, b):
    return _dot(a.astype(BF16), b.astype(BF16))


def _dot_hp(a, b):
    ah, al = _split2(a)
    bh, bl = _split2(b)
    return _dot(jnp.concatenate([ah, al, ah], axis=1), jnp.concatenate([bh, bh, bl], axis=0))


def _dot_t(a, b):
    return lax.dot_general(a, b, (((1,), (1,)), ((), ())), preferred_element_type=F32)


def _dot_t_hp(a, b):
    ah, al = _split2(a)
    bh, bl = _split2(b)
    return _dot_t(jnp.concatenate([ah, al, ah], axis=1), jnp.concatenate([bh, bh, bl], axis=1))


def _dn_body(hq, hk, hv, small, gate, pvec, normw, out, s_ref, *, nch):
    n = pl.program_id(1)
    c_ = QB
    npair = DN_HEADS // 2
    qkv = (hq, hk, hv)

    @pl.when(n == 0)
    def _():
        s_ref[...] = jnp.zeros_like(s_ref)

    lane = lax.broadcasted_iota(I32, (c_, LANES), 1)
    low = lane < HEAD_DIM
    row = lax.broadcasted_iota(I32, (c_, c_), 0)
    col = lax.broadcasted_iota(I32, (c_, c_), 1)
    lower = row >= col
    strict = row > col
    eye = jnp.where(row == col, 1.0, 0.0)
    blockdiag = (row < HEAD_DIM) == (col < HEAD_DIM)

    def headsum(x):
        sa = jnp.sum(jnp.where(low, x, 0.0), axis=-1, keepdims=True)
        sb = jnp.sum(jnp.where(low, 0.0, x), axis=-1, keepdims=True)
        return jnp.where(low, sa, sb)

    def col_of(arr, j):
        return arr[:, j:j + 1]

    ltri = jnp.where(lower, 1.0, 0.0).astype(BF16)
    ltri3 = jnp.concatenate([ltri, ltri, ltri], axis=1)
    pair, xs, pws, qks = [], [], [], []
    for ch in range(nch):
        rs = slice(ch * c_, (ch + 1) * c_)
        sm = small[rs, :]
        beta_all = _sigmoid(sm)
        z = sm + pvec[1:2, :]
        softplus = jnp.maximum(z, 0.0) + jnp.log(1.0 + jnp.exp(-jnp.abs(z)))
        g_all = -jnp.exp(pvec[0:1, :]) * softplus
        g1 = g_all.astype(BF16)
        rem = g_all - g1.astype(F32)
        g2 = rem.astype(BF16)
        g3 = (rem - g2.astype(F32)).astype(BF16)
        gc_all = _dot(ltri3, jnp.concatenate([g1, g2, g3], axis=0))
        gc_t = gc_all.T

        for p in range(npair):
            cs = slice(p * LANES, (p + 1) * LANES)
            q, k, v = qkv[0][rs, cs], qkv[1][rs, cs], qkv[2][rs, cs]
            q = q * lax.rsqrt(headsum(q * q) + NORM_EPS) * (HEAD_DIM ** -0.5)
            k = k * lax.rsqrt(headsum(k * k) + NORM_EPS)
            beta = jnp.where(low, col_of(beta_all, 2 * p), col_of(beta_all, 2 * p + 1))
            gc = jnp.where(low, col_of(gc_all, DN_HEADS + 2 * p), col_of(gc_all, DN_HEADS + 2 * p + 1))
            eg = jnp.exp(gc)
            rhs = jnp.concatenate([v * beta, k * beta * eg], axis=1)
            pair.append(dict(q=q, k=k, kb=k.astype(BF16), gc=gc, eg=eg, rhs=rhs))

        for h in range(DN_HEADS):
            pr = pair[ch * npair + h // 2]
            sel = low if h % 2 == 0 else ~low
            gcol = col_of(gc_all, DN_HEADS + h)
            grow = gc_t[DN_HEADS + h:DN_HEADS + h + 1, :]
            dec = jnp.where(lower, jnp.exp(jnp.minimum(gcol - grow, 0.0)), 0.0)
            kk = _dot_t(jnp.where(sel, pr["k"], 0.0).astype(BF16), pr["kb"])
            qk = _dot_t_hp(jnp.where(sel, pr["q"], 0.0), pr["k"])
            nm = jnp.where(strict, col_of(beta_all, h) * kk * dec, 0.0)
            xs.append(eye - nm)
            pws.append(nm)
            qks.append((qk * dec).astype(BF16))

    for level in range(int(math.log2(c_)) - 1):
        mm = _dot_hp if level < 3 else _dot_bf16
        for i in range(len(xs)):
            pws[i] = mm(pws[i], pws[i])
        for i in range(len(xs)):
            xs[i] = xs[i] + mm(xs[i], pws[i])
    uws = [_dot_hp(xs[i], pair[i // 2]["rhs"]) for i in range(len(xs))]

    for ch in range(nch):
        rs = slice(ch * c_, (ch + 1) * c_)
        for p in range(npair):
            pr = pair[ch * npair + p]
            i0 = ch * DN_HEADS + 2 * p
            cs = slice(p * LANES, (p + 1) * LANES)
            u = jnp.where(low, uws[i0][:, :LANES], uws[i0 + 1][:, :LANES])
            w = jnp.where(low, uws[i0][:, LANES:], uws[i0 + 1][:, LANES:])
            state = s_ref[p]
            sb16 = state.astype(BF16)
            vnew = u - _dot(w.astype(BF16), sb16)
            vn16 = vnew.astype(BF16)
            o = _dot((pr["q"] * pr["eg"]).astype(BF16), sb16)
            o = o + jnp.where(low, _dot(qks[i0], vn16), _dot(qks[i0 + 1], vn16))
            glast = pr["gc"][c_ - 1:c_, :]
            kd = pr["k"] * jnp.exp(glast - pr["gc"])
            upd = _dot(kd.T.astype(BF16), vn16)
            s_ref[p] = state * jnp.exp(glast) + jnp.where(blockdiag, upd, 0.0)
            ms = headsum(o * o) * (1.0 / HEAD_DIM)
            gt = gate[rs, cs]
            out[rs, cs] = (o * lax.rsqrt(ms + NORM_EPS) * normw[:, cs] * (gt * _sigmoid(gt))).astype(out.dtype)


def _delta_net(hr, pvec, normw, batch, seq, nch=2):
    h3 = hr.reshape(batch, seq, R_COLS)
    rows = nch * QB
    qcol = R_DN // DN_WIDTH

    def cur(off):
        return pl.BlockSpec((None, rows, DN_WIDTH), lambda b, n: (b, n, qcol + off))

    out = pl.pallas_call(
        functools.partial(_dn_body, nch=nch), grid=(batch, seq // rows),
        in_specs=[cur(0), cur(1), cur(2),
                  pl.BlockSpec((None, rows, LANES), lambda b, n: (b, n, R_SMALL // LANES)),
                  pl.BlockSpec((None, rows, DN_WIDTH), lambda b, n: (b, n, R_GATE // DN_WIDTH)),
                  pl.BlockSpec((2, LANES), lambda b, n: (0, 0)),
                  pl.BlockSpec((1, DN_WIDTH), lambda b, n: (0, 0))],
        out_specs=pl.BlockSpec((None, rows, DN_WIDTH), lambda b, n: (b, n, 0)),
        out_shape=jax.ShapeDtypeStruct((batch, seq, DN_WIDTH), BF16),
        scratch_shapes=[pltpu.VMEM((DN_HEADS // 2, LANES, LANES), F32)],
        compiler_params=_params("parallel", "arbitrary"),
        name="gated_delta_rule")(h3, h3, h3, h3, h3, pvec, normw)
    return out.reshape(batch * seq, DN_WIDTH)


def _pool_body(cur, prev, wbd, scale, out, xbuf, *, rows):
    n = pl.program_id(1)
    halo = 16
    xbuf[0:halo, :] = prev[...] * jnp.where(n > 0, 1.0, 0.0)
    xbuf[halo:, :] = cur[...]
    u = cur[...]
    lane = lax.broadcasted_iota(I32, (rows, POOL_WIDTH), 1)
    tpos = n * rows + lax.broadcasted_iota(I32, (rows, POOL_WIDTH), 0)
    run = u
    pooled = jnp.zeros_like(u)
    for j in range(1, max(POOL_WINDOWS)):
        run = run + xbuf[halo - j:halo - j + rows, :]
        if j + 1 in POOL_WINDOWS:
            gi = POOL_WINDOWS.index(j + 1)
            cnt = jnp.minimum(tpos + 1, j + 1).astype(F32)
            sel = (lane >= gi * POOL_GDIM) & (lane < (gi + 1) * POOL_GDIM)
            pooled = jnp.where(sel, run / cnt - u, pooled)
    mixed = jnp.dot(pooled.astype(BF16), wbd[...], preferred_element_type=F32)
    out[...] = (mixed * scale[...]).astype(out.dtype)


def _pool(hr, wbd, scale, batch, seq, rows=512):
    h3 = hr.reshape(batch, seq, R_COLS)
    out = pl.pallas_call(
        functools.partial(_pool_body, rows=rows), grid=(batch, seq // rows),
        in_specs=[pl.BlockSpec((None, rows, POOL_WIDTH), lambda b, n: (b, n, R_POOL // POOL_WIDTH)),
                  pl.BlockSpec((None, 16, POOL_WIDTH),
                               lambda b, n: (b, jnp.maximum(n * (rows // 16) - 1, 0), R_POOL // POOL_WIDTH)),
                  pl.BlockSpec((POOL_WIDTH, POOL_WIDTH), lambda b, n: (0, 0)),
                  pl.BlockSpec((1, POOL_WIDTH), lambda b, n: (0, 0))],
        out_specs=pl.BlockSpec((None, rows, POOL_WIDTH), lambda b, n: (b, n, 0)),
        out_shape=jax.ShapeDtypeStruct((batch, seq, POOL_WIDTH), BF16),
        scratch_shapes=[pltpu.VMEM((rows + 16, POOL_WIDTH), F32)],
        compiler_params=_params("parallel", "arbitrary"), name="multiscale_pool")(h3, h3, wbd, scale)
    return out.reshape(batch * seq, POOL_WIDTH)


def _proj_ln_body(o1, o2, o3, l1, l2, l3, d, p, x, wa, wd, wp, g, b, o):
    att = _merge_branches((o1, o2, o3), (l1, l2, l3)).astype(BF16)
    y = jnp.dot(att, wa[...], preferred_element_type=F32)
    y = y + jnp.dot(d[...], wd[...], preferred_element_type=F32)
    y = y + jnp.dot(p[...], wp[...], preferred_element_type=F32)
    o[...] = _layer_norm_rows(ALPHA * x[...] + y, g[...], b[...])


def _proj_ln(branch_o, branch_lse, dn, pool, x, wa, wd, wp, g, b, tm=512):
    t = x.shape[0]
    npair = ATT_HEADS // 2

    def rows(wd_):
        return pl.BlockSpec((tm, wd_), lambda i: (i, 0))

    def full(a_):
        return pl.BlockSpec(a_.shape, lambda i: (0, 0))

    slab = pl.BlockSpec((npair, tm, LANES), lambda i: (0, i, 0))
    return pl.pallas_call(
        _proj_ln_body, grid=(t // tm,),
        in_specs=[slab] * 3 + [rows(LANES)] * 3 + [rows(DN_WIDTH), rows(POOL_WIDTH), rows(D_MODEL),
                                                    full(wa), full(wd), full(wp), full(g), full(b)],
        out_specs=rows(D_MODEL), out_shape=jax.ShapeDtypeStruct((t, D_MODEL), F32),
        compiler_params=_params("parallel"),
        name="out_proj_ln")(*branch_o, *branch_lse, dn, pool, x, wa, wd, wp, g, b)


def _xattn_body(x, wq, k, v, wo, g, b, o):
    xv = x[...]
    q = jnp.dot(xv.astype(BF16), wq[...], preferred_element_type=F32).astype(BF16)
    cols = [slice(h * X_HEAD_DIM, (h + 1) * X_HEAD_DIM) for h in range(X_HEADS)]
    ss = [_dot_t(q[:, cs], k[:, cs]) for cs in cols]
    ms = [jnp.max(s, axis=-1, keepdims=True) for s in ss]
    es = [jnp.exp(s - m) for s, m in zip(ss, ms)]
    ls = [jnp.sum(e, axis=-1, keepdims=True) for e in es]
    heads = [(_dot(e.astype(BF16), v[:, cs]) / l).astype(BF16) for e, l, cs in zip(es, ls, cols)]
    oc = jnp.concatenate(heads, axis=1)
    y = jnp.dot(oc, wo[...], preferred_element_type=F32)
    o[...] = _layer_norm_rows(ALPHA * xv + y, g[...], b[...])


def _cross_attention(x, wq, kv, wo, g, b, seq, tm=1024):
    t = x.shape[0]
    mem_len = kv.shape[1]
    per_batch = seq // tm

    def full(a_):
        return pl.BlockSpec(a_.shape, lambda i: (0, 0))

    def half(which):
        return pl.BlockSpec((None, mem_len, D_MODEL), lambda i: (i // per_batch, 0, which))

    rows = pl.BlockSpec((tm, D_MODEL), lambda i: (i, 0))
    return pl.pallas_call(
        _xattn_body, grid=(t // tm,),
        in_specs=[rows, full(wq), half(0), half(1), full(wo), full(g), full(b)],
        out_specs=rows, out_shape=jax.ShapeDtypeStruct((t, D_MODEL), F32),
        compiler_params=_params("parallel"), name="cross_attention_ln")(x, wq, kv, kv, wo, g, b)


def _ffn_body(x, w1, w3, w2, g, b, o, *, chunks):
    xv = x[...]
    xb = xv.astype(BF16)
    y = None
    for c0, c1 in chunks:
        h1 = jnp.dot(xb, w1[:, c0:c1], preferred_element_type=F32)
        h3 = jnp.dot(xb, w3[:, c0:c1], preferred_element_type=F32)
        hh = (h1 * _sigmoid(h1) * h3).astype(BF16)
        t = jnp.dot(hh, w2[c0:c1, :], preferred_element_type=F32)
        y = t if y is None else y + t
    o[...] = _layer_norm_rows(ALPHA * xv + y, g[...], b[...])


def _ffn_dense(x, w1, w3, w2, g, b, tm=512, chunk=6 * MXU_WIDTH):
    t = x.shape[0]
    dff = w1.shape[1]
    chunks = tuple((c0, min(c0 + chunk, dff)) for c0 in range(0, dff, chunk))
    rows = pl.BlockSpec((tm, D_MODEL), lambda i: (i, 0))
    vec = pl.BlockSpec((1, D_MODEL), lambda i: (0, 0))

    def resident(a_):
        return pl.BlockSpec(a_.shape, lambda i: (0, 0), pipeline_mode=pl.Buffered(1))

    return pl.pallas_call(
        functools.partial(_ffn_body, chunks=chunks), grid=(t // tm,),
        in_specs=[rows, resident(w1), resident(w3), resident(w2), vec, vec],
        out_specs=rows, out_shape=jax.ShapeDtypeStruct((t, D_MODEL), F32),
        compiler_params=_params("parallel"), name="ffn_swiglu_ln")(x, w1, w3, w2, g, b)


def _router_body(x, rw, eidx, gates, ranks, counts, carry, *, tm):
    i = pl.program_id(0)

    @pl.when(i == 0)
    def _():
        carry[...] = jnp.zeros_like(carry)

    logits = _dot_hp(x[...], rw[...])
    lane = lax.broadcasted_iota(I32, (tm, LANES), 1)
    lanef = lane.astype(F32)
    lg = jnp.where(lane < N_EXPERTS, logits, NEG)
    m1 = jnp.max(lg, axis=-1, keepdims=True)
    i1 = jnp.min(jnp.where(lg == m1, lanef, float(LANES)), axis=-1, keepdims=True)
    lg2 = jnp.where(lanef == i1, NEG, lg)
    m2 = jnp.max(lg2, axis=-1, keepdims=True)
    i2 = jnp.min(jnp.where(lg2 == m2, lanef, float(LANES)), axis=-1, keepdims=True)
    e21 = jnp.exp(m2 - m1)
    g1 = 1.0 / (1.0 + e21)
    g2 = e21 * g1
    hot1 = lanef == i1
    hot2 = lanef == i2
    onehot = jnp.where(hot1 | hot2, 1.0, 0.0)
    r = lax.broadcasted_iota(I32, (tm, tm), 0)
    c = lax.broadcasted_iota(I32, (tm, tm), 1)
    before = jnp.dot(jnp.where(r > c, 1.0, 0.0).astype(BF16), onehot.astype(BF16),
                     preferred_element_type=F32) + carry[...]
    rank1 = jnp.sum(jnp.where(hot1, before, 0.0), axis=-1, keepdims=True)
    rank2 = jnp.sum(jnp.where(hot2, before, 0.0), axis=-1, keepdims=True)
    carry[...] = carry[...] + jnp.sum(onehot, axis=0, keepdims=True)
    eidx[...] = jnp.where(lane == 0, i1, jnp.where(lane == 1, i2, 0.0)).astype(I32)
    gates[...] = jnp.where(lane == 0, g1, jnp.where(lane == 1, g2, 0.0))
    ranks[...] = jnp.where(lane == 0, rank1, jnp.where(lane == 1, rank2, 0.0)).astype(I32)
    counts[...] = jnp.broadcast_to(carry[...], counts.shape).astype(I32)


def _router(x, rw, tm=512):
    t = x.shape[0]
    rows = pl.BlockSpec((tm, LANES), lambda i: (i, 0))
    return pl.pallas_call(
        functools.partial(_router_body, tm=tm), grid=(t // tm,),
        in_specs=[pl.BlockSpec((tm, D_MODEL), lambda i: (i, 0)),
                  pl.BlockSpec((D_MODEL, LANES), lambda i: (0, 0))],
        out_specs=[rows, rows, rows, pl.BlockSpec((8, LANES), lambda i: (0, 0))],
        out_shape=[jax.ShapeDtypeStruct((t, LANES), I32), jax.ShapeDtypeStruct((t, LANES), F32),
                   jax.ShapeDtypeStruct((t, LANES), I32), jax.ShapeDtypeStruct((8, LANES), I32)],
        scratch_shapes=[pltpu.VMEM((1, LANES), F32)],
        compiler_params=_params("arbitrary"), name="router_top2")(x, rw)


def _moe_body(be, nused, xs, w1, w3, w2, ys, xb, acc):
    i = pl.program_id(0)
    f = pl.program_id(1)
    last = f == pl.num_programs(1) - 1
    used = i < nused[0]

    @pl.when(used & (f == 0))
    def _():
        xb[...] = xs[...].astype(BF16)
        acc[...] = jnp.zeros_like(acc)

    @pl.when(used)
    def _():
        xv = xb[...]
        h1 = jnp.dot(xv, w1[...], preferred_element_type=F32)
        h3 = jnp.dot(xv, w3[...], preferred_element_type=F32)
        hh = (h1 * _sigmoid(h1) * h3).astype(BF16)
        acc[...] += jnp.dot(hh, w2[...], preferred_element_type=F32)

    @pl.when(used & last)
    def _():
        ys[...] = acc[...]

    @pl.when(jnp.logical_not(used) & last)
    def _():
        ys[...] = jnp.zeros_like(ys)


def _moe_experts(block_e, n_used, xs, w1, w3, w2, tm, tf=1792):
    n_slots = xs.shape[0]
    dex = w1.shape[2]
    nblk = n_slots // tm

    def live(i, nu):
        return jnp.minimum(i, nu[0] - 1)

    grid_spec = pltpu.PrefetchScalarGridSpec(
        num_scalar_prefetch=2, grid=(nblk, dex // tf),
        in_specs=[pl.BlockSpec((tm, D_MODEL), lambda i, f, be, nu: (live(i, nu), 0)),
                  pl.BlockSpec((None, D_MODEL, tf), lambda i, f, be, nu: (be[i], 0, jnp.where(i < nu[0], f, dex // tf - 1))),
                  pl.BlockSpec((None, D_MODEL, tf), lambda i, f, be, nu: (be[i], 0, jnp.where(i < nu[0], f, dex // tf - 1))),
                  pl.BlockSpec((None, tf, D_MODEL), lambda i, f, be, nu: (be[i], jnp.where(i < nu[0], f, dex // tf - 1), 0))],
        out_specs=pl.BlockSpec((tm, D_MODEL), lambda i, f, be, nu: (i, 0)),
        scratch_shapes=[pltpu.VMEM((tm, D_MODEL), BF16), pltpu.VMEM((tm, D_MODEL), F32)])
    return pl.pallas_call(
        _moe_body, grid_spec=grid_spec, out_shape=jax.ShapeDtypeStruct((n_slots, D_MODEL), F32),
        compiler_params=_params("arbitrary", "arbitrary"), name="moe_experts")(block_e, n_used, xs, w1, w3, w2)


SUBLANES = 8


def _tile_row(ref3, idx):
    return ref3.at[lax.shift_right_logical(idx, 3), pl.ds(idx & (SUBLANES - 1), 1), :]


def _start_row_gather(src, ids, buf, sem, slot, nrows):
    def issue(j, carry):
        for u in range(SUBLANES):
            idx = ids[0, j * SUBLANES + u]
            pltpu.make_async_copy(_tile_row(src, idx), buf.at[slot, j, pl.ds(u, 1), :], sem.at[slot]).start()
        return carry

    lax.fori_loop(0, nrows // SUBLANES, issue, 0)


def _wait_row_gather(src, buf, sem, slot, nrows):
    pltpu.make_async_copy(src.at[pl.ds(0, nrows // SUBLANES)], buf.at[slot], sem.at[slot]).wait()


def _pipelined_row_gather(src, ids_cur, ids_next, buf, sem, nrows):
    i = pl.program_id(0)
    slot = lax.rem(i, 2)

    @pl.when(i == 0)
    def _():
        _start_row_gather(src, ids_cur, buf, sem, 0, nrows)

    @pl.when(i + 1 < pl.num_programs(0))
    def _():
        _start_row_gather(src, ids_next, buf, sem, 1 - slot, nrows)

    _wait_row_gather(src, buf, sem, slot, nrows)
    return slot


def _dispatch_body(tails, ids, x, xs, zbuf, sem, *, tm):
    fill = zbuf.shape[0]

    @pl.when(pl.program_id(0) == 0)
    def _():
        zbuf[...] = jnp.zeros_like(zbuf)

        def tail_fill(e):
            t0 = jnp.minimum(lax.shift_right_logical(tails[e], 3), xs.shape[0] - fill)
            return pltpu.make_async_copy(zbuf, xs.at[pl.ds(t0, fill)], sem)

        for e in range(tails.shape[0]):
            tail_fill(e).start()
            tail_fill(e).wait()

    def issue(j, carry):
        for u in range(SUBLANES):
            for k in range(2):
                dst = _tile_row(xs, ids[0, k * tm + j * SUBLANES + u])
                pltpu.make_async_copy(x.at[j, pl.ds(u, 1), :], dst, sem).start()
        return carry

    lax.fori_loop(0, tm // SUBLANES, issue, 0)
    for _ in range(2):
        pltpu.make_async_copy(x, xs.at[pl.ds(0, tm // SUBLANES)], sem).wait()


def _moe_dispatch(x, ids, tails, n_slots, tm, max_tail):
    t = x.shape[0]
    grid_spec = pltpu.PrefetchScalarGridSpec(
        num_scalar_prefetch=1, grid=(t // tm,),
        in_specs=[pl.BlockSpec((None, 1, 2 * tm), lambda i, tl: (i, 0, 0), memory_space=pltpu.SMEM),
                  pl.BlockSpec((tm // SUBLANES, SUBLANES, D_MODEL), lambda i, tl: (i, 0, 0))],
        out_specs=pl.BlockSpec(memory_space=pl.ANY),
        scratch_shapes=[pltpu.VMEM((max_tail // SUBLANES + 1, SUBLANES, D_MODEL), F32),
                        pltpu.SemaphoreType.DMA(())])
    xs = pl.pallas_call(
        functools.partial(_dispatch_body, tm=tm), grid_spec=grid_spec,
        out_shape=jax.ShapeDtypeStruct((n_slots // SUBLANES, SUBLANES, D_MODEL), F32),
        compiler_params=_params("arbitrary"),
        name="moe_dispatch_scatter")(tails, ids, x.reshape(t // SUBLANES, SUBLANES, D_MODEL))
    return xs.reshape(n_slots, D_MODEL)


def _combine_ln_body(ids_cur, ids_next, x, ys, gates, g, b, o, buf, sem, *, tm):
    slot = _pipelined_row_gather(ys, ids_cur, ids_next, buf, sem, 2 * tm)
    gt = gates[...]
    nt = tm // SUBLANES
    ya = buf[slot, 0:nt].reshape(tm, D_MODEL)
    yb = buf[slot, nt:].reshape(tm, D_MODEL)
    o[...] = _layer_norm_rows(ALPHA * x[...] + (ya * gt[:, 0:1] + yb * gt[:, 1:2]), g[...], b[...])


def _combine_ln(x, ys, ids, gates, g, b, tm):
    t = x.shape[0]
    nt = t // tm
    smem = functools.partial(pl.BlockSpec, (None, 1, 2 * tm), memory_space=pltpu.SMEM)
    rows = pl.BlockSpec((tm, D_MODEL), lambda i: (i, 0))
    vec = pl.BlockSpec((1, D_MODEL), lambda i: (0, 0))
    return pl.pallas_call(
        functools.partial(_combine_ln_body, tm=tm), grid=(nt,),
        in_specs=[smem(lambda i: (i, 0, 0)), smem(lambda i: (jnp.minimum(i + 1, nt - 1), 0, 0)),
                  rows, pl.BlockSpec(memory_space=pl.ANY),
                  pl.BlockSpec((tm, LANES), lambda i: (i, 0)), vec, vec],
        out_specs=rows, out_shape=jax.ShapeDtypeStruct((t, D_MODEL), F32),
        scratch_shapes=[pltpu.VMEM((2, 2 * tm // SUBLANES, SUBLANES, D_MODEL), F32),
                        pltpu.SemaphoreType.DMA((2,))],
        compiler_params=_params("arbitrary"),
        name="moe_combine_ln")(ids, ids, x, ys.reshape(-1, SUBLANES, D_MODEL), gates, g, b)


def _moe_layer(x, rw_pad, w1, w3, w2, g, b, tm=512):
    t = x.shape[0]
    eidx, gates, ranks, counts = _router(x, rw_pad)
    cnt = counts[0, :N_EXPERTS]
    padded = (cnt + tm - 1) // tm * tm
    pad_end = jnp.cumsum(padded)
    pad_start = pad_end - padded
    e2 = eidx[:, :2]
    dest = pad_start[e2] + ranks[:, :2]
    nblk = (2 * t) // tm + N_EXPERTS
    n_slots = nblk * tm
    tt = 512
    ids = dest.reshape(t // tt, tt, 2).transpose(0, 2, 1).reshape(t // tt, 1, 2 * tt)
    n_used = (pad_end[-1] // tm).astype(I32).reshape(1)
    blk = jnp.arange(nblk, dtype=I32) * tm
    block_e = jnp.minimum(jnp.sum(pad_end[None, :] <= blk[:, None], axis=1), N_EXPERTS - 1).astype(I32)
    block_e = jnp.where(jnp.arange(nblk) < n_used[0], block_e, block_e[jnp.maximum(n_used[0] - 1, 0)])
    unused = pad_end[-1] + jnp.arange(N_EXPERTS, dtype=I32) * tm
    fills = jnp.concatenate([pad_start + cnt, unused]).astype(I32)
    xs = _moe_dispatch(x, ids, fills, n_slots, tt, tm)
    ys = _moe_experts(block_e, n_used, xs, w1, w3, w2, tm)
    return _combine_ln(x, ys, ids, gates, g, b, tt)


def _block_diag(w):
    g, c, _ = w.shape
    eye = jnp.eye(g, dtype=w.dtype)
    return (eye[:, None, :, None] * w[:, :, None, :]).reshape(g * c, g * c)


def kernel(x, mem, positions, w_in, conv_w, a_log, dt_bias, dn_norm_w, pool_w, pool_scale, w_out, ln_mix_g, ln_mix_b, xq_w, xk_w, xv_w, xo_w, ln_x_g, ln_x_b, ffn_w1, ffn_w3, ffn_w2, router_w, moe_w1, moe_w3, moe_w2, ln_ffn_g, ln_ffn_b):
    batch, seq, _ = x.shape
    t = batch * seq
    depth = w_in.shape[0]
    xf = x.reshape(t, D_MODEL)
    memf = mem.reshape(batch * mem.shape[1], D_MODEL)

    posc, posr = {}, {}
    for dil in DILATIONS:
        pt = positions.reshape(batch, seq // dil, dil).transpose(0, 2, 1)
        posc[dil] = pt[..., None]
        posr[dil] = pt[:, :, None, :]

    for l in range(depth):
        slabs, hr = _in_proj(xf, w_in, l, conv_w[l], seq)

        branches = [_att_branch(slabs, posc[d], posr[d], d, batch, seq) for d in DILATIONS]
        pvec = jnp.zeros((2, LANES), F32)
        pvec = pvec.at[0, DN_HEADS:2 * DN_HEADS].set(a_log[l]).at[1, DN_HEADS:2 * DN_HEADS].set(dt_bias[l])
        normw = jnp.tile(dn_norm_w[l], DN_HEADS)[None, :]
        dn = _delta_net(hr, pvec, normw, batch, seq)

        pool = _pool(hr, _block_diag(pool_w[l]).astype(BF16), pool_scale[l][None, :], batch, seq)

        wo = w_out[l].astype(BF16)
        xf = _proj_ln([o for o, _ in branches], [s for _, s in branches], dn, pool, xf, wo[:ATT_WIDTH],
                      wo[ATT_WIDTH:ATT_WIDTH + DN_WIDTH], wo[ATT_WIDTH + DN_WIDTH:],
                      ln_mix_g[l][None, :], ln_mix_b[l][None, :])

        wkv = jnp.concatenate([xk_w[l], xv_w[l]], axis=1).astype(BF16)
        kv = _matmul(memf, wkv, BF16, 512, 1024).reshape(batch, mem.shape[1], 2 * D_MODEL)
        xf = _cross_attention(xf, (xq_w[l] * (X_HEAD_DIM ** -0.5)).astype(BF16), kv,
                              xo_w[l].astype(BF16), ln_x_g[l][None, :],
                              ln_x_b[l][None, :], seq)

        j = l // 2
        gl, bl = ln_ffn_g[l][None, :], ln_ffn_b[l][None, :]
        if l % 2 == 0:
            xf = _ffn_dense(xf, ffn_w1[j].astype(BF16), ffn_w3[j].astype(BF16), ffn_w2[j].astype(BF16), gl, bl)
        else:
            rw_pad = jnp.pad(router_w[j], ((0, 0), (0, LANES - N_EXPERTS)))
            xf = _moe_layer(xf, rw_pad, moe_w1[j].astype(BF16), moe_w3[j].astype(BF16),
                            moe_w2[j].astype(BF16), gl, bl)
    return xf.reshape(batch, seq, D_MODEL)
```

```python
import functools
import math

import jax
import jax.numpy as jnp
from jax import lax
from jax.experimental import pallas as pl
from jax.experimental.pallas import tpu as pltpu

F32, BF16, I32 = jnp.float32, jnp.bfloat16, jnp.int32

D_MODEL = 1024
HEAD_DIM = 64
ATT_HEADS = 6
ATT_WIDTH = ATT_HEADS * HEAD_DIM
DN_HEADS = 6
DN_WIDTH = DN_HEADS * HEAD_DIM
POOL_WIDTH = 256
POOL_WINDOWS = (2, 4, 8, 16)
POOL_GDIM = POOL_WIDTH // len(POOL_WINDOWS)
DILATIONS = (1, 4, 16)
CONV_K = 4
X_HEADS = 4
X_HEAD_DIM = D_MODEL // X_HEADS
N_EXPERTS = 8
DEPTH = 4
ALPHA = (2 * DEPTH) ** 0.25
LN_EPS = 1e-5
NORM_EPS = 1e-6

OFF_DN = 3 * ATT_WIDTH
OFF_BETA = OFF_DN + 3 * DN_WIDTH
OFF_DECAY = OFF_BETA + DN_HEADS
OFF_GATE = OFF_DECAY + DN_HEADS
OFF_POOL = OFF_GATE + DN_WIDTH

LANES = 128
MXU_WIDTH = 256
R_DN = 0
R_GATE = R_DN + 3 * DN_WIDTH
R_POOL = R_GATE + DN_WIDTH
R_SMALL = R_POOL + POOL_WIDTH
R_COLS = R_SMALL + LANES

QB = 128
ATT_STEP_TOKENS = 2048
ATT_MAX_ROWS = 512
NEG = -1e30
VMEM_LIMIT = 48 * 1024 * 1024


def _params(*sem):
    return pltpu.CompilerParams(dimension_semantics=sem, vmem_limit_bytes=VMEM_LIMIT)


def _sigmoid(x):
    return 1.0 / (1.0 + jnp.exp(-x))


def _layer_norm_rows(z, g, b):
    mu = jnp.mean(z, axis=-1, keepdims=True)
    zc = z - mu
    var = jnp.mean(zc * zc, axis=-1, keepdims=True)
    return zc * lax.rsqrt(var + LN_EPS) * g + b


def _rows(start, size, stride):
    return pl.ds(start, size) if stride == 1 else pl.ds(start, size, stride=stride)


def _mm_body(x_ref, w_ref, o_ref):
    o_ref[...] = jnp.dot(x_ref[...].astype(BF16), w_ref[...],
                         preferred_element_type=F32).astype(o_ref.dtype)


def _matmul(x, w, out_dtype, tm, tn):
    m, k = x.shape
    n = w.shape[1]
    return pl.pallas_call(
        _mm_body, grid=(m // tm, n // tn),
        in_specs=[pl.BlockSpec((tm, k), lambda i, j: (i, 0)),
                  pl.BlockSpec((k, tn), lambda i, j: (0, j))],
        out_specs=pl.BlockSpec((tm, tn), lambda i, j: (i, j)),
        out_shape=jax.ShapeDtypeStruct((m, n), out_dtype),
        compiler_params=_params("parallel", "arbitrary"), name="matmul")(x, w)


CONV_HALO = 8


def _in_proj_body(x_ref, w_ref, cw_ref, slab_ref, o_ref, wb, hist, cbuf, *, tm, tiles_per_seq):
    i = pl.program_id(0)
    ncv = 3 * DN_WIDTH

    @pl.when(i == 0)
    def _():
        wb[:, 0:OFF_BETA] = w_ref[:, 0:OFF_BETA].astype(BF16)
        c0 = OFF_DN + R_GATE
        wb[:, c0:c0 + DN_WIDTH] = w_ref[:, OFF_GATE:OFF_POOL].astype(BF16)
        c0 = OFF_DN + R_POOL
        wb[:, c0:c0 + POOL_WIDTH] = w_ref[:, OFF_POOL:].astype(BF16)
        c0 = OFF_DN + R_SMALL
        wb[:, c0:c0 + LANES] = jnp.zeros((wb.shape[0], LANES), BF16)
        wb[:, c0:c0 + 2 * DN_HEADS] = w_ref[:, OFF_BETA:OFF_GATE].astype(BF16)

    @pl.when(lax.rem(i, tiles_per_seq) == 0)
    def _():
        hist[...] = jnp.zeros_like(hist)

    xb = x_ref[...].astype(BF16)
    for j in range(slab_ref.shape[0]):
        if j % 2 == 0:
            res = jnp.dot(xb, wb[:, j * LANES:min(j + 2, slab_ref.shape[0]) * LANES],
                          preferred_element_type=F32)
        slab_ref[j] = res[:, (j % 2) * LANES:(j % 2 + 1) * LANES]

    n = o_ref.shape[1]
    starts = list(range(0, ncv, MXU_WIDTH)) + list(range(ncv, n, MXU_WIDTH))
    for c0, c1 in zip(starts, starts[1:] + [n]):
        cs = slice(c0, c1)
        res = jnp.dot(xb, wb[:, OFF_DN + c0:OFF_DN + c1], preferred_element_type=F32)
        if c0 >= ncv:
            o_ref[:, cs] = res
            continue
        cbuf[0:CONV_HALO, cs] = hist[:, cs]
        cbuf[CONV_HALO:, cs] = res
        hist[:, cs] = cbuf[tm:, cs]
        acc = None
        for j in range(CONV_K):
            start = CONV_HALO - (CONV_K - 1) + j
            t = cbuf[start:start + tm, cs] * cw_ref[j:j + 1, cs]
            acc = t if acc is None else acc + t
        o_ref[:, cs] = acc * _sigmoid(acc)


def _in_proj(x, w_in, layer, conv_w, seq, tm=512):
    m, k = x.shape
    nslab = OFF_DN // LANES
    return pl.pallas_call(
        functools.partial(_in_proj_body, tm=tm, tiles_per_seq=seq // tm), grid=(m // tm,),
        in_specs=[pl.BlockSpec((tm, k), lambda i: (i, 0)),
                  pl.BlockSpec((None,) + w_in.shape[1:], lambda i: (layer, 0, 0),
                               pipeline_mode=pl.Buffered(1)),
                  pl.BlockSpec(conv_w.shape, lambda i: (0, 0))],
        out_specs=[pl.BlockSpec((nslab, tm, LANES), lambda i: (0, i, 0)),
                   pl.BlockSpec((tm, R_COLS), lambda i: (i, 0))],
        out_shape=[jax.ShapeDtypeStruct((nslab, m, LANES), F32),
                   jax.ShapeDtypeStruct((m, R_COLS), F32)],
        scratch_shapes=[pltpu.VMEM((k, OFF_DN + R_COLS), BF16),
                        pltpu.VMEM((CONV_HALO, 3 * DN_WIDTH), F32),
                        pltpu.VMEM((tm + CONV_HALO, 3 * DN_WIDTH), F32)],
        compiler_params=_params("arbitrary"), name="in_proj_conv")(x, w_in, conv_w)


def _att_body(q_ref, kc_ref, vc_ref, pq_ref, pkc_ref, o_ref, lse_ref,
              kbuf, vbuf, pkbuf, kprev, vprev, pkprev, *, dil, rows, slopes):
    n = pl.program_id(1)
    nsub = rows // QB
    npair = ATT_HEADS // 2

    @pl.when(n == 0)
    def _():
        kprev[...] = jnp.zeros_like(kprev)
        vprev[...] = jnp.zeros_like(vprev)
        pkprev[...] = jnp.zeros_like(pkprev)

    a = lax.broadcasted_iota(I32, (QB, 2 * QB), 0)
    c = lax.broadcasted_iota(I32, (QB, 2 * QB), 1)
    tri = jnp.where(c < QB, c - a, a - (c - QB)) >= 0
    lane = lax.broadcasted_iota(I32, (QB, LANES), 1)
    low = lane < HEAD_DIM

    def residue(r, carry):
        for p in range(npair):
            kbuf[p, 0:QB, :] = kprev[r, p]
            kbuf[p, QB:, :] = kc_ref[p, _rows(r, rows, dil), :].astype(BF16)
            vbuf[p, 0:QB, :] = vprev[r, p]
            vbuf[p, QB:, :] = vc_ref[p, _rows(r, rows, dil), :].astype(BF16)
            kprev[r, p] = kbuf[p, rows:, :]
            vprev[r, p] = vbuf[p, rows:, :]
        pkbuf[0] = pkprev[r]
        for j in range(nsub):
            pkbuf[j + 1] = pkc_ref[r, :, j * QB:(j + 1) * QB]
        pkprev[r] = pkbuf[nsub]

        def sub(sb, carry2):
            r0 = pl.multiple_of(sb * QB, QB)
            t0 = r + r0 * dil
            pq = pq_ref[r, pl.ds(r0, QB), :]
            pk = jnp.concatenate([pkbuf[sb], pkbuf[sb + 1]], axis=1)
            dist = (pq - pk).astype(F32)
            mask = tri & (c >= jnp.where((sb > 0) | (n > 0), 0, QB))
            heads = range(ATT_HEADS)
            ks = [kbuf[p, pl.ds(r0, 2 * QB), :] for p in range(npair)]
            vs = [vbuf[p, pl.ds(r0, 2 * QB), :] for p in range(npair)]
            qs = [q_ref[p, _rows(t0, QB, dil), :] * (HEAD_DIM ** -0.5) for p in range(npair)]
            qms = [jnp.where(low if h % 2 == 0 else ~low, qs[h // 2], 0.0).astype(BF16) for h in heads]
            ss = [lax.dot_general(qms[h], ks[h // 2], (((1,), (1,)), ((), ())), preferred_element_type=F32)
                  for h in heads]
            ss = [jnp.where(mask, ss[h] - slopes[h] * dist, NEG) for h in heads]
            ms = [jnp.max(ss[h], axis=-1, keepdims=True) for h in heads]
            es = [jnp.exp(ss[h] - ms[h]) for h in heads]
            ls = [jnp.sum(es[h], axis=-1, keepdims=True) for h in heads]
            outs = [jnp.dot(es[h].astype(BF16), vs[h // 2], preferred_element_type=F32) / ls[h] for h in heads]
            lse = jnp.zeros((QB, LANES), F32)
            for h in heads:
                lse = jnp.where(lane == h, ms[h] + jnp.log(ls[h]), lse)
            for p in range(npair):
                o_ref[p, _rows(t0, QB, dil), :] = jnp.where(low, outs[2 * p], outs[2 * p + 1])
            lse_ref[_rows(t0, QB, dil), :] = lse
            return carry2

        lax.fori_loop(0, nsub, sub, 0, unroll=min(nsub, 2))
        return carry

    lax.fori_loop(0, dil, residue, 0)


def _att_branch(slabs, posc, posr, dil, batch, seq):
    sub_len = seq // dil
    rows = min(ATT_STEP_TOKENS // max(dil, ATT_STEP_TOKENS // ATT_MAX_ROWS), sub_len)
    tok = rows * dil
    nsteps = seq // tok
    npair = ATT_HEADS // 2
    slopes = tuple(2.0 ** (-8.0 * (i + 1) / ATT_HEADS) for i in range(ATT_HEADS))

    def cur(which):
        return pl.BlockSpec((npair, tok, LANES), lambda b, n: (which, b * nsteps + n, 0))

    t = batch * seq
    return pl.pallas_call(
        functools.partial(_att_body, dil=dil, rows=rows, slopes=slopes),
        grid=(batch, nsteps),
        in_specs=[cur(0), cur(1), cur(2),
                  pl.BlockSpec((None, dil, rows, 1), lambda b, n: (b, 0, n, 0)),
                  pl.BlockSpec((None, dil, 1, rows), lambda b, n: (b, 0, 0, n))],
        out_specs=[pl.BlockSpec((npair, tok, LANES), lambda b, n: (0, b * nsteps + n, 0)),
                   pl.BlockSpec((tok, LANES), lambda b, n: (b * nsteps + n, 0))],
        out_shape=[jax.ShapeDtypeStruct((npair, t, LANES), F32),
                   jax.ShapeDtypeStruct((t, LANES), F32)],
        scratch_shapes=[pltpu.VMEM((npair, rows + QB, LANES), BF16),
                        pltpu.VMEM((npair, rows + QB, LANES), BF16),
                        pltpu.VMEM((rows // QB + 1, 1, QB), I32),
                        pltpu.VMEM((dil, npair, QB, LANES), BF16),
                        pltpu.VMEM((dil, npair, QB, LANES), BF16),
                        pltpu.VMEM((dil, 1, QB), I32)],
        compiler_params=_params("arbitrary", "arbitrary"),
        name=f"dilated_att_d{dil}")(slabs, slabs, slabs, posc, posr)


def _merge_branches(os_, ls_):
    ls = [l[...] for l in ls_]
    m = functools.reduce(jnp.maximum, ls)
    ws = [jnp.exp(l - m) for l in ls]
    inv = 1.0 / functools.reduce(lambda a, b: a + b, ws)
    ws = [w * inv for w in ws]
    lane = lax.broadcasted_iota(I32, ls[0].shape, 1)
    low = lane < HEAD_DIM
    pairs = []
    for p in range(ATT_HEADS // 2):
        acc = None
        for w, o in zip(ws, os_):
            wp = jnp.where(low, w[:, 2 * p:2 * p + 1], w[:, 2 * p + 1:2 * p + 2])
            t = wp * o[p]
            acc = t if acc is None else acc + t
        pairs.append(acc)
    return jnp.concatenate(pairs, axis=1)


def _split2(a):
    hi = a.astype(BF16)
    lo = (a - hi.astype(F32)).astype(BF16)
    return hi, lo


def _dot(a, b):
    return jnp.dot(a, b, preferred_element_type=F32)


def _dot_bf16(a, b):
    return _dot(a.astype(BF16), b.astype(BF16))


def _dot_hp(a, b):
    ah, al = _split2(a)
    bh, bl = _split2(b)
    return _dot(jnp.concatenate([ah, al, ah], axis=1), jnp.concatenate([bh, bh, bl], axis=0))


def _dot_t(a, b):
    return lax.dot_general(a, b, (((1,), (1,)), ((), ())), preferred_element_type=F32)


def _dot_t_hp(a, b):
    ah, al = _split2(a)
    bh, bl = _split2(b)
    return _dot_t(jnp.concatenate([ah, al, ah], axis=1), jnp.concatenate([bh, bh, bl], axis=1))


def _dn_body(hq, hk, hv, small, gate, pvec, normw, out, s_ref, *, nch):
    n = pl.program_id(1)
    c_ = QB
    npair = DN_HEADS // 2
    qkv = (hq, hk, hv)

    @pl.when(n == 0)
    def _():
        s_ref[...] = jnp.zeros_like(s_ref)

    lane = lax.broadcasted_iota(I32, (c_, LANES), 1)
    low = lane < HEAD_DIM
    row = lax.broadcasted_iota(I32, (c_, c_), 0)
    col = lax.broadcasted_iota(I32, (c_, c_), 1)
    lower = row >= col
    strict = row > col
    eye = jnp.where(row == col, 1.0, 0.0)
    blockdiag = (row < HEAD_DIM) == (col < HEAD_DIM)

    def headsum(x):
        sa = jnp.sum(jnp.where(low, x, 0.0), axis=-1, keepdims=True)
        sb = jnp.sum(jnp.where(low, 0.0, x), axis=-1, keepdims=True)
        return jnp.where(low, sa, sb)

    def col_of(arr, j):
        return arr[:, j:j + 1]

    ltri = jnp.where(lower, 1.0, 0.0).astype(BF16)
    ltri3 = jnp.concatenate([ltri, ltri, ltri], axis=1)
    pair, xs, pws, qks = [], [], [], []
    for ch in range(nch):
        rs = slice(ch * c_, (ch + 1) * c_)
        sm = small[rs, :]
        beta_all = _sigmoid(sm)
        z = sm + pvec[1:2, :]
        softplus = jnp.maximum(z, 0.0) + jnp.log(1.0 + jnp.exp(-jnp.abs(z)))
        g_all = -jnp.exp(pvec[0:1, :]) * softplus
        g1 = g_all.astype(BF16)
        rem = g_all - g1.astype(F32)
        g2 = rem.astype(BF16)
        g3 = (rem - g2.astype(F32)).astype(BF16)
        gc_all = _dot(ltri3, jnp.concatenate([g1, g2, g3], axis=0))
        gc_t = gc_all.T

        for p in range(npair):
            cs = slice(p * LANES, (p + 1) * LANES)
            q, k, v = qkv[0][rs, cs], qkv[1][rs, cs], qkv[2][rs, cs]
            q = q * lax.rsqrt(headsum(q * q) + NORM_EPS) * (HEAD_DIM ** -0.5)
            k = k * lax.rsqrt(headsum(k * k) + NORM_EPS)
            beta = jnp.where(low, col_of(beta_all, 2 * p), col_of(beta_all, 2 * p + 1))
            gc = jnp.where(low, col_of(gc_all, DN_HEADS + 2 * p), col_of(gc_all, DN_HEADS + 2 * p + 1))
            eg = jnp.exp(gc)
            rhs = jnp.concatenate([v * beta, k * beta * eg], axis=1)
            pair.append(dict(q=q, k=k, kb=k.astype(BF16), gc=gc, eg=eg, rhs=rhs))

        for h in range(DN_HEADS):
            pr = pair[ch * npair + h // 2]
            sel = low if h % 2 == 0 else ~low
            gcol = col_of(gc_all, DN_HEADS + h)
            grow = gc_t[DN_HEADS + h:DN_HEADS + h + 1, :]
            dec = jnp.where(lower, jnp.exp(jnp.minimum(gcol - grow, 0.0)), 0.0)
            kk = _dot_t(jnp.where(sel, pr["k"], 0.0).astype(BF16), pr["kb"])
            qk = _dot_t_hp(jnp.where(sel, pr["q"], 0.0), pr["k"])
            nm = jnp.where(strict, col_of(beta_all, h) * kk * dec, 0.0)
            xs.append(eye - nm)
            pws.append(nm)
            qks.append((qk * dec).astype(BF16))

    for level in range(int(math.log2(c_)) - 1):
        mm = _dot_hp if level < 3 else _dot_bf16
        for i in range(len(xs)):
            pws[i] = mm(pws[i], pws[i])
        for i in range(len(xs)):
            xs[i] = xs[i] + mm(xs[i], pws[i])
    uws = [_dot_hp(xs[i], pair[i // 2]["rhs"]) for i in range(len(xs))]

    for ch in range(nch):
        rs = slice(ch * c_, (ch + 1) * c_)
        for p in range(npair):
            pr = pair[ch * npair + p]
            i0 = ch * DN_HEADS + 2 * p
            cs = slice(p * LANES, (p + 1) * LANES)
            u = jnp.where(low, uws[i0][:, :LANES], uws[i0 + 1][:, :LANES])
            w = jnp.where(low, uws[i0][:, LANES:], uws[i0 + 1][:, LANES:])
            state = s_ref[p]
            sb16 = state.astype(BF16)
            vnew = u - _dot(w.astype(BF16), sb16)
            vn16 = vnew.astype(BF16)
            o = _dot((pr["q"] * pr["eg"]).astype(BF16), sb16)
            o = o + jnp.where(low, _dot(qks[i0], vn16), _dot(qks[i0 + 1], vn16))
            glast = pr["gc"][c_ - 1:c_, :]
            kd = pr["k"] * jnp.exp(glast - pr["gc"])
            upd = _dot(kd.T.astype(BF16), vn16)
            s_ref[p] = state * jnp.exp(glast) + jnp.where(blockdiag, upd, 0.0)
            ms = headsum(o * o) * (1.0 / HEAD_DIM)
            gt = gate[rs, cs]
            out[rs, cs] = (o * lax.rsqrt(ms + NORM_EPS) * normw[:, cs] * (gt * _sigmoid(gt))).astype(out.dtype)


def _delta_net(hr, pvec, normw, batch, seq, nch=2):
    h3 = hr.reshape(batch, seq, R_COLS)
    rows = nch * QB
    qcol = R_DN // DN_WIDTH

    def cur(off):
        return pl.BlockSpec((None, rows, DN_WIDTH), lambda b, n: (b, n, qcol + off))

    out = pl.pallas_call(
        functools.partial(_dn_body, nch=nch), grid=(batch, seq // rows),
        in_specs=[cur(0), cur(1), cur(2),
                  pl.BlockSpec((None, rows, LANES), lambda b, n: (b, n, R_SMALL // LANES)),
                  pl.BlockSpec((None, rows, DN_WIDTH), lambda b, n: (b, n, R_GATE // DN_WIDTH)),
                  pl.BlockSpec((2, LANES), lambda b, n: (0, 0)),
                  pl.BlockSpec((1, DN_WIDTH), lambda b, n: (0, 0))],
        out_specs=pl.BlockSpec((None, rows, DN_WIDTH), lambda b, n: (b, n, 0)),
        out_shape=jax.ShapeDtypeStruct((batch, seq, DN_WIDTH), BF16),
        scratch_shapes=[pltpu.VMEM((DN_HEADS // 2, LANES, LANES), F32)],
        compiler_params=_params("parallel", "arbitrary"),
        name="gated_delta_rule")(h3, h3, h3, h3, h3, pvec, normw)
    return out.reshape(batch * seq, DN_WIDTH)


def _pool_body(cur, prev, wbd, scale, out, xbuf, *, rows):
    n = pl.program_id(1)
    halo = 16
    xbuf[0:halo, :] = prev[...] * jnp.where(n > 0, 1.0, 0.0)
    xbuf[halo:, :] = cur[...]
    u = cur[...]
    lane = lax.broadcasted_iota(I32, (rows, POOL_WIDTH), 1)
    tpos = n * rows + lax.broadcasted_iota(I32, (rows, POOL_WIDTH), 0)
    run = u
    pooled = jnp.zeros_like(u)
    for j in range(1, max(POOL_WINDOWS)):
        run = run + xbuf[halo - j:halo - j + rows, :]
        if j + 1 in POOL_WINDOWS:
            gi = POOL_WINDOWS.index(j + 1)
            cnt = jnp.minimum(tpos + 1, j + 1).astype(F32)
            sel = (lane >= gi * POOL_GDIM) & (lane < (gi + 1) * POOL_GDIM)
            pooled = jnp.where(sel, run / cnt - u, pooled)
    mixed = jnp.dot(pooled.astype(BF16), wbd[...], preferred_element_type=F32)
    out[...] = (mixed * scale[...]).astype(out.dtype)


def _pool(hr, wbd, scale, batch, seq, rows=512):
    h3 = hr.reshape(batch, seq, R_COLS)
    out = pl.pallas_call(
        functools.partial(_pool_body, rows=rows), grid=(batch, seq // rows),
        in_specs=[pl.BlockSpec((None, rows, POOL_WIDTH), lambda b, n: (b, n, R_POOL // POOL_WIDTH)),
                  pl.BlockSpec((None, 16, POOL_WIDTH),
                               lambda b, n: (b, jnp.maximum(n * (rows // 16) - 1, 0), R_POOL // POOL_WIDTH)),
                  pl.BlockSpec((POOL_WIDTH, POOL_WIDTH), lambda b, n: (0, 0)),
                  pl.BlockSpec((1, POOL_WIDTH), lambda b, n: (0, 0))],
        out_specs=pl.BlockSpec((None, rows, POOL_WIDTH), lambda b, n: (b, n, 0)),
        out_shape=jax.ShapeDtypeStruct((batch, seq, POOL_WIDTH), BF16),
        scratch_shapes=[pltpu.VMEM((rows + 16, POOL_WIDTH), F32)],
        compiler_params=_params("parallel", "arbitrary"), name="multiscale_pool")(h3, h3, wbd, scale)
    return out.reshape(batch * seq, POOL_WIDTH)


def _proj_ln_body(o1, o2, o3, l1, l2, l3, d, p, x, wa, wd, wp, g, b, o):
    att = _merge_branches((o1, o2, o3), (l1, l2, l3)).astype(BF16)
    y = jnp.dot(att, wa[...], preferred_element_type=F32)
    y = y + jnp.dot(d[...], wd[...], preferred_element_type=F32)
    y = y + jnp.dot(p[...], wp[...], preferred_element_type=F32)
    o[...] = _layer_norm_rows(ALPHA * x[...] + y, g[...], b[...])


def _proj_ln(branch_o, branch_lse, dn, pool, x, wa, wd, wp, g, b, tm=512):
    t = x.shape[0]
    npair = ATT_HEADS // 2

    def rows(wd_):
        return pl.BlockSpec((tm, wd_), lambda i: (i, 0))

    def full(a_):
        return pl.BlockSpec(a_.shape, lambda i: (0, 0))

    slab = pl.BlockSpec((npair, tm, LANES), lambda i: (0, i, 0))
    return pl.pallas_call(
        _proj_ln_body, grid=(t // tm,),
        in_specs=[slab] * 3 + [rows(LANES)] * 3 + [rows(DN_WIDTH), rows(POOL_WIDTH), rows(D_MODEL),
                                                    full(wa), full(wd), full(wp), full(g), full(b)],
        out_specs=rows(D_MODEL), out_shape=jax.ShapeDtypeStruct((t, D_MODEL), F32),
        compiler_params=_params("parallel"),
        name="out_proj_ln")(*branch_o, *branch_lse, dn, pool, x, wa, wd, wp, g, b)


def _xattn_body(x, wq, k, v, wo, g, b, o):
    xv = x[...]
    q = jnp.dot(xv.astype(BF16), wq[...], preferred_element_type=F32).astype(BF16)
    cols = [slice(h * X_HEAD_DIM, (h + 1) * X_HEAD_DIM) for h in range(X_HEADS)]
    ss = [_dot_t(q[:, cs], k[:, cs]) for cs in cols]
    ms = [jnp.max(s, axis=-1, keepdims=True) for s in ss]
    es = [jnp.exp(s - m) for s, m in zip(ss, ms)]
    ls = [jnp.sum(e, axis=-1, keepdims=True) for e in es]
    heads = [(_dot(e.astype(BF16), v[:, cs]) / l).astype(BF16) for e, l, cs in zip(es, ls, cols)]
    oc = jnp.concatenate(heads, axis=1)
    y = jnp.dot(oc, wo[...], preferred_element_type=F32)
    o[...] = _layer_norm_rows(ALPHA * xv + y, g[...], b[...])


def _cross_attention(x, wq, kv, wo, g, b, seq, tm=1024):
    t = x.shape[0]
    mem_len = kv.shape[1]
    per_batch = seq // tm

    def full(a_):
        return pl.BlockSpec(a_.shape, lambda i: (0, 0))

    def half(which):
        return pl.BlockSpec((None, mem_len, D_MODEL), lambda i: (i // per_batch, 0, which))

    rows = pl.BlockSpec((tm, D_MODEL), lambda i: (i, 0))
    return pl.pallas_call(
        _xattn_body, grid=(t // tm,),
        in_specs=[rows, full(wq), half(0), half(1), full(wo), full(g), full(b)],
        out_specs=rows, out_shape=jax.ShapeDtypeStruct((t, D_MODEL), F32),
        compiler_params=_params("parallel"), name="cross_attention_ln")(x, wq, kv, kv, wo, g, b)


def _ffn_body(x, w1, w3, w2, g, b, o, *, chunks):
    xv = x[...]
    xb = xv.astype(BF16)
    y = None
    for c0, c1 in chunks:
        h1 = jnp.dot(xb, w1[:, c0:c1], preferred_element_type=F32)
        h3 = jnp.dot(xb, w3[:, c0:c1], preferred_element_type=F32)
        hh = (h1 * _sigmoid(h1) * h3).astype(BF16)
        t = jnp.dot(hh, w2[c0:c1, :], preferred_element_type=F32)
        y = t if y is None else y + t
    o[...] = _layer_norm_rows(ALPHA * xv + y, g[...], b[...])


def _ffn_dense(x, w1, w3, w2, g, b, tm=512, chunk=6 * MXU_WIDTH):
    t = x.shape[0]
    dff = w1.shape[1]
    chunks = tuple((c0, min(c0 + chunk, dff)) for c0 in range(0, dff, chunk))
    rows = pl.BlockSpec((tm, D_MODEL), lambda i: (i, 0))
    vec = pl.BlockSpec((1, D_MODEL), lambda i: (0, 0))

    def resident(a_):
        return pl.BlockSpec(a_.shape, lambda i: (0, 0), pipeline_mode=pl.Buffered(1))

    return pl.pallas_call(
        functools.partial(_ffn_body, chunks=chunks), grid=(t // tm,),
        in_specs=[rows, resident(w1), resident(w3), resident(w2), vec, vec],
        out_specs=rows, out_shape=jax.ShapeDtypeStruct((t, D_MODEL), F32),
        compiler_params=_params("parallel"), name="ffn_swiglu_ln")(x, w1, w3, w2, g, b)


def _router_body(x, rw, eidx, gates, ranks, counts, carry, *, tm):
    i = pl.program_id(0)

    @pl.when(i == 0)
    def _():
        carry[...] = jnp.zeros_like(carry)

    logits = _dot_hp(x[...], rw[...])
    lane = lax.broadcasted_iota(I32, (tm, LANES), 1)
    lanef = lane.astype(F32)
    lg = jnp.where(lane < N_EXPERTS, logits, NEG)
    m1 = jnp.max(lg, axis=-1, keepdims=True)
    i1 = jnp.min(jnp.where(lg == m1, lanef, float(LANES)), axis=-1, keepdims=True)
    lg2 = jnp.where(lanef == i1, NEG, lg)
    m2 = jnp.max(lg2, axis=-1, keepdims=True)
    i2 = jnp.min(jnp.where(lg2 == m2, lanef, float(LANES)), axis=-1, keepdims=True)
    e21 = jnp.exp(m2 - m1)
    g1 = 1.0 / (1.0 + e21)
    g2 = e21 * g1
    hot1 = lanef == i1
    hot2 = lanef == i2
    onehot = jnp.where(hot1 | hot2, 1.0, 0.0)
    r = lax.broadcasted_iota(I32, (tm, tm), 0)
    c = lax.broadcasted_iota(I32, (tm, tm), 1)
    before = jnp.dot(jnp.where(r > c, 1.0, 0.0).astype(BF16), onehot.astype(BF16),
                     preferred_element_type=F32) + carry[...]
    rank1 = jnp.sum(jnp.where(hot1, before, 0.0), axis=-1, keepdims=True)
    rank2 = jnp.sum(jnp.where(hot2, before, 0.0), axis=-1, keepdims=True)
    carry[...] = carry[...] + jnp.sum(onehot, axis=0, keepdims=True)
    eidx[...] = jnp.where(lane == 0, i1, jnp.where(lane == 1, i2, 0.0)).astype(I32)
    gates[...] = jnp.where(lane == 0, g1, jnp.where(lane == 1, g2, 0.0))
    ranks[...] = jnp.where(lane == 0, rank1, jnp.where(lane == 1, rank2, 0.0)).astype(I32)
    counts[...] = jnp.broadcast_to(carry[...], counts.shape).astype(I32)


def _router(x, rw, tm=512):
    t = x.shape[0]
    rows = pl.BlockSpec((tm, LANES), lambda i: (i, 0))
    return pl.pallas_call(
        functools.partial(_router_body, tm=tm), grid=(t // tm,),
        in_specs=[pl.BlockSpec((tm, D_MODEL), lambda i: (i, 0)),
                  pl.BlockSpec((D_MODEL, LANES), lambda i: (0, 0))],
        out_specs=[rows, rows, rows, pl.BlockSpec((8, LANES), lambda i: (0, 0))],
        out_shape=[jax.ShapeDtypeStruct((t, LANES), I32), jax.ShapeDtypeStruct((t, LANES), F32),
                   jax.ShapeDtypeStruct((t, LANES), I32), jax.ShapeDtypeStruct((8, LANES), I32)],
        scratch_shapes=[pltpu.VMEM((1, LANES), F32)],
        compiler_params=_params("arbitrary"), name="router_top2")(x, rw)


def _moe_body(be, nused, xs, w1, w3, w2, ys, xb, acc):
    i = pl.program_id(0)
    f = pl.program_id(1)
    last = f == pl.num_programs(1) - 1
    used = i < nused[0]

    @pl.when(used & (f == 0))
    def _():
        xb[...] = xs[...].astype(BF16)
        acc[...] = jnp.zeros_like(acc)

    @pl.when(used)
    def _():
        xv = xb[...]
        h1 = jnp.dot(xv, w1[...], preferred_element_type=F32)
        h3 = jnp.dot(xv, w3[...], preferred_element_type=F32)
        hh = (h1 * _sigmoid(h1) * h3).astype(BF16)
        acc[...] += jnp.dot(hh, w2[...], preferred_element_type=F32)

    @pl.when(used & last)
    def _():
        ys[...] = acc[...]

    @pl.when(jnp.logical_not(used) & last)
    def _():
        ys[...] = jnp.zeros_like(ys)


def _moe_experts(block_e, n_used, xs, w1, w3, w2, layer, tm, tf=1792):
    n_slots = xs.shape[0]
    dex = w1.shape[3]
    nblk = n_slots // tm
    nf = dex // tf

    def live(i, nu):
        return jnp.minimum(i, nu[0] - 1)

    def chunk(i, f, nu):
        return jnp.where(i < nu[0], f, nf - 1)

    grid_spec = pltpu.PrefetchScalarGridSpec(
        num_scalar_prefetch=2, grid=(nblk, nf),
        in_specs=[pl.BlockSpec((tm, D_MODEL), lambda i, f, be, nu: (live(i, nu), 0)),
                  pl.BlockSpec((None, None, D_MODEL, tf), lambda i, f, be, nu: (layer, be[i], 0, chunk(i, f, nu))),
                  pl.BlockSpec((None, None, D_MODEL, tf), lambda i, f, be, nu: (layer, be[i], 0, chunk(i, f, nu))),
                  pl.BlockSpec((None, None, tf, D_MODEL), lambda i, f, be, nu: (layer, be[i], chunk(i, f, nu), 0))],
        out_specs=pl.BlockSpec((tm, D_MODEL), lambda i, f, be, nu: (i, 0)),
        scratch_shapes=[pltpu.VMEM((tm, D_MODEL), BF16), pltpu.VMEM((tm, D_MODEL), F32)])
    return pl.pallas_call(
        _moe_body, grid_spec=grid_spec, out_shape=jax.ShapeDtypeStruct((n_slots, D_MODEL), F32),
        compiler_params=_params("arbitrary", "arbitrary"), name="moe_experts")(block_e, n_used, xs, w1, w3, w2)


SUBLANES = 8


def _tile_row(ref3, idx):
    tile = lax.shift_right_logical(idx, SUBLANES.bit_length() - 1)
    return ref3.at[tile, pl.ds(idx & (SUBLANES - 1), 1), :]


def _start_row_gather(src, ids, buf, sem, slot, nrows):
    def issue(j, carry):
        for u in range(SUBLANES):
            idx = ids[0, j * SUBLANES + u]
            pltpu.make_async_copy(_tile_row(src, idx), buf.at[slot, j, pl.ds(u, 1), :], sem.at[slot]).start()
        return carry

    lax.fori_loop(0, nrows // SUBLANES, issue, 0)


def _wait_row_gather(src, buf, sem, slot, nrows):
    pltpu.make_async_copy(src.at[pl.ds(0, nrows // SUBLANES)], buf.at[slot], sem.at[slot]).wait()


def _pipelined_row_gather(src, ids_cur, ids_next, buf, sem, nrows):
    i = pl.program_id(0)
    slot = lax.rem(i, 2)

    @pl.when(i == 0)
    def _():
        _start_row_gather(src, ids_cur, buf, sem, 0, nrows)

    @pl.when(i + 1 < pl.num_programs(0))
    def _():
        _start_row_gather(src, ids_next, buf, sem, 1 - slot, nrows)

    _wait_row_gather(src, buf, sem, slot, nrows)
    return slot


def _dispatch_body(tails, ids, x, xs, zbuf, sem, *, tm):
    fill = zbuf.shape[0]

    @pl.when(pl.program_id(0) == 0)
    def _():
        zbuf[...] = jnp.zeros_like(zbuf)

        def tail_fill(e):
            t0 = jnp.minimum(lax.shift_right_logical(tails[e], 3), xs.shape[0] - fill)
            return pltpu.make_async_copy(zbuf, xs.at[pl.ds(t0, fill)], sem)

        for e in range(tails.shape[0]):
            tail_fill(e).start()
            tail_fill(e).wait()

    def issue(j, carry):
        for u in range(SUBLANES):
            for k in range(2):
                dst = _tile_row(xs, ids[0, k * tm + j * SUBLANES + u])
                pltpu.make_async_copy(x.at[j, pl.ds(u, 1), :], dst, sem).start()
        return carry

    lax.fori_loop(0, tm // SUBLANES, issue, 0)
    for _ in range(2):
        pltpu.make_async_copy(x, xs.at[pl.ds(0, tm // SUBLANES)], sem).wait()


def _moe_dispatch(x, ids, tails, n_slots, tm, max_tail):
    t = x.shape[0]
    grid_spec = pltpu.PrefetchScalarGridSpec(
        num_scalar_prefetch=1, grid=(t // tm,),
        in_specs=[pl.BlockSpec((None, 1, 2 * tm), lambda i, tl: (i, 0, 0), memory_space=pltpu.SMEM),
                  pl.BlockSpec((tm // SUBLANES, SUBLANES, D_MODEL), lambda i, tl: (i, 0, 0))],
        out_specs=pl.BlockSpec(memory_space=pl.ANY),
        scratch_shapes=[pltpu.VMEM((max_tail // SUBLANES + 1, SUBLANES, D_MODEL), F32),
                        pltpu.SemaphoreType.DMA(())])
    xs = pl.pallas_call(
        functools.partial(_dispatch_body, tm=tm), grid_spec=grid_spec,
        out_shape=jax.ShapeDtypeStruct((n_slots // SUBLANES, SUBLANES, D_MODEL), F32),
        compiler_params=_params("arbitrary"),
        name="moe_dispatch_scatter")(tails, ids, x.reshape(t // SUBLANES, SUBLANES, D_MODEL))
    return xs.reshape(n_slots, D_MODEL)


def _combine_ln_body(ids_cur, ids_next, x, ys, gates, g, b, o, buf, sem, *, tm):
    slot = _pipelined_row_gather(ys, ids_cur, ids_next, buf, sem, 2 * tm)
    gt = gates[...]
    nt = tm // SUBLANES
    ya = buf[slot, 0:nt].reshape(tm, D_MODEL)
    yb = buf[slot, nt:].reshape(tm, D_MODEL)
    o[...] = _layer_norm_rows(ALPHA * x[...] + (ya * gt[:, 0:1] + yb * gt[:, 1:2]), g[...], b[...])


def _combine_ln(x, ys, ids, gates, g, b, tm):
    t = x.shape[0]
    nt = t // tm
    smem = functools.partial(pl.BlockSpec, (None, 1, 2 * tm), memory_space=pltpu.SMEM)
    rows = pl.BlockSpec((tm, D_MODEL), lambda i: (i, 0))
    vec = pl.BlockSpec((1, D_MODEL), lambda i: (0, 0))
    return pl.pallas_call(
        functools.partial(_combine_ln_body, tm=tm), grid=(nt,),
        in_specs=[smem(lambda i: (i, 0, 0)), smem(lambda i: (jnp.minimum(i + 1, nt - 1), 0, 0)),
                  rows, pl.BlockSpec(memory_space=pl.ANY),
                  pl.BlockSpec((tm, LANES), lambda i: (i, 0)), vec, vec],
        out_specs=rows, out_shape=jax.ShapeDtypeStruct((t, D_MODEL), F32),
        scratch_shapes=[pltpu.VMEM((2, 2 * tm // SUBLANES, SUBLANES, D_MODEL), F32),
                        pltpu.SemaphoreType.DMA((2,))],
        compiler_params=_params("arbitrary"),
        name="moe_combine_ln")(ids, ids, x, ys.reshape(-1, SUBLANES, D_MODEL), gates, g, b)


def _moe_layer(x, rw_pad, w1, w3, w2, layer, g, b, tm=512):
    t = x.shape[0]
    eidx, gates, ranks, counts = _router(x, rw_pad)
    cnt = counts[0, :N_EXPERTS]
    padded = (cnt + tm - 1) // tm * tm
    pad_end = jnp.cumsum(padded)
    pad_start = pad_end - padded
    e2 = eidx[:, :2]
    dest = pad_start[e2] + ranks[:, :2]
    nblk = (2 * t) // tm + N_EXPERTS
    n_slots = nblk * tm
    tt = 512
    ids = dest.reshape(t // tt, tt, 2).transpose(0, 2, 1).reshape(t // tt, 1, 2 * tt)
    n_used = (pad_end[-1] // tm).astype(I32).reshape(1)
    blk = jnp.arange(nblk, dtype=I32) * tm
    block_e = jnp.minimum(jnp.sum(pad_end[None, :] <= blk[:, None], axis=1), N_EXPERTS - 1).astype(I32)
    block_e = jnp.where(jnp.arange(nblk) < n_used[0], block_e, block_e[jnp.maximum(n_used[0] - 1, 0)])
    unused = pad_end[-1] + jnp.arange(N_EXPERTS, dtype=I32) * tm
    fills = jnp.concatenate([pad_start + cnt, unused]).astype(I32)
    xs = _moe_dispatch(x, ids, fills, n_slots, tt, tm)
    ys = _moe_experts(block_e, n_used, xs, w1, w3, w2, layer, tm)
    return _combine_ln(x, ys, ids, gates, g, b, tt)


def _block_diag(w):
    g, c, _ = w.shape
    eye = jnp.eye(g, dtype=w.dtype)
    return (eye[:, None, :, None] * w[:, :, None, :]).reshape(g * c, g * c)


def kernel(x, mem, positions, w_in, conv_w, a_log, dt_bias, dn_norm_w, pool_w, pool_scale, w_out, ln_mix_g, ln_mix_b, xq_w, xk_w, xv_w, xo_w, ln_x_g, ln_x_b, ffn_w1, ffn_w3, ffn_w2, router_w, moe_w1, moe_w3, moe_w2, ln_ffn_g, ln_ffn_b):
    batch, seq, _ = x.shape
    t = batch * seq
    depth = w_in.shape[0]
    assert depth == DEPTH and x.shape[2] == D_MODEL and seq % (QB * max(DILATIONS)) == 0
    xf = x.reshape(t, D_MODEL)
    memf = mem.reshape(batch * mem.shape[1], D_MODEL)

    posc, posr = {}, {}
    for dil in DILATIONS:
        pt = positions.reshape(batch, seq // dil, dil).transpose(0, 2, 1)
        posc[dil] = pt[..., None]
        posr[dil] = pt[:, :, None, :]
    moe_b16 = [w.astype(BF16) for w in (moe_w1, moe_w3, moe_w2)]

    for l in range(depth):
        slabs, hr = _in_proj(xf, w_in, l, conv_w[l], seq)

        branches = [_att_branch(slabs, posc[d], posr[d], d, batch, seq) for d in DILATIONS]
        pvec = jnp.zeros((2, LANES), F32)
        pvec = pvec.at[0, DN_HEADS:2 * DN_HEADS].set(a_log[l]).at[1, DN_HEADS:2 * DN_HEADS].set(dt_bias[l])
        normw = jnp.tile(dn_norm_w[l], DN_HEADS)[None, :]
        dn = _delta_net(hr, pvec, normw, batch, seq)

        pool = _pool(hr, _block_diag(pool_w[l]).astype(BF16), pool_scale[l][None, :], batch, seq)

        wo = w_out[l].astype(BF16)
        xf = _proj_ln([o for o, _ in branches], [s for _, s in branches], dn, pool, xf, wo[:ATT_WIDTH],
                      wo[ATT_WIDTH:ATT_WIDTH + DN_WIDTH], wo[ATT_WIDTH + DN_WIDTH:],
                      ln_mix_g[l][None, :], ln_mix_b[l][None, :])

        wkv = jnp.concatenate([xk_w[l], xv_w[l]], axis=1).astype(BF16)
        kv = _matmul(memf, wkv, BF16, 512, 1024).reshape(batch, mem.shape[1], 2 * D_MODEL)
        xf = _cross_attention(xf, (xq_w[l] * (X_HEAD_DIM ** -0.5)).astype(BF16), kv,
                              xo_w[l].astype(BF16), ln_x_g[l][None, :],
                              ln_x_b[l][None, :], seq)

        j = l // 2
        gl, bl = ln_ffn_g[l][None, :], ln_ffn_b[l][None, :]
        if l % 2 == 0:
            xf = _ffn_dense(xf, ffn_w1[j].astype(BF16), ffn_w3[j].astype(BF16), ffn_w2[j].astype(BF16), gl, bl)
        else:
            rw_pad = jnp.pad(router_w[j], ((0, 0), (0, LANES - N_EXPERTS)))
            xf = _moe_layer(xf, rw_pad, *moe_b16, j, gl, bl)
    return xf.reshape(batch, seq, D_MODEL)
```

```python
import functools
import math

import jax
import jax.numpy as jnp
from jax import lax
from jax.experimental import pallas as pl
from jax.experimental.pallas import tpu as pltpu

F32, BF16, I32 = jnp.float32, jnp.bfloat16, jnp.int32

D_MODEL = 1024
HEAD_DIM = 64
ATT_HEADS = 6
ATT_WIDTH = ATT_HEADS * HEAD_DIM
DN_HEADS = 6
DN_WIDTH = DN_HEADS * HEAD_DIM
POOL_WIDTH = 256
POOL_WINDOWS = (2, 4, 8, 16)
POOL_GDIM = POOL_WIDTH // len(POOL_WINDOWS)
DILATIONS = (1, 4, 16)
CONV_K = 4
X_HEADS = 4
X_HEAD_DIM = D_MODEL // X_HEADS
N_EXPERTS = 8
DEPTH = 4
ALPHA = (2 * DEPTH) ** 0.25
LN_EPS = 1e-5
NORM_EPS = 1e-6

OFF_DN = 3 * ATT_WIDTH
OFF_BETA = OFF_DN + 3 * DN_WIDTH
OFF_DECAY = OFF_BETA + DN_HEADS
OFF_GATE = OFF_DECAY + DN_HEADS
OFF_POOL = OFF_GATE + DN_WIDTH

LANES = 128
MXU_WIDTH = 256
R_DN = 0
R_GATE = R_DN + 3 * DN_WIDTH
R_POOL = R_GATE + DN_WIDTH
R_SMALL = R_POOL + POOL_WIDTH
R_COLS = R_SMALL + LANES

QB = 128
ATT_STEP_TOKENS = 2048
ATT_MAX_ROWS = 512
NEG = -1e30
VMEM_LIMIT = 48 * 1024 * 1024


def _params(*sem):
    return pltpu.CompilerParams(dimension_semantics=sem, vmem_limit_bytes=VMEM_LIMIT)


def _sigmoid(x):
    return 1.0 / (1.0 + jnp.exp(-x))


def _layer_norm_rows(z, g, b):
    mu = jnp.mean(z, axis=-1, keepdims=True)
    zc = z - mu
    var = jnp.mean(zc * zc, axis=-1, keepdims=True)
    return zc * lax.rsqrt(var + LN_EPS) * g + b


def _rows(start, size, stride):
    return pl.ds(start, size) if stride == 1 else pl.ds(start, size, stride=stride)


def _mm_body(x_ref, w_ref, o_ref):
    o_ref[...] = jnp.dot(x_ref[...].astype(BF16), w_ref[...],
                         preferred_element_type=F32).astype(o_ref.dtype)


def _matmul(x, w, out_dtype, tm, tn):
    m, k = x.shape
    n = w.shape[1]
    return pl.pallas_call(
        _mm_body, grid=(m // tm, n // tn),
        in_specs=[pl.BlockSpec((tm, k), lambda i, j: (i, 0)),
                  pl.BlockSpec((k, tn), lambda i, j: (0, j))],
        out_specs=pl.BlockSpec((tm, tn), lambda i, j: (i, j)),
        out_shape=jax.ShapeDtypeStruct((m, n), out_dtype),
        compiler_params=_params("parallel", "arbitrary"), name="matmul")(x, w)


CONV_HALO = 8


def _in_proj_body(x_ref, w_ref, cw_ref, slab_ref, o_ref, wb, hist, cbuf, *, tm, tiles_per_seq):
    i = pl.program_id(0)
    ncv = 3 * DN_WIDTH

    @pl.when(i == 0)
    def _():
        wb[:, 0:OFF_BETA] = w_ref[:, 0:OFF_BETA].astype(BF16)
        c0 = OFF_DN + R_GATE
        wb[:, c0:c0 + DN_WIDTH] = w_ref[:, OFF_GATE:OFF_POOL].astype(BF16)
        c0 = OFF_DN + R_POOL
        wb[:, c0:c0 + POOL_WIDTH] = w_ref[:, OFF_POOL:].astype(BF16)
        c0 = OFF_DN + R_SMALL
        wb[:, c0:c0 + LANES] = jnp.zeros((wb.shape[0], LANES), BF16)
        wb[:, c0:c0 + 2 * DN_HEADS] = w_ref[:, OFF_BETA:OFF_GATE].astype(BF16)

    @pl.when(lax.rem(i, tiles_per_seq) == 0)
    def _():
        hist[...] = jnp.zeros_like(hist)

    xb = x_ref[...].astype(BF16)
    for j in range(slab_ref.shape[0]):
        if j % 2 == 0:
            res = jnp.dot(xb, wb[:, j * LANES:min(j + 2, slab_ref.shape[0]) * LANES],
                          preferred_element_type=F32)
        slab_ref[j] = res[:, (j % 2) * LANES:(j % 2 + 1) * LANES]

    n = o_ref.shape[1]
    starts = list(range(0, ncv, MXU_WIDTH)) + list(range(ncv, n, MXU_WIDTH))
    for c0, c1 in zip(starts, starts[1:] + [n]):
        cs = slice(c0, c1)
        res = jnp.dot(xb, wb[:, OFF_DN + c0:OFF_DN + c1], preferred_element_type=F32)
        if c0 >= ncv:
            o_ref[:, cs] = res
            continue
        cbuf[0:CONV_HALO, cs] = hist[:, cs]
        cbuf[CONV_HALO:, cs] = res
        hist[:, cs] = cbuf[tm:, cs]
        acc = None
        for j in range(CONV_K):
            start = CONV_HALO - (CONV_K - 1) + j
            t = cbuf[start:start + tm, cs] * cw_ref[j:j + 1, cs]
            acc = t if acc is None else acc + t
        o_ref[:, cs] = acc * _sigmoid(acc)


def _in_proj(x, w_in, layer, conv_w, seq, tm=512):
    m, k = x.shape
    nslab = OFF_DN // LANES
    return pl.pallas_call(
        functools.partial(_in_proj_body, tm=tm, tiles_per_seq=seq // tm), grid=(m // tm,),
        in_specs=[pl.BlockSpec((tm, k), lambda i: (i, 0)),
                  pl.BlockSpec((None,) + w_in.shape[1:], lambda i: (layer, 0, 0),
                               pipeline_mode=pl.Buffered(1)),
                  pl.BlockSpec(conv_w.shape, lambda i: (0, 0))],
        out_specs=[pl.BlockSpec((nslab, tm, LANES), lambda i: (0, i, 0)),
                   pl.BlockSpec((tm, R_COLS), lambda i: (i, 0))],
        out_shape=[jax.ShapeDtypeStruct((nslab, m, LANES), F32),
                   jax.ShapeDtypeStruct((m, R_COLS), F32)],
        scratch_shapes=[pltpu.VMEM((k, OFF_DN + R_COLS), BF16),
                        pltpu.VMEM((CONV_HALO, 3 * DN_WIDTH), F32),
                        pltpu.VMEM((tm + CONV_HALO, 3 * DN_WIDTH), F32)],
        compiler_params=_params("arbitrary"), name="in_proj_conv")(x, w_in, conv_w)


def _att_body(q_ref, kc_ref, vc_ref, pq_ref, pkc_ref, o_ref, lse_ref,
              kbuf, vbuf, pkbuf, kprev, vprev, pkprev, *, dil, rows, slopes):
    n = pl.program_id(1)
    nsub = rows // QB
    npair = ATT_HEADS // 2

    @pl.when(n == 0)
    def _():
        kprev[...] = jnp.zeros_like(kprev)
        vprev[...] = jnp.zeros_like(vprev)
        pkprev[...] = jnp.zeros_like(pkprev)

    a = lax.broadcasted_iota(I32, (QB, 2 * QB), 0)
    c = lax.broadcasted_iota(I32, (QB, 2 * QB), 1)
    tri = jnp.where(c < QB, c - a, a - (c - QB)) >= 0
    lane = lax.broadcasted_iota(I32, (QB, LANES), 1)
    low = lane < HEAD_DIM

    def residue(r, carry):
        for p in range(npair):
            kbuf[p, 0:QB, :] = kprev[r, p]
            kbuf[p, QB:, :] = kc_ref[p, _rows(r, rows, dil), :].astype(BF16)
            vbuf[p, 0:QB, :] = vprev[r, p]
            vbuf[p, QB:, :] = vc_ref[p, _rows(r, rows, dil), :].astype(BF16)
            kprev[r, p] = kbuf[p, rows:, :]
            vprev[r, p] = vbuf[p, rows:, :]
        pkbuf[0] = pkprev[r]
        for j in range(nsub):
            pkbuf[j + 1] = pkc_ref[r, :, j * QB:(j + 1) * QB]
        pkprev[r] = pkbuf[nsub]

        def sub(sb, carry2):
            r0 = pl.multiple_of(sb * QB, QB)
            t0 = r + r0 * dil
            pq = pq_ref[r, pl.ds(r0, QB), :]
            pk = jnp.concatenate([pkbuf[sb], pkbuf[sb + 1]], axis=1)
            dist = (pq - pk).astype(F32)
            mask = tri & (c >= jnp.where((sb > 0) | (n > 0), 0, QB))
            heads = range(ATT_HEADS)
            ks = [kbuf[p, pl.ds(r0, 2 * QB), :] for p in range(npair)]
            vs = [vbuf[p, pl.ds(r0, 2 * QB), :] for p in range(npair)]
            qs = [q_ref[p, _rows(t0, QB, dil), :] * (HEAD_DIM ** -0.5) for p in range(npair)]
            qms = [jnp.where(low if h % 2 == 0 else ~low, qs[h // 2], 0.0).astype(BF16) for h in heads]
            ss = [lax.dot_general(qms[h], ks[h // 2], (((1,), (1,)), ((), ())), preferred_element_type=F32)
                  for h in heads]
            ss = [jnp.where(mask, ss[h] - slopes[h] * dist, NEG) for h in heads]
            ms = [jnp.max(ss[h], axis=-1, keepdims=True) for h in heads]
            es = [jnp.exp(ss[h] - ms[h]) for h in heads]
            ls = [jnp.sum(es[h], axis=-1, keepdims=True) for h in heads]
            outs = [jnp.dot(es[h].astype(BF16), vs[h // 2], preferred_element_type=F32) / ls[h] for h in heads]
            lse = jnp.zeros((QB, LANES), F32)
            for h in heads:
                lse = jnp.where(lane == h, ms[h] + jnp.log(ls[h]), lse)
            for p in range(npair):
                o_ref[p, _rows(t0, QB, dil), :] = jnp.where(low, outs[2 * p], outs[2 * p + 1])
            lse_ref[_rows(t0, QB, dil), :] = lse
            return carry2

        lax.fori_loop(0, nsub, sub, 0, unroll=min(nsub, 2))
        return carry

    lax.fori_loop(0, dil, residue, 0)


def _att_branch(slabs, posc, posr, dil, batch, seq):
    sub_len = seq // dil
    rows = min(ATT_STEP_TOKENS // max(dil, ATT_STEP_TOKENS // ATT_MAX_ROWS), sub_len)
    tok = rows * dil
    nsteps = seq // tok
    npair = ATT_HEADS // 2
    slopes = tuple(2.0 ** (-8.0 * (i + 1) / ATT_HEADS) for i in range(ATT_HEADS))

    def cur(which):
        return pl.BlockSpec((npair, tok, LANES), lambda b, n: (which, b * nsteps + n, 0))

    t = batch * seq
    return pl.pallas_call(
        functools.partial(_att_body, dil=dil, rows=rows, slopes=slopes),
        grid=(batch, nsteps),
        in_specs=[cur(0), cur(1), cur(2),
                  pl.BlockSpec((None, dil, rows, 1), lambda b, n: (b, 0, n, 0)),
                  pl.BlockSpec((None, dil, 1, rows), lambda b, n: (b, 0, 0, n))],
        out_specs=[pl.BlockSpec((npair, tok, LANES), lambda b, n: (0, b * nsteps + n, 0)),
                   pl.BlockSpec((tok, LANES), lambda b, n: (b * nsteps + n, 0))],
        out_shape=[jax.ShapeDtypeStruct((npair, t, LANES), F32),
                   jax.ShapeDtypeStruct((t, LANES), F32)],
        scratch_shapes=[pltpu.VMEM((npair, rows + QB, LANES), BF16),
                        pltpu.VMEM((npair, rows + QB, LANES), BF16),
                        pltpu.VMEM((rows // QB + 1, 1, QB), I32),
                        pltpu.VMEM((dil, npair, QB, LANES), BF16),
                        pltpu.VMEM((dil, npair, QB, LANES), BF16),
                        pltpu.VMEM((dil, 1, QB), I32)],
        compiler_params=_params("arbitrary", "arbitrary"),
        name=f"dilated_att_d{dil}")(slabs, slabs, slabs, posc, posr)


def _merge_branches(os_, ls_):
    ls = [l[...] for l in ls_]
    m = functools.reduce(jnp.maximum, ls)
    ws = [jnp.exp(l - m) for l in ls]
    inv = 1.0 / functools.reduce(lambda a, b: a + b, ws)
    ws = [w * inv for w in ws]
    lane = lax.broadcasted_iota(I32, ls[0].shape, 1)
    low = lane < HEAD_DIM
    pairs = []
    for p in range(ATT_HEADS // 2):
        acc = None
        for w, o in zip(ws, os_):
            wp = jnp.where(low, w[:, 2 * p:2 * p + 1], w[:, 2 * p + 1:2 * p + 2])
            t = wp * o[p]
            acc = t if acc is None else acc + t
        pairs.append(acc)
    return jnp.concatenate(pairs, axis=1)


def _split2(a):
    hi = a.astype(BF16)
    lo = (a - hi.astype(F32)).astype(BF16)
    return hi, lo


def _dot(a, b):
    return jnp.dot(a, b, preferred_element_type=F32)


def _dot_bf16(a, b):
    return _dot(a.astype(BF16), b.astype(BF16))


def _dot_hp(a, b):
    ah, al = _split2(a)
    bh, bl = _split2(b)
    return _dot(jnp.concatenate([ah, al, ah], axis=1), jnp.concatenate([bh, bh, bl], axis=0))


def _dot_t(a, b):
    return lax.dot_general(a, b, (((1,), (1,)), ((), ())), preferred_element_type=F32)


def _dot_t_hp(a, b):
    ah, al = _split2(a)
    bh, bl = _split2(b)
    return _dot_t(jnp.concatenate([ah, al, ah], axis=1), jnp.concatenate([bh, bh, bl], axis=1))


def _dn_body(hq, hk, hv, small, gate, pvec, normw, out, s_ref, *, nch):
    n = pl.program_id(1)
    c_ = QB
    npair = DN_HEADS // 2
    qkv = (hq, hk, hv)

    @pl.when(n == 0)
    def _():
        s_ref[...] = jnp.zeros_like(s_ref)

    lane = lax.broadcasted_iota(I32, (c_, LANES), 1)
    low = lane < HEAD_DIM
    row = lax.broadcasted_iota(I32, (c_, c_), 0)
    col = lax.broadcasted_iota(I32, (c_, c_), 1)
    lower = row >= col
    strict = row > col
    eye = jnp.where(row == col, 1.0, 0.0)
    blockdiag = (row < HEAD_DIM) == (col < HEAD_DIM)

    def headsum(x):
        sa = jnp.sum(jnp.where(low, x, 0.0), axis=-1, keepdims=True)
        sb = jnp.sum(jnp.where(low, 0.0, x), axis=-1, keepdims=True)
        return jnp.where(low, sa, sb)

    def col_of(arr, j):
        return arr[:, j:j + 1]

    ltri = jnp.where(lower, 1.0, 0.0).astype(BF16)
    ltri3 = jnp.concatenate([ltri, ltri, ltri], axis=1)
    pair, xs, pws, qks = [], [], [], []
    for ch in range(nch):
        rs = slice(ch * c_, (ch + 1) * c_)
        sm = small[rs, :]
        beta_all = _sigmoid(sm)
        z = sm + pvec[1:2, :]
        softplus = jnp.maximum(z, 0.0) + jnp.log(1.0 + jnp.exp(-jnp.abs(z)))
        g_all = -jnp.exp(pvec[0:1, :]) * softplus
        g1 = g_all.astype(BF16)
        rem = g_all - g1.astype(F32)
        g2 = rem.astype(BF16)
        g3 = (rem - g2.astype(F32)).astype(BF16)
        gc_all = _dot(ltri3, jnp.concatenate([g1, g2, g3], axis=0))
        gc_t = gc_all.T

        for p in range(npair):
            cs = slice(p * LANES, (p + 1) * LANES)
            q, k, v = qkv[0][rs, cs], qkv[1][rs, cs], qkv[2][rs, cs]
            q = q * lax.rsqrt(headsum(q * q) + NORM_EPS) * (HEAD_DIM ** -0.5)
            k = k * lax.rsqrt(headsum(k * k) + NORM_EPS)
            beta = jnp.where(low, col_of(beta_all, 2 * p), col_of(beta_all, 2 * p + 1))
            gc = jnp.where(low, col_of(gc_all, DN_HEADS + 2 * p), col_of(gc_all, DN_HEADS + 2 * p + 1))
            eg = jnp.exp(gc)
            rhs = jnp.concatenate([v * beta, k * beta * eg], axis=1)
            pair.append(dict(q=q, k=k, kb=k.astype(BF16), gc=gc, eg=eg, rhs=rhs))

        for h in range(DN_HEADS):
            pr = pair[ch * npair + h // 2]
            sel = low if h % 2 == 0 else ~low
            gcol = col_of(gc_all, DN_HEADS + h)
            grow = gc_t[DN_HEADS + h:DN_HEADS + h + 1, :]
            dec = jnp.where(lower, jnp.exp(jnp.minimum(gcol - grow, 0.0)), 0.0)
            kk = _dot_t(jnp.where(sel, pr["k"], 0.0).astype(BF16), pr["kb"])
            qk = _dot_t_hp(jnp.where(sel, pr["q"], 0.0), pr["k"])
            nm = jnp.where(strict, col_of(beta_all, h) * kk * dec, 0.0)
            xs.append(eye - nm)
            pws.append(nm)
            qks.append((qk * dec).astype(BF16))

    for level in range(int(math.log2(c_)) - 1):
        mm = _dot_hp if level < 3 else _dot_bf16
        for i in range(len(xs)):
            pws[i] = mm(pws[i], pws[i])
        for i in range(len(xs)):
            xs[i] = xs[i] + mm(xs[i], pws[i])
    uws = [_dot_hp(xs[i], pair[i // 2]["rhs"]) for i in range(len(xs))]

    for ch in range(nch):
        rs = slice(ch * c_, (ch + 1) * c_)
        for p in range(npair):
            pr = pair[ch * npair + p]
            i0 = ch * DN_HEADS + 2 * p
            cs = slice(p * LANES, (p + 1) * LANES)
            u = jnp.where(low, uws[i0][:, :LANES], uws[i0 + 1][:, :LANES])
            w = jnp.where(low, uws[i0][:, LANES:], uws[i0 + 1][:, LANES:])
            state = s_ref[p]
            sb16 = state.astype(BF16)
            vnew = u - _dot(w.astype(BF16), sb16)
            vn16 = vnew.astype(BF16)
            o = _dot((pr["q"] * pr["eg"]).astype(BF16), sb16)
            o = o + jnp.where(low, _dot(qks[i0], vn16), _dot(qks[i0 + 1], vn16))
            glast = pr["gc"][c_ - 1:c_, :]
            kd = pr["k"] * jnp.exp(glast - pr["gc"])
            upd = _dot(kd.T.astype(BF16), vn16)
            s_ref[p] = state * jnp.exp(glast) + jnp.where(blockdiag, upd, 0.0)
            ms = headsum(o * o) * (1.0 / HEAD_DIM)
            gt = gate[rs, cs]
            out[rs, cs] = (o * lax.rsqrt(ms + NORM_EPS) * normw[:, cs] * (gt * _sigmoid(gt))).astype(out.dtype)


def _delta_net(hr, pvec, normw, batch, seq, nch=4):
    h3 = hr.reshape(batch, seq, R_COLS)
    rows = nch * QB
    qcol = R_DN // DN_WIDTH

    def cur(off):
        return pl.BlockSpec((None, rows, DN_WIDTH), lambda b, n: (b, n, qcol + off))

    out = pl.pallas_call(
        functools.partial(_dn_body, nch=nch), grid=(batch, seq // rows),
        in_specs=[cur(0), cur(1), cur(2),
                  pl.BlockSpec((None, rows, LANES), lambda b, n: (b, n, R_SMALL // LANES)),
                  pl.BlockSpec((None, rows, DN_WIDTH), lambda b, n: (b, n, R_GATE // DN_WIDTH)),
                  pl.BlockSpec((2, LANES), lambda b, n: (0, 0)),
                  pl.BlockSpec((1, DN_WIDTH), lambda b, n: (0, 0))],
        out_specs=pl.BlockSpec((None, rows, DN_WIDTH), lambda b, n: (b, n, 0)),
        out_shape=jax.ShapeDtypeStruct((batch, seq, DN_WIDTH), BF16),
        scratch_shapes=[pltpu.VMEM((DN_HEADS // 2, LANES, LANES), F32)],
        compiler_params=_params("parallel", "arbitrary"),
        name="gated_delta_rule")(h3, h3, h3, h3, h3, pvec, normw)
    return out.reshape(batch * seq, DN_WIDTH)


def _pool_body(cur, prev, wbd, scale, out, xbuf, *, rows):
    n = pl.program_id(1)
    halo = 16
    xbuf[0:halo, :] = prev[...] * jnp.where(n > 0, 1.0, 0.0)
    xbuf[halo:, :] = cur[...]
    u = cur[...]
    lane = lax.broadcasted_iota(I32, (rows, POOL_WIDTH), 1)
    tpos = n * rows + lax.broadcasted_iota(I32, (rows, POOL_WIDTH), 0)
    run = u
    pooled = jnp.zeros_like(u)
    for j in range(1, max(POOL_WINDOWS)):
        run = run + xbuf[halo - j:halo - j + rows, :]
        if j + 1 in POOL_WINDOWS:
            gi = POOL_WINDOWS.index(j + 1)
            cnt = jnp.minimum(tpos + 1, j + 1).astype(F32)
            sel = (lane >= gi * POOL_GDIM) & (lane < (gi + 1) * POOL_GDIM)
            pooled = jnp.where(sel, run / cnt - u, pooled)
    mixed = jnp.dot(pooled.astype(BF16), wbd[...], preferred_element_type=F32)
    out[...] = (mixed * scale[...]).astype(out.dtype)


def _pool(hr, wbd, scale, batch, seq, rows=512):
    h3 = hr.reshape(batch, seq, R_COLS)
    out = pl.pallas_call(
        functools.partial(_pool_body, rows=rows), grid=(batch, seq // rows),
        in_specs=[pl.BlockSpec((None, rows, POOL_WIDTH), lambda b, n: (b, n, R_POOL // POOL_WIDTH)),
                  pl.BlockSpec((None, 16, POOL_WIDTH),
                               lambda b, n: (b, jnp.maximum(n * (rows // 16) - 1, 0), R_POOL // POOL_WIDTH)),
                  pl.BlockSpec((POOL_WIDTH, POOL_WIDTH), lambda b, n: (0, 0)),
                  pl.BlockSpec((1, POOL_WIDTH), lambda b, n: (0, 0))],
        out_specs=pl.BlockSpec((None, rows, POOL_WIDTH), lambda b, n: (b, n, 0)),
        out_shape=jax.ShapeDtypeStruct((batch, seq, POOL_WIDTH), BF16),
        scratch_shapes=[pltpu.VMEM((rows + 16, POOL_WIDTH), F32)],
        compiler_params=_params("parallel", "arbitrary"), name="multiscale_pool")(h3, h3, wbd, scale)
    return out.reshape(batch * seq, POOL_WIDTH)


def _proj_ln_body(o1, o2, o3, l1, l2, l3, d, p, x, wa, wd, wp, g, b, o):
    att = _merge_branches((o1, o2, o3), (l1, l2, l3)).astype(BF16)
    y = jnp.dot(att, wa[...], preferred_element_type=F32)
    y = y + jnp.dot(d[...], wd[...], preferred_element_type=F32)
    y = y + jnp.dot(p[...], wp[...], preferred_element_type=F32)
    o[...] = _layer_norm_rows(ALPHA * x[...] + y, g[...], b[...])


def _proj_ln(branch_o, branch_lse, dn, pool, x, wa, wd, wp, g, b, tm=1024):
    t = x.shape[0]
    npair = ATT_HEADS // 2

    def rows(wd_):
        return pl.BlockSpec((tm, wd_), lambda i: (i, 0))

    def full(a_):
        return pl.BlockSpec(a_.shape, lambda i: (0, 0))

    slab = pl.BlockSpec((npair, tm, LANES), lambda i: (0, i, 0))
    return pl.pallas_call(
        _proj_ln_body, grid=(t // tm,),
        in_specs=[slab] * 3 + [rows(LANES)] * 3 + [rows(DN_WIDTH), rows(POOL_WIDTH), rows(D_MODEL),
                                                    full(wa), full(wd), full(wp), full(g), full(b)],
        out_specs=rows(D_MODEL), out_shape=jax.ShapeDtypeStruct((t, D_MODEL), F32),
        compiler_params=_params("parallel"),
        name="out_proj_ln")(*branch_o, *branch_lse, dn, pool, x, wa, wd, wp, g, b)


def _xattn_body(x, wq, k, v, wo, g, b, o):
    xv = x[...]
    q = jnp.dot(xv.astype(BF16), wq[...], preferred_element_type=F32).astype(BF16)
    cols = [slice(h * X_HEAD_DIM, (h + 1) * X_HEAD_DIM) for h in range(X_HEADS)]
    ss = [_dot_t(q[:, cs], k[:, cs]) for cs in cols]
    ms = [jnp.max(s, axis=-1, keepdims=True) for s in ss]
    es = [jnp.exp(s - m) for s, m in zip(ss, ms)]
    ls = [jnp.sum(e, axis=-1, keepdims=True) for e in es]
    heads = [(_dot(e.astype(BF16), v[:, cs]) / l).astype(BF16) for e, l, cs in zip(es, ls, cols)]
    oc = jnp.concatenate(heads, axis=1)
    y = jnp.dot(oc, wo[...], preferred_element_type=F32)
    o[...] = _layer_norm_rows(ALPHA * xv + y, g[...], b[...])


def _cross_attention(x, wq, kv, wo, g, b, seq, tm=1024):
    t = x.shape[0]
    mem_len = kv.shape[1]
    per_batch = seq // tm

    def full(a_):
        return pl.BlockSpec(a_.shape, lambda i: (0, 0))

    def half(which):
        return pl.BlockSpec((None, mem_len, D_MODEL), lambda i: (i // per_batch, 0, which))

    rows = pl.BlockSpec((tm, D_MODEL), lambda i: (i, 0))
    return pl.pallas_call(
        _xattn_body, grid=(t // tm,),
        in_specs=[rows, full(wq), half(0), half(1), full(wo), full(g), full(b)],
        out_specs=rows, out_shape=jax.ShapeDtypeStruct((t, D_MODEL), F32),
        compiler_params=_params("parallel"), name="cross_attention_ln")(x, wq, kv, kv, wo, g, b)


def _ffn_body(x, w1, w3, w2, g, b, o, *, chunks):
    xv = x[...]
    xb = xv.astype(BF16)
    y = None
    for c0, c1 in chunks:
        h1 = jnp.dot(xb, w1[:, c0:c1], preferred_element_type=F32)
        h3 = jnp.dot(xb, w3[:, c0:c1], preferred_element_type=F32)
        hh = (h1 * _sigmoid(h1) * h3).astype(BF16)
        t = jnp.dot(hh, w2[c0:c1, :], preferred_element_type=F32)
        y = t if y is None else y + t
    o[...] = _layer_norm_rows(ALPHA * xv + y, g[...], b[...])


def _ffn_dense(x, w1, w3, w2, g, b, tm=512, chunk=6 * MXU_WIDTH):
    t = x.shape[0]
    dff = w1.shape[1]
    chunks = tuple((c0, min(c0 + chunk, dff)) for c0 in range(0, dff, chunk))
    rows = pl.BlockSpec((tm, D_MODEL), lambda i: (i, 0))
    vec = pl.BlockSpec((1, D_MODEL), lambda i: (0, 0))

    def resident(a_):
        return pl.BlockSpec(a_.shape, lambda i: (0, 0), pipeline_mode=pl.Buffered(1))

    return pl.pallas_call(
        functools.partial(_ffn_body, chunks=chunks), grid=(t // tm,),
        in_specs=[rows, resident(w1), resident(w3), resident(w2), vec, vec],
        out_specs=rows, out_shape=jax.ShapeDtypeStruct((t, D_MODEL), F32),
        compiler_params=_params("parallel"), name="ffn_swiglu_ln")(x, w1, w3, w2, g, b)


def _router_body(x, rw, eidx, gates, ranks, counts, carry, *, tm):
    i = pl.program_id(0)

    @pl.when(i == 0)
    def _():
        carry[...] = jnp.zeros_like(carry)

    logits = _dot_hp(x[...], rw[...])
    lane = lax.broadcasted_iota(I32, (tm, LANES), 1)
    lanef = lane.astype(F32)
    lg = jnp.where(lane < N_EXPERTS, logits, NEG)
    m1 = jnp.max(lg, axis=-1, keepdims=True)
    i1 = jnp.min(jnp.where(lg == m1, lanef, float(LANES)), axis=-1, keepdims=True)
    lg2 = jnp.where(lanef == i1, NEG, lg)
    m2 = jnp.max(lg2, axis=-1, keepdims=True)
    i2 = jnp.min(jnp.where(lg2 == m2, lanef, float(LANES)), axis=-1, keepdims=True)
    e21 = jnp.exp(m2 - m1)
    g1 = 1.0 / (1.0 + e21)
    g2 = e21 * g1
    hot1 = lanef == i1
    hot2 = lanef == i2
    onehot = jnp.where(hot1 | hot2, 1.0, 0.0)
    r = lax.broadcasted_iota(I32, (tm, tm), 0)
    c = lax.broadcasted_iota(I32, (tm, tm), 1)
    before = jnp.dot(jnp.where(r > c, 1.0, 0.0).astype(BF16), onehot.astype(BF16),
                     preferred_element_type=F32) + carry[...]
    rank1 = jnp.sum(jnp.where(hot1, before, 0.0), axis=-1, keepdims=True)
    rank2 = jnp.sum(jnp.where(hot2, before, 0.0), axis=-1, keepdims=True)
    carry[...] = carry[...] + jnp.sum(onehot, axis=0, keepdims=True)
    eidx[...] = jnp.where(lane == 0, i1, jnp.where(lane == 1, i2, 0.0)).astype(I32)
    gates[...] = jnp.where(lane == 0, g1, jnp.where(lane == 1, g2, 0.0))
    ranks[...] = jnp.where(lane == 0, rank1, jnp.where(lane == 1, rank2, 0.0)).astype(I32)
    counts[...] = jnp.broadcast_to(carry[...], counts.shape).astype(I32)


def _router(x, rw, tm=512):
    t = x.shape[0]
    rows = pl.BlockSpec((tm, LANES), lambda i: (i, 0))
    return pl.pallas_call(
        functools.partial(_router_body, tm=tm), grid=(t // tm,),
        in_specs=[pl.BlockSpec((tm, D_MODEL), lambda i: (i, 0)),
                  pl.BlockSpec((D_MODEL, LANES), lambda i: (0, 0))],
        out_specs=[rows, rows, rows, pl.BlockSpec((8, LANES), lambda i: (0, 0))],
        out_shape=[jax.ShapeDtypeStruct((t, LANES), I32), jax.ShapeDtypeStruct((t, LANES), F32),
                   jax.ShapeDtypeStruct((t, LANES), I32), jax.ShapeDtypeStruct((8, LANES), I32)],
        scratch_shapes=[pltpu.VMEM((1, LANES), F32)],
        compiler_params=_params("arbitrary"), name="router_top2")(x, rw)


def _moe_body(be, nused, xs, w1, w3, w2, ys, xb, acc):
    i = pl.program_id(0)
    f = pl.program_id(1)
    last = f == pl.num_programs(1) - 1
    used = i < nused[0]

    @pl.when(used & (f == 0))
    def _():
        xb[...] = xs[...].astype(BF16)
        acc[...] = jnp.zeros_like(acc)

    @pl.when(used)
    def _():
        xv = xb[...]
        h1 = jnp.dot(xv, w1[...], preferred_element_type=F32)
        h3 = jnp.dot(xv, w3[...], preferred_element_type=F32)
        hh = (h1 * _sigmoid(h1) * h3).astype(BF16)
        acc[...] += jnp.dot(hh, w2[...], preferred_element_type=F32)

    @pl.when(used & last)
    def _():
        ys[...] = acc[...]

    @pl.when(jnp.logical_not(used) & last)
    def _():
        ys[...] = jnp.zeros_like(ys)


def _moe_experts(block_e, n_used, xs, w1, w3, w2, layer, tm, tf=1792):
    n_slots = xs.shape[0]
    dex = w1.shape[3]
    nblk = n_slots // tm
    nf = dex // tf

    def live(i, nu):
        return jnp.minimum(i, nu[0] - 1)

    def chunk(i, f, nu):
        return jnp.where(i < nu[0], f, nf - 1)

    grid_spec = pltpu.PrefetchScalarGridSpec(
        num_scalar_prefetch=2, grid=(nblk, nf),
        in_specs=[pl.BlockSpec((tm, D_MODEL), lambda i, f, be, nu: (live(i, nu), 0)),
                  pl.BlockSpec((None, None, D_MODEL, tf), lambda i, f, be, nu: (layer, be[i], 0, chunk(i, f, nu))),
                  pl.BlockSpec((None, None, D_MODEL, tf), lambda i, f, be, nu: (layer, be[i], 0, chunk(i, f, nu))),
                  pl.BlockSpec((None, None, tf, D_MODEL), lambda i, f, be, nu: (layer, be[i], chunk(i, f, nu), 0))],
        out_specs=pl.BlockSpec((tm, D_MODEL), lambda i, f, be, nu: (i, 0)),
        scratch_shapes=[pltpu.VMEM((tm, D_MODEL), BF16), pltpu.VMEM((tm, D_MODEL), F32)])
    return pl.pallas_call(
        _moe_body, grid_spec=grid_spec, out_shape=jax.ShapeDtypeStruct((n_slots, D_MODEL), F32),
        compiler_params=_params("arbitrary", "arbitrary"), name="moe_experts")(block_e, n_used, xs, w1, w3, w2)


SUBLANES = 8


def _tile_row(ref3, idx):
    tile = lax.shift_right_logical(idx, SUBLANES.bit_length() - 1)
    return ref3.at[tile, pl.ds(idx & (SUBLANES - 1), 1), :]


def _start_row_gather(src, ids, buf, sem, slot, nrows):
    def issue(j, carry):
        for u in range(SUBLANES):
            idx = ids[0, j * SUBLANES + u]
            pltpu.make_async_copy(_tile_row(src, idx), buf.at[slot, j, pl.ds(u, 1), :], sem.at[slot]).start()
        return carry

    lax.fori_loop(0, nrows // SUBLANES, issue, 0)


def _wait_row_gather(src, buf, sem, slot, nrows):
    pltpu.make_async_copy(src.at[pl.ds(0, nrows // SUBLANES)], buf.at[slot], sem.at[slot]).wait()


def _pipelined_row_gather(src, ids_cur, ids_next, buf, sem, nrows):
    i = pl.program_id(0)
    slot = lax.rem(i, 2)

    @pl.when(i == 0)
    def _():
        _start_row_gather(src, ids_cur, buf, sem, 0, nrows)

    @pl.when(i + 1 < pl.num_programs(0))
    def _():
        _start_row_gather(src, ids_next, buf, sem, 1 - slot, nrows)

    _wait_row_gather(src, buf, sem, slot, nrows)
    return slot


def _dispatch_body(tails, ids, x, xs, zbuf, sem, *, tm):
    fill = zbuf.shape[0]

    @pl.when(pl.program_id(0) == 0)
    def _():
        zbuf[...] = jnp.zeros_like(zbuf)

        def tail_fill(e):
            t0 = jnp.minimum(lax.shift_right_logical(tails[e], 3), xs.shape[0] - fill)
            return pltpu.make_async_copy(zbuf, xs.at[pl.ds(t0, fill)], sem)

        for e in range(tails.shape[0]):
            tail_fill(e).start()
            tail_fill(e).wait()

    def issue(j, carry):
        for u in range(SUBLANES):
            for k in range(2):
                dst = _tile_row(xs, ids[0, k * tm + j * SUBLANES + u])
                pltpu.make_async_copy(x.at[j, pl.ds(u, 1), :], dst, sem).start()
        return carry

    lax.fori_loop(0, tm // SUBLANES, issue, 0)
    for _ in range(2):
        pltpu.make_async_copy(x, xs.at[pl.ds(0, tm // SUBLANES)], sem).wait()


def _moe_dispatch(x, ids, tails, n_slots, tm, max_tail):
    t = x.shape[0]
    grid_spec = pltpu.PrefetchScalarGridSpec(
        num_scalar_prefetch=1, grid=(t // tm,),
        in_specs=[pl.BlockSpec((None, 1, 2 * tm), lambda i, tl: (i, 0, 0), memory_space=pltpu.SMEM),
                  pl.BlockSpec((tm // SUBLANES, SUBLANES, D_MODEL), lambda i, tl: (i, 0, 0))],
        out_specs=pl.BlockSpec(memory_space=pl.ANY),
        scratch_shapes=[pltpu.VMEM((max_tail // SUBLANES + 1, SUBLANES, D_MODEL), F32),
                        pltpu.SemaphoreType.DMA(())])
    xs = pl.pallas_call(
        functools.partial(_dispatch_body, tm=tm), grid_spec=grid_spec,
        out_shape=jax.ShapeDtypeStruct((n_slots // SUBLANES, SUBLANES, D_MODEL), F32),
        compiler_params=_params("arbitrary"),
        name="moe_dispatch_scatter")(tails, ids, x.reshape(t // SUBLANES, SUBLANES, D_MODEL))
    return xs.reshape(n_slots, D_MODEL)


def _combine_ln_body(ids_cur, ids_next, x, ys, gates, g, b, o, buf, sem, *, tm):
    slot = _pipelined_row_gather(ys, ids_cur, ids_next, buf, sem, 2 * tm)
    gt = gates[...]
    nt = tm // SUBLANES
    ya = buf[slot, 0:nt].reshape(tm, D_MODEL)
    yb = buf[slot, nt:].reshape(tm, D_MODEL)
    o[...] = _layer_norm_rows(ALPHA * x[...] + (ya * gt[:, 0:1] + yb * gt[:, 1:2]), g[...], b[...])


def _combine_ln(x, ys, ids, gates, g, b, tm):
    t = x.shape[0]
    nt = t // tm
    smem = functools.partial(pl.BlockSpec, (None, 1, 2 * tm), memory_space=pltpu.SMEM)
    rows = pl.BlockSpec((tm, D_MODEL), lambda i: (i, 0))
    vec = pl.BlockSpec((1, D_MODEL), lambda i: (0, 0))
    return pl.pallas_call(
        functools.partial(_combine_ln_body, tm=tm), grid=(nt,),
        in_specs=[smem(lambda i: (i, 0, 0)), smem(lambda i: (jnp.minimum(i + 1, nt - 1), 0, 0)),
                  rows, pl.BlockSpec(memory_space=pl.ANY),
                  pl.BlockSpec((tm, LANES), lambda i: (i, 0)), vec, vec],
        out_specs=rows, out_shape=jax.ShapeDtypeStruct((t, D_MODEL), F32),
        scratch_shapes=[pltpu.VMEM((2, 2 * tm // SUBLANES, SUBLANES, D_MODEL), F32),
                        pltpu.SemaphoreType.DMA((2,))],
        compiler_params=_params("arbitrary"),
        name="moe_combine_ln")(ids, ids, x, ys.reshape(-1, SUBLANES, D_MODEL), gates, g, b)


def _moe_layer(x, rw_pad, w1, w3, w2, layer, g, b, tm=512):
    t = x.shape[0]
    eidx, gates, ranks, counts = _router(x, rw_pad)
    cnt = counts[0, :N_EXPERTS]
    padded = (cnt + tm - 1) // tm * tm
    pad_end = jnp.cumsum(padded)
    pad_start = pad_end - padded
    e2 = eidx[:, :2]
    dest = pad_start[e2] + ranks[:, :2]
    nblk = (2 * t) // tm + N_EXPERTS
    n_slots = nblk * tm
    tt = 512
    ids = dest.reshape(t // tt, tt, 2).transpose(0, 2, 1).reshape(t // tt, 1, 2 * tt)
    n_used = (pad_end[-1] // tm).astype(I32).reshape(1)
    blk = jnp.arange(nblk, dtype=I32) * tm
    block_e = jnp.minimum(jnp.sum(pad_end[None, :] <= blk[:, None], axis=1), N_EXPERTS - 1).astype(I32)
    block_e = jnp.where(jnp.arange(nblk) < n_used[0], block_e, block_e[jnp.maximum(n_used[0] - 1, 0)])
    unused = pad_end[-1] + jnp.arange(N_EXPERTS, dtype=I32) * tm
    fills = jnp.concatenate([pad_start + cnt, unused]).astype(I32)
    xs = _moe_dispatch(x, ids, fills, n_slots, tt, tm)
    ys = _moe_experts(block_e, n_used, xs, w1, w3, w2, layer, tm)
    return _combine_ln(x, ys, ids, gates, g, b, tt)


def _block_diag(w):
    g, c, _ = w.shape
    eye = jnp.eye(g, dtype=w.dtype)
    return (eye[:, None, :, None] * w[:, :, None, :]).reshape(g * c, g * c)


def kernel(x, mem, positions, w_in, conv_w, a_log, dt_bias, dn_norm_w, pool_w, pool_scale, w_out, ln_mix_g, ln_mix_b, xq_w, xk_w, xv_w, xo_w, ln_x_g, ln_x_b, ffn_w1, ffn_w3, ffn_w2, router_w, moe_w1, moe_w3, moe_w2, ln_ffn_g, ln_ffn_b):
    batch, seq, _ = x.shape
    t = batch * seq
    depth = w_in.shape[0]
    assert depth == DEPTH and x.shape[2] == D_MODEL and seq % (QB * max(DILATIONS)) == 0
    xf = x.reshape(t, D_MODEL)
    memf = mem.reshape(batch * mem.shape[1], D_MODEL)

    posc, posr = {}, {}
    for dil in DILATIONS:
        pt = positions.reshape(batch, seq // dil, dil).transpose(0, 2, 1)
        posc[dil] = pt[..., None]
        posr[dil] = pt[:, :, None, :]
    moe_b16 = [w.astype(BF16) for w in (moe_w1, moe_w3, moe_w2)]

    for l in range(depth):
        slabs, hr = _in_proj(xf, w_in, l, conv_w[l], seq)

        branches = [_att_branch(slabs, posc[d], posr[d], d, batch, seq) for d in DILATIONS]
        pvec = jnp.zeros((2, LANES), F32)
        pvec = pvec.at[0, DN_HEADS:2 * DN_HEADS].set(a_log[l]).at[1, DN_HEADS:2 * DN_HEADS].set(dt_bias[l])
        normw = jnp.tile(dn_norm_w[l], DN_HEADS)[None, :]
        dn = _delta_net(hr, pvec, normw, batch, seq)

        pool = _pool(hr, _block_diag(pool_w[l]).astype(BF16), pool_scale[l][None, :], batch, seq)

        wo = w_out[l].astype(BF16)
        xf = _proj_ln([o for o, _ in branches], [s for _, s in branches], dn, pool, xf, wo[:ATT_WIDTH],
                      wo[ATT_WIDTH:ATT_WIDTH + DN_WIDTH], wo[ATT_WIDTH + DN_WIDTH:],
                      ln_mix_g[l][None, :], ln_mix_b[l][None, :])

        wkv = jnp.concatenate([xk_w[l], xv_w[l]], axis=1).astype(BF16)
        kv = _matmul(memf, wkv, BF16, 512, 1024).reshape(batch, mem.shape[1], 2 * D_MODEL)
        xf = _cross_attention(xf, (xq_w[l] * (X_HEAD_DIM ** -0.5)).astype(BF16), kv,
                              xo_w[l].astype(BF16), ln_x_g[l][None, :],
                              ln_x_b[l][None, :], seq)

        j = l // 2
        gl, bl = ln_ffn_g[l][None, :], ln_ffn_b[l][None, :]
        if l % 2 == 0:
            xf = _ffn_dense(xf, ffn_w1[j].astype(BF16), ffn_w3[j].astype(BF16), ffn_w2[j].astype(BF16), gl, bl)
        else:
            rw_pad = jnp.pad(router_w[j], ((0, 0), (0, LANES - N_EXPERTS)))
            xf = _moe_layer(xf, rw_pad, *moe_b16, j, gl, bl)
    return xf.reshape(batch, seq, D_MODEL)
```
